```python
import math
import jax
import jax.numpy as jnp
from jax import lax
import numpy as np

D_MODEL = 4096
BATCH = 2
SEQ = 4096
DEPTH = 2

GRID_W = 64
CTX_LEN = 256
NORM_EPS = 1e-6
N_MOD = 6
NEG_INF = -1e30

HEAD_DIM = 128
ATTN_HEADS = D_MODEL // 2 // HEAD_DIM
ATTN_KV_HEADS = ATTN_HEADS // 4
GQ = ATTN_HEADS // ATTN_KV_HEADS
WINDOW = 128
ATTN_BLOCK = 128
ROPE_THETA = 10000.0
ATTN_WIDTH = ATTN_HEADS * HEAD_DIM
KV_WIDTH = ATTN_KV_HEADS * HEAD_DIM

SGU_GROUPS = 16
SGU_WIDTH = D_MODEL // 2
SGU_GROUP_DIM = SGU_WIDTH // SGU_GROUPS
SGU_CHUNK = 128

EVEN_IN_SPLITS = (ATTN_WIDTH, ATTN_WIDTH + KV_WIDTH, ATTN_WIDTH + 2 * KV_WIDTH, ATTN_WIDTH + 2 * KV_WIDTH + SGU_WIDTH)
EVEN_IN_WIDTH = ATTN_WIDTH + 2 * KV_WIDTH + 2 * SGU_WIDTH
EVEN_OUT_WIDTH = ATTN_WIDTH + SGU_WIDTH

HYENA_WIDTH = D_MODEL
HYENA_SHORT = 3
HYENA_BANDS = 16
HYENA_EMB = 1 + 2 * HYENA_BANDS
HYENA_FILTER_HIDDEN = 64
HYENA_TARGET = 1e-2
HYENA_FAST_DECAY_PCT = 0.3
HYENA_SLOW_DECAY_PCT = 1.5

N_EXPERTS = 16
EXPERT_FF = D_MODEL // 4
EC_CAPACITY_FACTOR = 2

N_EVEN = (DEPTH + 1) // 2
N_ODD = DEPTH // 2

kernel_name = 'hybrid_window_gmlp_hyena_ec_moe_block'


def rms_norm(x, g):
    xf = x.astype(jnp.float32)
    y = xf * lax.rsqrt(jnp.mean(xf * xf, axis=-1, keepdims=True) + NORM_EPS)
    return (y * g.astype(jnp.float32)).astype(x.dtype)


def modulate(x, shift, scale):
    return x * (1.0 + scale) + shift


def grid_positions(L):
    rows = L // GRID_W
    row = jnp.repeat(jnp.arange(rows, dtype=jnp.int32), GRID_W)
    col = jnp.tile(jnp.arange(GRID_W, dtype=jnp.int32), rows)
    return row, col


def rope_axis(x, pos):
    d = x.shape[-1]
    inv = ROPE_THETA ** (-jnp.arange(0, d, 2, dtype=jnp.float32) / d)
    ang = pos.astype(jnp.float32)[:, None] * inv[None, :]
    cos = jnp.cos(ang)[None, :, None, :]
    sin = jnp.sin(ang)[None, :, None, :]
    xf = x.astype(jnp.float32)
    x1, x2 = xf[..., : d // 2], xf[..., d // 2:]
    return jnp.concatenate([x1 * cos - x2 * sin, x2 * cos + x1 * sin], axis=-1).astype(x.dtype)


def rope_2d(x, row, col):
    half = x.shape[-1] // 2
    return jnp.concatenate([rope_axis(x[..., :half], row), rope_axis(x[..., half:], col)], axis=-1)


def even_project(a, w_in):
    B, L = a.shape[0], a.shape[1]
    q, k, v, u, z = jnp.split(a @ w_in, EVEN_IN_SPLITS, axis=-1)
    q = q.reshape(B, L, ATTN_HEADS, HEAD_DIM)
    k = k.reshape(B, L, ATTN_KV_HEADS, HEAD_DIM)
    v = v.reshape(B, L, ATTN_KV_HEADS, HEAD_DIM)
    return q, k, v, u, z


def project_kv(a, w_in):
    B, L = a.shape[0], a.shape[1]
    kv = a @ w_in[:, ATTN_WIDTH:ATTN_WIDTH + 2 * KV_WIDTH]
    k, v = jnp.split(kv, 2, axis=-1)
    return k.reshape(B, L, ATTN_KV_HEADS, HEAD_DIM), v.reshape(B, L, ATTN_KV_HEADS, HEAD_DIM)


def _band(t, nb):
    B = t.shape[0]
    tb = t.reshape(B, nb, ATTN_BLOCK, ATTN_KV_HEADS, HEAD_DIM)
    tp = jnp.pad(tb, ((0, 0), (1, 1), (0, 0), (0, 0), (0, 0)))
    return jnp.concatenate([tp[:, :-2], tp[:, 1:-1], tp[:, 2:]], axis=2)


def windowed_gqa_with_context(q, k, v, kc, vc, sink):
    B, L = q.shape[0], q.shape[1]
    Lc = kc.shape[1]
    nb = L // ATTN_BLOCK
    scale = HEAD_DIM ** -0.5
    qb = q.reshape(B, nb, ATTN_BLOCK, ATTN_KV_HEADS, GQ, HEAD_DIM)
    kb, vb = _band(k, nb), _band(v, nb)
    s_loc = jnp.einsum('bnqhgd,bnkhd->bnhgqk', qb, kb).astype(jnp.float32) * scale
    q_off = jnp.arange(ATTN_BLOCK)
    k_off = jnp.arange(3 * ATTN_BLOCK) - ATTN_BLOCK
    k_abs = jnp.arange(nb)[:, None] * ATTN_BLOCK + k_off[None, :]
    valid = (jnp.abs(k_off[None, :] - q_off[:, None]) <= WINDOW)[None] & ((k_abs >= 0) & (k_abs < L))[:, None, :]
    s_loc = jnp.where(valid[None, :, None, None], s_loc, NEG_INF)
    s_ctx = jnp.einsum('bnqhgd,bchd->bnhgqc', qb, kc).astype(jnp.float32) * scale
    sink_col = jnp.broadcast_to(sink.astype(jnp.float32).reshape(ATTN_KV_HEADS, GQ)[None, None, :, :, None, None], s_loc.shape[:-1] + (1,))
    p = jax.nn.softmax(jnp.concatenate([s_loc, s_ctx, sink_col], axis=-1), axis=-1)
    n_loc = 3 * ATTN_BLOCK
    p_loc = p[..., :n_loc].astype(v.dtype)
    p_ctx = p[..., n_loc:n_loc + Lc].astype(v.dtype)
    o = jnp.einsum('bnhgqk,bnkhd->bnqhgd', p_loc, vb) + jnp.einsum('bnhgqc,bchd->bnqhgd', p_ctx, vc)
    return o.reshape(B, L, ATTN_WIDTH)


def context_gqa(qc, kc, vc, sink):
    B, Lc = qc.shape[0], qc.shape[1]
    qg = qc.reshape(B, Lc, ATTN_KV_HEADS, GQ, HEAD_DIM)
    s = jnp.einsum('bqhgd,bkhd->bhgqk', qg, kc).astype(jnp.float32) * HEAD_DIM ** -0.5
    sink_col = jnp.broadcast_to(sink.astype(jnp.float32).reshape(ATTN_KV_HEADS, GQ)[None, :, :, None, None], s.shape[:-1] + (1,))
    p = jax.nn.softmax(jnp.concatenate([s, sink_col], axis=-1), axis=-1)[..., :Lc]
    o = jnp.einsum('bhgqk,bkhd->bqhgd', p.astype(vc.dtype), vc)
    return o.reshape(B, Lc, ATTN_WIDTH)


def spatial_gating(u, z, g, w_s, b_s):
    B, L = u.shape[0], u.shape[1]
    u = jax.nn.gelu(u, approximate=False)
    z = jax.nn.gelu(z, approximate=False)
    zf = z.astype(jnp.float32).reshape(B, L, SGU_GROUPS, SGU_GROUP_DIM)
    mu = jnp.mean(zf, axis=-1, keepdims=True)
    var = jnp.mean(jnp.square(zf - mu), axis=-1, keepdims=True)
    zn = ((zf - mu) * lax.rsqrt(var + NORM_EPS) * g.astype(jnp.float32).reshape(SGU_GROUPS, SGU_GROUP_DIM)).astype(z.dtype)
    zc = zn.reshape(B, L // SGU_CHUNK, SGU_CHUNK, SGU_GROUPS, SGU_GROUP_DIM)
    mixed = jnp.einsum('gpq,bnqgc->bnpgc', w_s, zc) + jnp.transpose(b_s)[None, None, :, :, None]
    return u * mixed.reshape(B, L, SGU_WIDTH)


def attn_sgu_mixer(a_lat, a_ctx, row, col, w_in, sink, sgu_g, w_s, b_s, w_out, need_ctx_out):
    q, k, v, u, z = even_project(a_lat, w_in)
    q, k = rope_2d(q, row, col), rope_2d(k, row, col)
    if need_ctx_out:
        qc, kc, vc, uc, zc = even_project(a_ctx, w_in)
    else:
        kc, vc = project_kv(a_ctx, w_in)
    o = windowed_gqa_with_context(q, k, v, kc, vc, sink)
    y_lat = jnp.concatenate([o, spatial_gating(u, z, sgu_g, w_s, b_s)], axis=-1) @ w_out
    if not need_ctx_out:
        return y_lat, None
    oc = context_gqa(qc, kc, vc, sink)
    y_ctx = jnp.concatenate([oc, spatial_gating(uc, zc, sgu_g, w_s, b_s)], axis=-1) @ w_out
    return y_lat, y_ctx


def short_conv_centred(u, w, b):
    up = jnp.pad(u, ((0, 0), (1, 1), (0, 0)))
    return up[:, :-2] * w[0] + up[:, 1:-1] * w[1] + up[:, 2:] * w[2] + b


def hyena_filter(L, w1, b1, w2, b2, freq, w3):
    pos = jnp.arange(L, dtype=jnp.float32)
    t01 = pos / max(L - 1, 1)
    bands = jnp.linspace(1e-4, HYENA_BANDS - 1, HYENA_BANDS, dtype=jnp.float32)
    ang = (2.0 * math.pi / L) * pos[:, None] * bands[None, :]
    feats = jnp.concatenate([t01[:, None], jnp.cos(ang), -jnp.sin(ang)], axis=-1)
    hid = jnp.sin(freq[0] * (feats @ w1 + b1))
    hid = jnp.sin(freq[1] * (hid @ w2 + b2))
    h = (hid @ w3).astype(jnp.float32).reshape(L, 2, HYENA_WIDTH)
    deltas = jnp.abs(jnp.linspace(math.log(HYENA_TARGET) / HYENA_SLOW_DECAY_PCT, math.log(HYENA_TARGET) / HYENA_FAST_DECAY_PCT, HYENA_WIDTH, dtype=jnp.float32))
    h = h * jnp.exp(-t01[:, None] * deltas[None, :])[:, None, :]
    k = jnp.concatenate([h[:, 0], jnp.zeros((1, HYENA_WIDTH), jnp.float32), h[1:, 1][::-1]], axis=0)
    return k / jnp.sum(jnp.abs(k), axis=0, keepdims=True)


def hyena_mix(a, w_in, conv_w, conv_b, w1, b1, w2, b2, freq, w3, bias, w_out):
    L = a.shape[1]
    u = short_conv_centred(a @ w_in, conv_w, conv_b)
    x0, x1, v = jnp.split(u, 3, axis=-1)
    k = hyena_filter(L, w1, b1, w2, b2, freq, w3)
    z = (v * x1).astype(jnp.float32)
    zf = jnp.fft.rfft(z, n=2 * L, axis=1)
    kf = jnp.fft.rfft(k, n=2 * L, axis=0)
    y = jnp.fft.irfft(zf * kf[None], n=2 * L, axis=1)[:, :L] + z * bias.astype(jnp.float32)
    return (x0 * y.astype(a.dtype)) @ w_out


def expert_choice_moe(h, w_router, w_gate, w_up, w_down):
    B, L = h.shape[0], h.shape[1]
    cap = EC_CAPACITY_FACTOR * L // N_EXPERTS
    aff = jax.nn.softmax((h @ w_router).astype(jnp.float32), axis=-1)
    gates, idx = lax.top_k(jnp.swapaxes(aff, 1, 2), cap)
    bidx = jnp.arange(B)[:, None, None]
    xs = h[bidx, idx]
    act = jax.nn.silu(jnp.einsum('becd,edf->becf', xs, w_gate)) * jnp.einsum('becd,edf->becf', xs, w_up)
    ys = jnp.einsum('becf,efd->becd', act, w_down) * gates[..., None].astype(h.dtype)
    return jnp.zeros_like(h).at[bidx, idx].add(ys)


def setup_inputs(seed: int = 0) -> dict:
    key = jax.random.key(seed)
    ks = jax.random.split(key, 30)
    D = D_MODEL

    def nrm(k, shape, scale):
        return jax.random.normal(k, shape, jnp.float32) * scale

    return {
        'x': nrm(ks[0], (BATCH, SEQ, D), 1.0),
        'c': nrm(ks[1], (BATCH, D), 1.0),
        'ctx': nrm(ks[2], (BATCH, CTX_LEN, D), 1.0),
        'c_ctx': nrm(ks[3], (D,), 1.0),
        'ada_w': nrm(ks[4], (DEPTH, D, N_MOD * D), 0.5 * D ** -0.5),
        'ada_b': nrm(ks[5], (DEPTH, N_MOD * D), 0.01),
        'norm_mix_g': 1.0 + nrm(ks[6], (DEPTH, D), 0.1),
        'norm_ffn_g': 1.0 + nrm(ks[7], (DEPTH, D), 0.1),
        'attn_sgu_w_in': nrm(ks[8], (N_EVEN, D, EVEN_IN_WIDTH), D ** -0.5),
        'attn_sink': nrm(ks[9], (N_EVEN, ATTN_HEADS), 1.0),
        'sgu_norm_g': 1.0 + nrm(ks[10], (N_EVEN, SGU_WIDTH), 0.1),
        'sgu_w_s': nrm(ks[11], (N_EVEN, SGU_GROUPS, SGU_CHUNK, SGU_CHUNK), SGU_CHUNK ** -0.5),
        'sgu_b_s': 1.0 + nrm(ks[12], (N_EVEN, SGU_GROUPS, SGU_CHUNK), 0.1),
        'attn_sgu_w_out': nrm(ks[13], (N_EVEN, EVEN_OUT_WIDTH, D), EVEN_OUT_WIDTH ** -0.5),
        'hyena_w_in': nrm(ks[14], (N_ODD, D, 3 * HYENA_WIDTH), D ** -0.5),
        'hyena_conv_w': nrm(ks[15], (N_ODD, HYENA_SHORT, 3 * HYENA_WIDTH), HYENA_SHORT ** -0.5),
        'hyena_conv_b': nrm(ks[16], (N_ODD, 3 * HYENA_WIDTH), 0.01),
        'hyena_filt_w1': nrm(ks[17], (N_ODD, HYENA_EMB, HYENA_FILTER_HIDDEN), HYENA_EMB ** -0.5),
        'hyena_filt_b1': nrm(ks[18], (N_ODD, HYENA_FILTER_HIDDEN), 0.1),
        'hyena_filt_w2': nrm(ks[19], (N_ODD, HYENA_FILTER_HIDDEN, HYENA_FILTER_HIDDEN), HYENA_FILTER_HIDDEN ** -0.5),
        'hyena_filt_b2': nrm(ks[20], (N_ODD, HYENA_FILTER_HIDDEN), 0.1),
        'hyena_filt_freq': 1.0 + nrm(ks[21], (N_ODD, 2, HYENA_FILTER_HIDDEN), 0.1),
        'hyena_filt_w3': nrm(ks[22], (N_ODD, HYENA_FILTER_HIDDEN, 2 * HYENA_WIDTH), HYENA_FILTER_HIDDEN ** -0.5),
        'hyena_bias': nrm(ks[23], (N_ODD, HYENA_WIDTH), 0.1),
        'hyena_w_out': nrm(ks[24], (N_ODD, HYENA_WIDTH, D), HYENA_WIDTH ** -0.5),
        'router_w': nrm(ks[25], (DEPTH, D, N_EXPERTS), D ** -0.5),
        'expert_w_gate': nrm(ks[26], (DEPTH, N_EXPERTS, D, EXPERT_FF), D ** -0.5),
        'expert_w_up': nrm(ks[27], (DEPTH, N_EXPERTS, D, EXPERT_FF), D ** -0.5),
        'expert_w_down': nrm(ks[28], (DEPTH, N_EXPERTS, EXPERT_FF, D), EXPERT_FF ** -0.5),
        'final_norm_g': 1.0 + nrm(ks[29], (D,), 0.1),
    }


def reference(x, c, ctx, c_ctx, ada_w, ada_b, norm_mix_g, norm_ffn_g, attn_sgu_w_in, attn_sink, sgu_norm_g, sgu_w_s, sgu_b_s, attn_sgu_w_out, hyena_w_in, hyena_conv_w, hyena_conv_b, hyena_filt_w1, hyena_filt_b1, hyena_filt_w2, hyena_filt_b2, hyena_filt_freq, hyena_filt_w3, hyena_bias, hyena_w_out, router_w, expert_w_gate, expert_w_up, expert_w_down, final_norm_g):
    L = x.shape[1]
    row, col = grid_positions(L)
    s_c = jax.nn.silu(c)
    s_cc = jax.nn.silu(c_ctx)
    h_lat, h_ctx = x, ctx
    for layer in range(DEPTH):
        even = layer % 2 == 0
        i = layer // 2
        update_ctx = any(l % 2 == 0 for l in range(layer + 1, DEPTH))
        mod_lat = jnp.split((s_c @ ada_w[layer] + ada_b[layer])[:, None, :], N_MOD, axis=-1)
        a_lat = modulate(rms_norm(h_lat, norm_mix_g[layer]), mod_lat[0], mod_lat[1])
        if even or update_ctx:
            mod_ctx = jnp.split(s_cc @ ada_w[layer] + ada_b[layer], N_MOD, axis=-1)
            a_ctx = modulate(rms_norm(h_ctx, norm_mix_g[layer]), mod_ctx[0], mod_ctx[1])
        if even:
            y_lat, y_ctx = attn_sgu_mixer(a_lat, a_ctx, row, col, attn_sgu_w_in[i], attn_sink[i], sgu_norm_g[i], sgu_w_s[i], sgu_b_s[i], attn_sgu_w_out[i], update_ctx)
        else:
            hp = (hyena_w_in[i], hyena_conv_w[i], hyena_conv_b[i], hyena_filt_w1[i], hyena_filt_b1[i], hyena_filt_w2[i], hyena_filt_b2[i], hyena_filt_freq[i], hyena_filt_w3[i], hyena_bias[i], hyena_w_out[i])
            y_lat = hyena_mix(a_lat, *hp)
            y_ctx = hyena_mix(a_ctx, *hp) if update_ctx else None
        h_lat = h_lat + mod_lat[2] * y_lat
        b_lat = modulate(rms_norm(h_lat, norm_ffn_g[layer]), mod_lat[3], mod_lat[4])
        h_lat = h_lat + mod_lat[5] * expert_choice_moe(b_lat, router_w[layer], expert_w_gate[layer], expert_w_up[layer], expert_w_down[layer])
        if update_ctx:
            h_ctx = h_ctx + mod_ctx[2] * y_ctx
            b_ctx = modulate(rms_norm(h_ctx, norm_ffn_g[layer]), mod_ctx[3], mod_ctx[4])
            h_ctx = h_ctx + mod_ctx[5] * expert_choice_moe(b_ctx, router_w[layer], expert_w_gate[layer], expert_w_up[layer], expert_w_down[layer])
    return rms_norm(h_lat, final_norm_g)
```

```python
import functools
import math

import numpy as np
import jax
import jax.numpy as jnp
from jax import lax
from jax.experimental import pallas as pl
from jax.experimental.pallas import tpu as pltpu

F32 = jnp.float32
BF16 = jnp.bfloat16
I32 = jnp.int32
HIGHEST = lax.Precision.HIGHEST

D_MODEL = 4096
GRID_W = 64
NORM_EPS = 1e-6
N_MOD = 6
NEG_INF = -1e30
HEAD_DIM = 128
ATTN_HEADS = 16
ATTN_KV_HEADS = 4
GQ = 4
ATTN_BLOCK = 128
ROPE_THETA = 10000.0
ATTN_WIDTH = 2048
KV_WIDTH = 512
SGU_WIDTH = 2048
SGU_CHUNK = 128
ROPE_WIDTH = ATTN_WIDTH + KV_WIDTH
HYENA_BANDS = 16
HYENA_EMB = 33
HYENA_HID = 64
N_EXPERTS = 16
EXPERT_FF = 1024

LANES = 128
VMEM_LIMIT_BYTES = 56 * 1024 * 1024

FFT_N1 = 64
FFT_N2 = 128
FFT_N = FFT_N1 * FFT_N2
FFT_HALF = FFT_N1 // 2


def _params(*sem):
    return pltpu.CompilerParams(dimension_semantics=sem, vmem_limit_bytes=VMEM_LIMIT_BYTES)


def _ada_kernel(c_ref, w_ref, b_ref, o_ref):
    s = jax.nn.silu(c_ref[...]).astype(BF16)
    o_ref[0] = jnp.dot(s, w_ref[0].astype(BF16), preferred_element_type=F32) + b_ref[0]


def ada_modulation(c_rows, ada_w, ada_b, tn=512):
    depth, d, n = ada_w.shape
    rows = c_rows.shape[0]
    return pl.pallas_call(
        _ada_kernel,
        grid=(depth, n // tn),
        in_specs=[
            pl.BlockSpec((rows, d), lambda l, j: (0, 0)),
            pl.BlockSpec((1, d, tn), lambda l, j: (l, 0, j)),
            pl.BlockSpec((1, 1, tn), lambda l, j: (l, 0, j)),
        ],
        out_specs=pl.BlockSpec((1, rows, tn), lambda l, j: (l, 0, j)),
        out_shape=jax.ShapeDtypeStruct((depth, rows, n), F32),
        compiler_params=_params("parallel", "parallel"),
        name="ada_modulation",
    )(c_rows, ada_w, ada_b.reshape(depth, 1, n))


def _rms(x, g):
    return x * lax.rsqrt(jnp.mean(x * x, axis=-1, keepdims=True) + NORM_EPS) * g


def _norm_mod_kernel(h_ref, g_ref, sh_ref, sc_ref, o_ref):
    y = _rms(h_ref[0], g_ref[...])
    o_ref[0] = (y * (1.0 + sc_ref[0]) + sh_ref[0]).astype(o_ref.dtype)


def norm_modulate(h, g, shift, scale, out_dtype, tl=256):
    b, l, d = h.shape
    tl = min(tl, l)
    return pl.pallas_call(
        _norm_mod_kernel,
        grid=(b, l // tl),
        in_specs=[
            pl.BlockSpec((1, tl, d), lambda i, j: (i, j, 0)),
            pl.BlockSpec((1, d), lambda i, j: (0, 0)),
            pl.BlockSpec((1, 1, d), lambda i, j: (i, 0, 0)),
            pl.BlockSpec((1, 1, d), lambda i, j: (i, 0, 0)),
        ],
        out_specs=pl.BlockSpec((1, tl, d), lambda i, j: (i, j, 0)),
        out_shape=jax.ShapeDtypeStruct((b, l, d), out_dtype),
        compiler_params=_params("parallel", "parallel"),
        name="norm_modulate",
    )(h, g.reshape(1, d), shift.reshape(b, 1, d), scale.reshape(b, 1, d))


def _norm_mod_router_kernel(h_ref, g_ref, sh_ref, sc_ref, rw_ref, o_ref, aff_ref):
    y = _rms(h_ref[0], g_ref[...])
    a = y * (1.0 + sc_ref[0]) + sh_ref[0]
    o_ref[0] = a
    lt = lax.dot_general(rw_ref[...], a, (((1,), (1,)), ((), ())), precision=HIGHEST, preferred_element_type=F32)
    m = jnp.max(lt, axis=0, keepdims=True)
    p = jnp.exp(lt - m)
    aff_ref[0] = p / jnp.sum(p, axis=0, keepdims=True)


def norm_modulate_router(h, g, shift, scale, router_w, tl=256):
    b, l, d = h.shape
    e = router_w.shape[1]
    return pl.pallas_call(
        _norm_mod_router_kernel,
        grid=(b, l // tl),
        in_specs=[
            pl.BlockSpec((1, tl, d), lambda i, j: (i, j, 0)),
            pl.BlockSpec((1, d), lambda i, j: (0, 0)),
            pl.BlockSpec((1, 1, d), lambda i, j: (i, 0, 0)),
            pl.BlockSpec((1, 1, d), lambda i, j: (i, 0, 0)),
            pl.BlockSpec((e, d), lambda i, j: (0, 0)),
        ],
        out_specs=[
            pl.BlockSpec((1, tl, d), lambda i, j: (i, j, 0)),
            pl.BlockSpec((1, e, tl), lambda i, j: (i, 0, j)),
        ],
        out_shape=[jax.ShapeDtypeStruct((b, l, d), F32), jax.ShapeDtypeStruct((b, e, l), F32)],
        compiler_params=_params("parallel", "parallel"),
        name="norm_modulate_router",
    )(h, g.reshape(1, d), shift.reshape(b, 1, d), scale.reshape(b, 1, d), router_w.T)


def _final_norm_kernel(h_ref, g_ref, o_ref):
    o_ref[0] = _rms(h_ref[0], g_ref[...])


def final_norm(h, g, tl=256):
    b, l, d = h.shape
    return pl.pallas_call(
        _final_norm_kernel,
        grid=(b, l // tl),
        in_specs=[pl.BlockSpec((1, tl, d), lambda i, j: (i, j, 0)), pl.BlockSpec((1, d), lambda i, j: (0, 0))],
        out_specs=pl.BlockSpec((1, tl, d), lambda i, j: (i, j, 0)),
        out_shape=jax.ShapeDtypeStruct((b, l, d), F32),
        compiler_params=_params("parallel", "parallel"),
        name="final_norm",
    )(h, g.reshape(1, d))


def _mm_kernel(a_ref, w_ref, o_ref, wb_ref):
    @pl.when(pl.program_id(1) == 0)
    def _():
        wb_ref[...] = w_ref[...].astype(BF16)

    o_ref[...] = jnp.dot(a_ref[...], wb_ref[...], preferred_element_type=F32).astype(o_ref.dtype)


def _mm_res_kernel(a_ref, w_ref, res_ref, gate_ref, o_ref, wb_ref):
    @pl.when(pl.program_id(1) == 0)
    def _():
        wb_ref[...] = w_ref[...].astype(BF16)

    acc = jnp.dot(a_ref[...], wb_ref[...], preferred_element_type=F32)
    o_ref[...] = res_ref[...] + gate_ref[0] * acc


def project(a, w, *, n_start=0, n_cols=None, out_dtype=BF16, tm=512, tn=512):
    m, k = a.shape
    n_cols = w.shape[1] - n_start if n_cols is None else n_cols
    tm = min(tm, m)
    off = n_start // tn
    return pl.pallas_call(
        _mm_kernel,
        grid=(n_cols // tn, m // tm),
        in_specs=[
            pl.BlockSpec((tm, k), lambda j, i: (i, 0)),
            pl.BlockSpec((k, tn), lambda j, i: (0, off + j)),
        ],
        out_specs=pl.BlockSpec((tm, tn), lambda j, i: (i, j)),
        out_shape=jax.ShapeDtypeStruct((m, n_cols), out_dtype),
        scratch_shapes=[pltpu.VMEM((k, tn), BF16)],
        compiler_params=_params("arbitrary", "arbitrary"),
        name="project",
    )(a, w)


def project_residual(a, w, res, gate, *, rows_per_batch, tm=512, tn=512):
    m, k = a.shape
    n = w.shape[1]
    nb = gate.shape[0]
    tiles_per_batch = rows_per_batch // tm
    return pl.pallas_call(
        _mm_res_kernel,
        grid=(n // tn, m // tm),
        in_specs=[
            pl.BlockSpec((tm, k), lambda j, i: (i, 0)),
            pl.BlockSpec((k, tn), lambda j, i: (0, j)),
            pl.BlockSpec((tm, tn), lambda j, i: (i, j)),
            pl.BlockSpec((1, 1, tn), lambda j, i: (i // tiles_per_batch, 0, j)),
        ],
        out_specs=pl.BlockSpec((tm, tn), lambda j, i: (i, j)),
        out_shape=jax.ShapeDtypeStruct((m, n), F32),
        scratch_shapes=[pltpu.VMEM((k, tn), BF16)],
        compiler_params=_params("arbitrary", "arbitrary"),
        name="project_residual",
    )(a, w, res, gate.reshape(nb, 1, n))


def _rope_tables(l):
    rows = l // GRID_W
    row = np.repeat(np.arange(rows), GRID_W).astype(np.float32)
    col = np.tile(np.arange(GRID_W), rows).astype(np.float32)
    half = HEAD_DIM // 2
    inv = (ROPE_THETA ** (-np.arange(0, half, 2, dtype=np.float32) / half)).astype(np.float32)
    ang_r = row[:, None] * inv[None, :]
    ang_c = col[:, None] * inv[None, :]
    ang = np.concatenate([ang_r, ang_r, ang_c, ang_c], axis=1)
    sign = np.tile(np.concatenate([-np.ones(half // 2), np.ones(half // 2)]), 2).astype(np.float32)
    return jnp.asarray(np.cos(ang), F32), jnp.asarray(np.sin(ang) * sign[None, :], F32)


def _rope_kernel(x_ref, cos_ref, sin_ref, o_ref):
    cos = cos_ref[...]
    sin = sin_ref[...]
    quarter = HEAD_DIM // 4
    lane = lax.broadcasted_iota(I32, cos.shape, 1)
    first = (lane % (2 * quarter)) < quarter
    for j in range(x_ref.shape[2] // HEAD_DIM):
        sl = slice(j * HEAD_DIM, (j + 1) * HEAD_DIM)
        x = x_ref[0, :, sl].astype(F32)
        partner = jnp.where(first, pltpu.roll(x, HEAD_DIM - quarter, 1), pltpu.roll(x, quarter, 1))
        o_ref[0, :, sl] = (x * cos + partner * sin).astype(o_ref.dtype)


def rope_qk(qkvuz, tl=256):
    b, l, _ = qkvuz.shape
    cos, sin = _rope_tables(l)
    return pl.pallas_call(
        _rope_kernel,
        grid=(b, l // tl),
        in_specs=[
            pl.BlockSpec((1, tl, ROPE_WIDTH), lambda i, j: (i, j, 0)),
            pl.BlockSpec((tl, HEAD_DIM), lambda i, j: (j, 0)),
            pl.BlockSpec((tl, HEAD_DIM), lambda i, j: (j, 0)),
        ],
        out_specs=pl.BlockSpec((1, tl, ROPE_WIDTH), lambda i, j: (i, j, 0)),
        out_shape=jax.ShapeDtypeStruct((b, l, ROPE_WIDTH), BF16),
        compiler_params=_params("parallel", "parallel"),
        name="rope_qk",
    )(qkvuz, cos, sin)


def _attn_kernel(sink_ref, q_ref, kp_ref, k0_ref, kn_ref, vp_ref, v0_ref, vn_ref, kx_ref, vx_ref, o_ref, *, nb):
    h = pl.program_id(1)
    n = pl.program_id(2)
    blk = ATTN_BLOCK
    scale = HEAD_DIM ** -0.5
    q = jnp.concatenate([q_ref[0, :, g * HEAD_DIM:(g + 1) * HEAD_DIM] for g in range(GQ)], axis=0)

    def scores(k):
        return lax.dot_general(q, k, (((1,), (1,)), ((), ())), preferred_element_type=F32) * scale

    sp, s0, sn, sx = scores(kp_ref[0]), scores(k0_ref[0]), scores(kn_ref[0]), scores(kx_ref[0])
    qi = lax.broadcasted_iota(I32, (GQ * blk, blk), 0) % blk
    ki = lax.broadcasted_iota(I32, (GQ * blk, blk), 1)
    sp = jnp.where(ki >= qi + jnp.where(n > 0, 0, blk), sp, NEG_INF)
    sn = jnp.where(ki <= qi - jnp.where(n < nb - 1, 0, blk), sn, NEG_INF)
    sink = jnp.concatenate([jnp.full((blk, 1), sink_ref[h * GQ + g], F32) for g in range(GQ)], axis=0)

    def rmax(s):
        return jnp.max(s, axis=-1, keepdims=True)

    m = jnp.maximum(jnp.maximum(jnp.maximum(rmax(sp), rmax(s0)), jnp.maximum(rmax(sn), rmax(sx))), sink)
    pp, p0, pn, px = jnp.exp(sp - m), jnp.exp(s0 - m), jnp.exp(sn - m), jnp.exp(sx - m)

    def rsum(p):
        return jnp.sum(p, axis=-1, keepdims=True)

    denom = rsum(pp) + rsum(p0) + rsum(pn) + rsum(px) + jnp.exp(sink - m)

    def pv(p, v):
        return jnp.dot(p.astype(BF16), v, preferred_element_type=F32)

    o = pv(pp, vp_ref[0]) + pv(p0, v0_ref[0]) + pv(pn, vn_ref[0]) + pv(px, vx_ref[0])
    o = o / denom
    for g in range(GQ):
        o_ref[0, :, g * HEAD_DIM:(g + 1) * HEAD_DIM] = o[g * blk:(g + 1) * blk].astype(o_ref.dtype)


def windowed_attention(qk, qkvuz, ctx_kv, sink):
    b, l, _ = qk.shape
    lc = ctx_kv.shape[1]
    nb = l // ATTN_BLOCK
    kcol = ATTN_WIDTH // HEAD_DIM
    vcol = (ATTN_WIDTH + KV_WIDTH) // HEAD_DIM
    blk = (1, ATTN_BLOCK, HEAD_DIM)
    prev = lambda n: jnp.maximum(n - 1, 0)
    nxt = lambda n: jnp.minimum(n + 1, nb - 1)
    return pl.pallas_call(
        functools.partial(_attn_kernel, nb=nb),
        grid=(b, ATTN_KV_HEADS, nb),
        in_specs=[
            pl.BlockSpec(memory_space=pltpu.SMEM),
            pl.BlockSpec((1, ATTN_BLOCK, GQ * HEAD_DIM), lambda i, h, n: (i, n, h)),
            pl.BlockSpec(blk, lambda i, h, n: (i, prev(n), kcol + h)),
            pl.BlockSpec(blk, lambda i, h, n: (i, n, kcol + h)),
            pl.BlockSpec(blk, lambda i, h, n: (i, nxt(n), kcol + h)),
            pl.BlockSpec(blk, lambda i, h, n: (i, prev(n), vcol + h)),
            pl.BlockSpec(blk, lambda i, h, n: (i, n, vcol + h)),
            pl.BlockSpec(blk, lambda i, h, n: (i, nxt(n), vcol + h)),
            pl.BlockSpec((1, lc, HEAD_DIM), lambda i, h, n: (i, 0, h)),
            pl.BlockSpec((1, lc, HEAD_DIM), lambda i, h, n: (i, 0, ATTN_KV_HEADS + h)),
        ],
        out_specs=pl.BlockSpec((1, ATTN_BLOCK, GQ * HEAD_DIM), lambda i, h, n: (i, n, h)),
        out_shape=jax.ShapeDtypeStruct((b, l, ATTN_WIDTH), BF16),
        compiler_params=_params("parallel", "parallel", "parallel"),
        name="windowed_attention",
    )(sink, qk, qk, qk, qk, qkvuz, qkvuz, qkvuz, ctx_kv, ctx_kv)


def _gelu(x):
    return 0.5 * x * (1.0 + lax.erf(x * (1.0 / math.sqrt(2.0))))


def _sgu_kernel(u_ref, z_ref, g_ref, ws_ref, bs_ref, o_ref):
    for gi in range(ws_ref.shape[0]):
        sl = slice(gi * LANES, (gi + 1) * LANES)
        z = _gelu(z_ref[0, :, sl].astype(F32))
        mu = jnp.mean(z, axis=-1, keepdims=True)
        zc = z - mu
        var = jnp.mean(zc * zc, axis=-1, keepdims=True)
        zn = zc * lax.rsqrt(var + NORM_EPS) * g_ref[:, sl]
        mixed = jnp.dot(ws_ref[gi].astype(BF16), zn.astype(BF16), preferred_element_type=F32) + bs_ref[:, sl]
        o_ref[0, :, sl] = (_gelu(u_ref[0, :, sl].astype(F32)) * mixed).astype(o_ref.dtype)


def spatial_gating(qkvuz, g, w_s, b_s, tw=1024):
    b, l, _ = qkvuz.shape
    groups = w_s.shape[0]
    ucol = (ATTN_WIDTH + 2 * KV_WIDTH) // tw
    zcol = (ATTN_WIDTH + 2 * KV_WIDTH + SGU_WIDTH) // tw
    gpt = tw // LANES
    bias = jnp.repeat(jnp.transpose(b_s), LANES, axis=1)
    return pl.pallas_call(
        _sgu_kernel,
        grid=(b, l // SGU_CHUNK, SGU_WIDTH // tw),
        in_specs=[
            pl.BlockSpec((1, SGU_CHUNK, tw), lambda i, n, j: (i, n, ucol + j)),
            pl.BlockSpec((1, SGU_CHUNK, tw), lambda i, n, j: (i, n, zcol + j)),
            pl.BlockSpec((1, tw), lambda i, n, j: (0, j)),
            pl.BlockSpec((gpt, SGU_CHUNK, SGU_CHUNK), lambda i, n, j: (j, 0, 0)),
            pl.BlockSpec((SGU_CHUNK, tw), lambda i, n, j: (0, j)),
        ],
        out_specs=pl.BlockSpec((1, SGU_CHUNK, tw), lambda i, n, j: (i, n, j)),
        out_shape=jax.ShapeDtypeStruct((b, l, SGU_WIDTH), BF16),
        compiler_params=_params("parallel", "parallel", "parallel"),
        name="spatial_gating",
    )(qkvuz, qkvuz, g.reshape(1, groups * LANES), w_s, bias)


HALO = 16


def _hyena_pre_kernel(*refs, tl):
    (p0, m0, n0, p1, m1, n1, p2, m2, n2, w0, w1, w2, b0, b1, b2, x0_ref, z_ref) = refs
    first = pl.program_id(1) == 0
    last = pl.program_id(1) == pl.num_programs(1) - 1
    rows = lax.broadcasted_iota(I32, m0.shape[1:], 0)

    def conv(p_ref, m_ref, n_ref, w_ref, b_ref):
        x = m_ref[0].astype(F32)
        prev_row = jnp.where(first, 0.0, p_ref[0, HALO - 1:HALO, :].astype(F32))
        next_row = jnp.where(last, 0.0, n_ref[0, 0:1, :].astype(F32))
        xm = jnp.where(rows == 0, prev_row, pltpu.roll(x, 1, 0))
        xp = jnp.where(rows == tl - 1, next_row, pltpu.roll(x, tl - 1, 0))
        return xm * w_ref[0:1, :] + x * w_ref[1:2, :] + xp * w_ref[2:3, :] + b_ref[...]

    x0_ref[0] = conv(p0, m0, n0, w0, b0).astype(x0_ref.dtype)
    x1 = conv(p1, m1, n1, w1, b1)
    v = conv(p2, m2, n2, w2, b2)
    z_ref[0] = (v * x1).astype(z_ref.dtype)


def hyena_pre(u3, conv_w, conv_b, tl=512, tc=512):
    b, l, w3 = u3.shape
    c = w3 // 3
    ct = c // tc
    hb = tl // HALO
    nhb = l // HALO
    in_specs = []
    for part in range(3):
        in_specs += [
            pl.BlockSpec((1, HALO, tc), lambda i, t, j, part=part: (i, jnp.maximum(t * hb - 1, 0), part * ct + j)),
            pl.BlockSpec((1, tl, tc), lambda i, t, j, part=part: (i, t, part * ct + j)),
            pl.BlockSpec((1, HALO, tc), lambda i, t, j, part=part: (i, jnp.minimum((t + 1) * hb, nhb - 1), part * ct + j)),
        ]
    for part in range(3):
        in_specs.append(pl.BlockSpec((3, tc), lambda i, t, j, part=part: (0, part * ct + j)))
    for part in range(3):
        in_specs.append(pl.BlockSpec((1, tc), lambda i, t, j, part=part: (0, part * ct + j)))
    out_spec = pl.BlockSpec((1, tl, tc), lambda i, t, j: (i, t, j))
    return pl.pallas_call(
        functools.partial(_hyena_pre_kernel, tl=tl),
        grid=(b, l // tl, ct),
        in_specs=in_specs,
        out_specs=[out_spec, out_spec],
        out_shape=[jax.ShapeDtypeStruct((b, l, c), BF16), jax.ShapeDtypeStruct((b, l, c), BF16)],
        compiler_params=_params("parallel", "parallel", "parallel"),
        name="hyena_pre",
    )(*([u3] * 9), *([conv_w] * 3), *([conv_b.reshape(1, w3)] * 3))


def _filter_features(l):
    pos = np.arange(l, dtype=np.float32)
    t01 = pos / np.float32(max(l - 1, 1))
    bands = np.linspace(1e-4, HYENA_BANDS - 1, HYENA_BANDS, dtype=np.float32)
    ang = np.float32(2.0 * math.pi / l) * pos[:, None] * bands[None, :]
    feats = np.concatenate([t01[:, None], np.cos(ang), -np.sin(ang)], axis=-1).astype(np.float32)
    return np.pad(feats, ((0, 0), (0, HYENA_HID - HYENA_EMB)))


def _filter_mlp_kernel(f_ref, w1_ref, b1_ref, w2_ref, b2_ref, fr_ref, o_ref):
    h = jnp.dot(f_ref[...], w1_ref[...], precision=HIGHEST, preferred_element_type=F32) + b1_ref[...]
    h = jnp.sin(fr_ref[0:1, :] * h)
    h = jnp.dot(h, w2_ref[...], precision=HIGHEST, preferred_element_type=F32) + b2_ref[...]
    o_ref[...] = jnp.sin(fr_ref[1:2, :] * h)


def filter_mlp(l, w1, b1, w2, b2, freq):
    feats = jnp.asarray(_filter_features(l))
    w1p = jnp.pad(w1, ((0, HYENA_HID - HYENA_EMB), (0, 0)))
    return pl.pallas_call(
        _filter_mlp_kernel,
        out_shape=jax.ShapeDtypeStruct((l, HYENA_HID), F32),
        compiler_params=pltpu.CompilerParams(vmem_limit_bytes=VMEM_LIMIT_BYTES),
        name="filter_mlp",
    )(feats, w1p, b1.reshape(1, -1), w2, b2.reshape(1, -1), freq)


def _dft_constants():
    n1 = np.arange(FFT_N1, dtype=np.float64)
    n2 = np.arange(FFT_N2, dtype=np.float64)
    k1 = n1
    two_pi = 2.0 * np.pi
    theta = two_pi * (k1[None, :, None] * n1[None, None, :FFT_HALF] / FFT_N1 + n2[:, None, None] * k1[None, :, None] / FFT_N)
    gr, gi = np.cos(theta), -np.sin(theta)
    g1 = np.concatenate([np.concatenate([gr, -gi], axis=2), np.concatenate([gi, gr], axis=2)], axis=1)
    ang2 = two_pi * np.outer(n2, n2) / FFT_N2
    cm, sm = np.cos(ang2), np.sin(ang2)
    f2 = np.block([[cm, sm], [-sm, cm]])
    f2i = np.block([[cm, -sm], [sm, cm]])
    phi = two_pi * (n1[None, :FFT_HALF, None] * k1[None, None, :] / FFT_N1 + n2[:, None, None] * k1[None, None, :] / FFT_N)
    cp, sp = np.cos(phi) / FFT_N, np.sin(phi) / FFT_N
    hm = np.concatenate([np.concatenate([cp, -sp], axis=2), np.concatenate([sp, cp], axis=2)], axis=1)
    as_bf16 = lambda a: jnp.asarray(a.astype(np.float32)).astype(BF16)
    return as_bf16(g1), as_bf16(f2), as_bf16(f2i), as_bf16(hm)


def _hyena_fft_kernel(z_ref, x0_ref, hid_ref, w3f_ref, w3b_ref, dl_ref, bias_ref, g1_ref, f2_ref, f2i_ref, hm_ref,
                      o_ref, t_ref, a_ref, kf_ref, b_ref, *, seq):
    two_n1 = 2 * FFT_N1
    two_n2 = 2 * FFT_N2

    def stage1(complex_input):
        def body(n2, carry):
            xr = t_ref[pl.ds(n2, FFT_HALF, stride=FFT_N2), :]
            if complex_input:
                xi = t_ref[pl.ds(seq + n2, FFT_HALF, stride=FFT_N2), :]
                x = jnp.concatenate([xr, xi], axis=0).astype(BF16)
                g = g1_ref[n2]
            else:
                x = xr.astype(BF16)
                g = g1_ref[n2, :, 0:FFT_HALF]
            a_ref[pl.ds(pl.multiple_of(n2 * two_n1, two_n1), two_n1), :] = jnp.dot(g, x, preferred_element_type=F32)
            return carry
        lax.fori_loop(0, FFT_N2, body, 0)

    def stage2(mode):
        def body(k1, carry):
            ar = a_ref[pl.ds(k1, FFT_N2, stride=two_n1), :]
            ai = a_ref[pl.ds(FFT_N1 + k1, FFT_N2, stride=two_n1), :]
            s = jnp.concatenate([ar, ai], axis=0).astype(BF16)
            x = jnp.dot(f2_ref[...], s, preferred_element_type=F32)
            rows = pl.ds(pl.multiple_of(k1 * two_n2, two_n2), two_n2)
            if mode == "filter_forward":
                b_ref[rows, :] = x
            elif mode == "filter_backward":
                sign = jnp.where(lax.broadcasted_iota(I32, x.shape, 0) < FFT_N2, 1.0, -1.0)
                kf_ref[rows, :] = (b_ref[rows, :] + sign * x).astype(kf_ref.dtype)
            else:
                kf = kf_ref[rows, :].astype(F32)
                xr, xi = x[:FFT_N2], x[FFT_N2:]
                kr, ki = kf[:FFT_N2], kf[FFT_N2:]
                y = jnp.concatenate([xr * kr - xi * ki, xr * ki + xi * kr], axis=0).astype(BF16)
                b_ref[rows, :] = jnp.dot(f2i_ref[...], y, preferred_element_type=F32)
            return carry
        lax.fori_loop(0, FFT_N1, body, 0)

    def stage_out():
        def body(n2, carry):
            br = b_ref[pl.ds(n2, FFT_N1, stride=two_n2), :]
            bi = b_ref[pl.ds(FFT_N2 + n2, FFT_N1, stride=two_n2), :]
            s = jnp.concatenate([br, bi], axis=0).astype(BF16)
            y = jnp.dot(hm_ref[n2], s, preferred_element_type=F32)
            t_ref[pl.ds(n2, FFT_HALF, stride=FFT_N2), :] = y[:FFT_HALF]
            t_ref[pl.ds(seq + n2, FFT_HALF, stride=FFT_N2), :] = y[FFT_HALF:]
            return carry
        lax.fori_loop(0, FFT_N2, body, 0)

    ct = dl_ref.shape[1]
    t_idx = lax.broadcasted_iota(I32, (seq, ct), 0)
    t01 = t_idx.astype(F32) / float(max(seq - 1, 1))
    decay = jnp.exp(-t01 * dl_ref[...])
    hid = hid_ref[...]
    hf = jnp.dot(hid, w3f_ref[...], precision=HIGHEST, preferred_element_type=F32) * decay
    hb = jnp.dot(hid, w3b_ref[...], precision=HIGHEST, preferred_element_type=F32) * decay
    hb = jnp.where(t_idx == 0, 0.0, hb)
    norm = jnp.sum(jnp.abs(hf), axis=0, keepdims=True) + jnp.sum(jnp.abs(hb), axis=0, keepdims=True)

    t_ref[0:seq, :] = hf
    stage1(False)
    stage2("filter_forward")
    t_ref[0:seq, :] = hb
    stage1(False)
    stage2("filter_backward")

    t_ref[0:seq, :] = z_ref[0].astype(F32)
    t_ref[seq:2 * seq, :] = z_ref[1].astype(F32)
    stage1(True)
    stage2("data")
    stage_out()

    inv_norm = 1.0 / norm
    for bi in range(2):
        zb = z_ref[bi].astype(F32)
        y = t_ref[bi * seq:(bi + 1) * seq, :] * inv_norm + zb * bias_ref[...]
        o_ref[bi] = (x0_ref[bi].astype(F32) * y).astype(o_ref.dtype)


def hyena_long_conv(z, x0, hid, w3, bias, ct=128):
    b, l, c = z.shape
    assert b == 2 and 2 * l == FFT_N
    g1, f2, f2i, hm = _dft_constants()
    lo, hi = math.log(1e-2) / 1.5, math.log(1e-2) / 0.3
    deltas = jnp.asarray(np.abs(np.linspace(lo, hi, c, dtype=np.float32)).reshape(1, c))
    nct = c // ct
    full = lambda a: pl.BlockSpec(a.shape, lambda j: (0,) * a.ndim)
    return pl.pallas_call(
        functools.partial(_hyena_fft_kernel, seq=l),
        grid=(nct,),
        in_specs=[
            pl.BlockSpec((b, l, ct), lambda j: (0, 0, j)),
            pl.BlockSpec((b, l, ct), lambda j: (0, 0, j)),
            full(hid),
            pl.BlockSpec((HYENA_HID, ct), lambda j: (0, j)),
            pl.BlockSpec((HYENA_HID, ct), lambda j: (0, nct + j)),
            pl.BlockSpec((1, ct), lambda j: (0, j)),
            pl.BlockSpec((1, ct), lambda j: (0, j)),
            full(g1), full(f2), full(f2i), full(hm),
        ],
        out_specs=pl.BlockSpec((b, l, ct), lambda j: (0, 0, j)),
        out_shape=jax.ShapeDtypeStruct((b, l, c), BF16),
        scratch_shapes=[
            pltpu.VMEM((2 * l, ct), F32),
            pltpu.VMEM((FFT_N2 * 2 * FFT_N1, ct), F32),
            pltpu.VMEM((FFT_N1 * 2 * FFT_N2, ct), BF16),
            pltpu.VMEM((FFT_N1 * 2 * FFT_N2, ct), F32),
        ],
        compiler_params=_params("parallel"),
        name="hyena_long_conv",
    )(z, x0, hid, w3, w3, deltas, bias.reshape(1, c), g1, f2, f2i, hm)


SEL_BLK = 512


def _select_kernel(aff_ref, tri_ref, idx_ref, gate_ref, *, cap):
    x = aff_ref[0]
    l = x.shape[1]
    bits = pltpu.bitcast(x, I32)

    def search(i, prefix):
        cand = prefix | lax.shift_left(jnp.int32(1), 30 - i)
        cnt = jnp.sum((bits >= cand).astype(I32), axis=1, keepdims=True)
        return jnp.where(cnt >= cap, cand, prefix)

    tau = lax.fori_loop(0, 31, search, jnp.zeros((1, 1), I32))
    gt = bits > tau
    eq = bits == tau
    need = (cap - jnp.sum(gt.astype(I32), axis=1, keepdims=True)).astype(F32)

    def prefix_count(mask):
        run = jnp.zeros((1, 1), F32)
        parts = []
        for blk in range(l // SEL_BLK):
            m = mask[:, blk * SEL_BLK:(blk + 1) * SEL_BLK].astype(BF16)
            c = jnp.dot(m, tri_ref[...], preferred_element_type=F32) + run
            parts.append(c)
            run = c[:, SEL_BLK - 1:SEL_BLK]
        return jnp.concatenate(parts, axis=1)

    eq_rank = prefix_count(jnp.where(eq, 1.0, 0.0))
    sel = jnp.logical_or(gt, jnp.logical_and(eq, eq_rank <= need))
    pos = jnp.where(sel, prefix_count(jnp.where(sel, 1.0, 0.0)), 0.0)

    slot = (lax.broadcasted_iota(I32, (cap, SEL_BLK), 0) + 1).astype(F32)
    tok = lax.broadcasted_iota(I32, (1, SEL_BLK), 1).astype(F32)
    acc_i = jnp.zeros((cap, SEL_BLK), F32)
    acc_g = jnp.zeros((cap, SEL_BLK), F32)
    for blk in range(l // SEL_BLK):
        sl = slice(blk * SEL_BLK, (blk + 1) * SEL_BLK)
        hit = pos[:, sl] == slot
        acc_i = acc_i + jnp.where(hit, tok + float(blk * SEL_BLK), 0.0)
        acc_g = acc_g + jnp.where(hit, x[:, sl], 0.0)
    idx_ref[0] = jnp.sum(acc_i, axis=1, keepdims=True).astype(I32)
    gate_ref[0] = jnp.sum(acc_g, axis=1, keepdims=True)


def expert_select(aff_t, cap):
    b, e, l = aff_t.shape
    r = b * e
    tri = jnp.asarray(np.triu(np.ones((SEL_BLK, SEL_BLK), np.float32))).astype(BF16)
    idx, gate = pl.pallas_call(
        functools.partial(_select_kernel, cap=cap),
        grid=(r,),
        in_specs=[pl.BlockSpec((1, 1, l), lambda i: (i, 0, 0)), pl.BlockSpec((SEL_BLK, SEL_BLK), lambda i: (0, 0))],
        out_specs=[pl.BlockSpec((1, cap, 1), lambda i: (i, 0, 0))] * 2,
        out_shape=[jax.ShapeDtypeStruct((r, cap, 1), I32), jax.ShapeDtypeStruct((r, cap, 1), F32)],
        compiler_params=_params("parallel"),
        name="expert_select",
    )(aff_t.reshape(r, 1, l), tri)
    return idx.reshape(r, cap), gate.reshape(r, cap)


def _row_copy(src_ref, src_row, dst_ref, dst_row, sem):
    return pltpu.make_async_copy(src_ref.at[pl.ds(src_row, 1)], dst_ref.at[pl.ds(dst_row, 1)], sem)


def _moe_ffn_kernel(idx_ref, x_hbm, wg_ref, wu_ref, wd_ref, o_ref, stage_ref, xb_ref, act_ref, sem,
                    *, nb, cap, seq, n_f):
    e = pl.program_id(0)
    s = pl.program_id(1)
    ne = pl.num_programs(0)
    tf = wg_ref.shape[2]

    @pl.when(s == 0)
    def _():
        for bi in range(nb):
            base = (bi * ne + e) * cap

            def issue(j, carry):
                _row_copy(x_hbm, bi * seq + idx_ref[base + j], stage_ref, j, sem).start()
                return carry
            lax.fori_loop(0, cap, issue, 0)

            def drain(j, carry):
                _row_copy(x_hbm, 0, stage_ref, j, sem).wait()
                return carry
            lax.fori_loop(0, cap, drain, 0)
            xb_ref[bi * cap:(bi + 1) * cap, :] = stage_ref[...].astype(BF16)

    @pl.when(s < n_f)
    def _():
        x = xb_ref[...]
        g = jnp.dot(x, wg_ref[0].astype(BF16), preferred_element_type=F32)
        u = jnp.dot(x, wu_ref[0].astype(BF16), preferred_element_type=F32)
        act = (jax.nn.silu(g) * u).astype(BF16)
        for f in range(n_f):
            @pl.when(s == f)
            def _(f=f):
                act_ref[:, f * tf:(f + 1) * tf] = act

    @pl.when(s >= n_f)
    def _():
        o_ref[0] = jnp.dot(act_ref[...], wd_ref[0].astype(BF16), preferred_element_type=F32).astype(o_ref.dtype)


def moe_ffn(idx, x_rows, w_gate, w_up, w_down, *, nb, seq, tf=256, tn=512):
    ne, d, ff = w_gate.shape
    cap = idx.shape[0] // (nb * ne)
    n_f = ff // tf
    n_d = d // tn
    grid_spec = pltpu.PrefetchScalarGridSpec(
        num_scalar_prefetch=1,
        grid=(ne, n_f + n_d),
        in_specs=[
            pl.BlockSpec(memory_space=pl.ANY),
            pl.BlockSpec((1, d, tf), lambda e, s, idx: (e, 0, jnp.minimum(s, n_f - 1))),
            pl.BlockSpec((1, d, tf), lambda e, s, idx: (e, 0, jnp.minimum(s, n_f - 1))),
            pl.BlockSpec((1, ff, tn), lambda e, s, idx: (e, 0, jnp.maximum(s - n_f, 0))),
        ],
        out_specs=pl.BlockSpec((1, nb * cap, tn), lambda e, s, idx: (e, 0, jnp.maximum(s - n_f, 0))),
        scratch_shapes=[
            pltpu.VMEM((cap, d), F32),
            pltpu.VMEM((nb * cap, d), BF16),
            pltpu.VMEM((nb * cap, ff), BF16),
            pltpu.SemaphoreType.DMA(()),
        ],
    )
    return pl.pallas_call(
        functools.partial(_moe_ffn_kernel, nb=nb, cap=cap, seq=seq, n_f=n_f),
        grid_spec=grid_spec,
        out_shape=jax.ShapeDtypeStruct((ne, nb * cap, d), BF16),
        compiler_params=_params("arbitrary", "arbitrary"),
        name="moe_ffn",
    )(idx, x_rows, w_gate, w_up, w_down)


def _moe_scatter_kernel(idx_ref, h_in, ys_ref, gate_ref, mod_ref, h_out, rows_ref, sem_in, sem_out, *, nb, cap, seq):
    del h_in
    e = pl.program_id(0)
    ne = pl.num_programs(0)
    n = nb * cap

    def token_row(r):
        bi = r // cap
        return bi * seq + idx_ref[(bi * ne + e) * cap + (r - bi * cap)]

    def issue_in(r, carry):
        _row_copy(h_out, token_row(r), rows_ref, r, sem_in).start()
        return carry
    lax.fori_loop(0, n, issue_in, 0)

    def drain_in(r, carry):
        _row_copy(h_out, 0, rows_ref, r, sem_in).wait()
        return carry
    lax.fori_loop(0, n, drain_in, 0)

    for bi in range(nb):
        sl = slice(bi * cap, (bi + 1) * cap)
        w = gate_ref[0, sl, :] * mod_ref[bi:bi + 1, :]
        rows_ref[sl, :] = rows_ref[sl, :] + w * ys_ref[0, sl, :].astype(F32)

    def issue_out(r, carry):
        _row_copy(rows_ref, r, h_out, token_row(r), sem_out).start()
        return carry
    lax.fori_loop(0, n, issue_out, 0)

    def drain_out(r, carry):
        _row_copy(rows_ref, r, h_out, 0, sem_out).wait()
        return carry
    lax.fori_loop(0, n, drain_out, 0)


def moe_scatter(idx, h_rows, ys, gates, mod, *, nb, seq):
    ne, n, d = ys.shape
    cap = n // nb
    grid_spec = pltpu.PrefetchScalarGridSpec(
        num_scalar_prefetch=1,
        grid=(ne,),
        in_specs=[
            pl.BlockSpec(memory_space=pl.ANY),
            pl.BlockSpec((1, n, d), lambda e, idx: (e, 0, 0)),
            pl.BlockSpec((1, n, 1), lambda e, idx: (e, 0, 0)),
            pl.BlockSpec((nb, d), lambda e, idx: (0, 0)),
        ],
        out_specs=pl.BlockSpec(memory_space=pl.ANY),
        scratch_shapes=[pltpu.VMEM((n, d), F32), pltpu.SemaphoreType.DMA(()), pltpu.SemaphoreType.DMA(())],
    )
    return pl.pallas_call(
        functools.partial(_moe_scatter_kernel, nb=nb, cap=cap, seq=seq),
        grid_spec=grid_spec,
        out_shape=jax.ShapeDtypeStruct(h_rows.shape, F32),
        input_output_aliases={1: 0},
        compiler_params=_params("arbitrary"),
        name="moe_scatter",
    )(idx, h_rows, ys, gates, mod)


def expert_choice_moe_residual(h, norm_g, shift, scale, gate_mod, router_w, w_gate, w_up, w_down):
    b, l, d = h.shape
    ne = router_w.shape[1]
    cap = 2 * l // ne
    x, aff_t = norm_modulate_router(h, norm_g, shift, scale, router_w)
    idx, gates = expert_select(aff_t, cap)
    idx_flat = idx.reshape(-1)
    ys = moe_ffn(idx_flat, x.reshape(b * l, d), w_gate, w_up, w_down, nb=b, seq=l)
    gates_e = jnp.transpose(gates.reshape(b, ne, cap), (1, 0, 2)).reshape(ne, b * cap, 1)
    out = moe_scatter(idx_flat, h.reshape(b * l, d), ys, gates_e, gate_mod, nb=b, seq=l)
    return out.reshape(b, l, d)


def kernel(x, c, ctx, c_ctx, ada_w, ada_b, norm_mix_g, norm_ffn_g, attn_sgu_w_in, attn_sink, sgu_norm_g, sgu_w_s, sgu_b_s, attn_sgu_w_out, hyena_w_in, hyena_conv_w, hyena_conv_b, hyena_filt_w1, hyena_filt_b1, hyena_filt_w2, hyena_filt_b2, hyena_filt_freq, hyena_filt_w3, hyena_bias, hyena_w_out, router_w, expert_w_gate, expert_w_up, expert_w_down, final_norm_g):
    b, l, d = x.shape
    lc = ctx.shape[1]
    c_rows = jnp.concatenate([c, c_ctx[None, :], jnp.zeros((8 - b - 1, d), F32)], axis=0)
    mods = ada_modulation(c_rows, ada_w, ada_b)

    def mod(layer, rows, k):
        return mods[layer, rows, k * d:(k + 1) * d]

    lat = slice(0, b)
    ctx_rows = jnp.full((b,), b, I32)

    a_lat = norm_modulate(x, norm_mix_g[0], mod(0, lat, 0), mod(0, lat, 1), BF16)
    a_ctx = norm_modulate(ctx, norm_mix_g[0], mod(0, ctx_rows, 0), mod(0, ctx_rows, 1), BF16)
    w_in = attn_sgu_w_in[0]
    qkvuz = project(a_lat.reshape(b * l, d), w_in).reshape(b, l, -1)
    ctx_kv = project(a_ctx.reshape(b * lc, d), w_in, n_start=ATTN_WIDTH, n_cols=2 * KV_WIDTH).reshape(b, lc, -1)
    qk = rope_qk(qkvuz)
    o = windowed_attention(qk, qkvuz, ctx_kv, attn_sink[0])
    s = spatial_gating(qkvuz, sgu_norm_g[0], sgu_w_s[0], sgu_b_s[0])
    mixed = jnp.concatenate([o, s], axis=-1).reshape(b * l, -1)
    h = project_residual(mixed, attn_sgu_w_out[0], x.reshape(b * l, d), mod(0, lat, 2), rows_per_batch=l).reshape(b, l, d)
    h = expert_choice_moe_residual(h, norm_ffn_g[0], mod(0, lat, 3), mod(0, lat, 4), mod(0, lat, 5),
                                   router_w[0], expert_w_gate[0], expert_w_up[0], expert_w_down[0])

    a_lat = norm_modulate(h, norm_mix_g[1], mod(1, lat, 0), mod(1, lat, 1), BF16)
    u3 = project(a_lat.reshape(b * l, d), hyena_w_in[0]).reshape(b, l, -1)
    x0, z = hyena_pre(u3, hyena_conv_w[0], hyena_conv_b[0])
    hid = filter_mlp(l, hyena_filt_w1[0], hyena_filt_b1[0], hyena_filt_w2[0], hyena_filt_b2[0], hyena_filt_freq[0])
    xy = hyena_long_conv(z, x0, hid, hyena_filt_w3[0], hyena_bias[0])
    h = project_residual(xy.reshape(b * l, d), hyena_w_out[0], h.reshape(b * l, d), mod(1, lat, 2), rows_per_batch=l).reshape(b, l, d)
    h = expert_choice_moe_residual(h, norm_ffn_g[1], mod(1, lat, 3), mod(1, lat, 4), mod(1, lat, 5),
                                   router_w[1], expert_w_gate[1], expert_w_up[1], expert_w_down[1])
    return final_norm(h, final_norm_g)
```

```python
import functools
import math

import numpy as np
import jax
import jax.numpy as jnp
from jax import lax
from jax.experimental import pallas as pl
from jax.experimental.pallas import tpu as pltpu

F32 = jnp.float32
BF16 = jnp.bfloat16
I32 = jnp.int32
HIGHEST = lax.Precision.HIGHEST

D_MODEL = 4096
GRID_W = 64
NORM_EPS = 1e-6
N_MOD = 6
NEG_INF = -1e30
HEAD_DIM = 128
ATTN_HEADS = 16
ATTN_KV_HEADS = 4
GQ = 4
ATTN_BLOCK = 128
ROPE_THETA = 10000.0
ATTN_WIDTH = 2048
KV_WIDTH = 512
SGU_WIDTH = 2048
SGU_CHUNK = 128
ROPE_WIDTH = ATTN_WIDTH + KV_WIDTH
HYENA_BANDS = 16
HYENA_EMB = 33
HYENA_HID = 64
N_EXPERTS = 16
EXPERT_FF = 1024

LANES = 128
VMEM_LIMIT_BYTES = 56 * 1024 * 1024

FFT_N1 = 64
FFT_N2 = 128
FFT_N = FFT_N1 * FFT_N2
FFT_HALF = FFT_N1 // 2


def _params(*sem):
    return pltpu.CompilerParams(dimension_semantics=sem, vmem_limit_bytes=VMEM_LIMIT_BYTES)


def _ada_kernel(c_ref, w_ref, b_ref, o_ref):
    s = jax.nn.silu(c_ref[...]).astype(BF16)
    o_ref[0] = jnp.dot(s, w_ref[0].astype(BF16), preferred_element_type=F32) + b_ref[0]


def ada_modulation(c_rows, ada_w, ada_b, tn=512):
    depth, d, n = ada_w.shape
    rows = c_rows.shape[0]
    return pl.pallas_call(
        _ada_kernel,
        grid=(depth, n // tn),
        in_specs=[
            pl.BlockSpec((rows, d), lambda l, j: (0, 0)),
            pl.BlockSpec((1, d, tn), lambda l, j: (l, 0, j)),
            pl.BlockSpec((1, 1, tn), lambda l, j: (l, 0, j)),
        ],
        out_specs=pl.BlockSpec((1, rows, tn), lambda l, j: (l, 0, j)),
        out_shape=jax.ShapeDtypeStruct((depth, rows, n), F32),
        compiler_params=_params("parallel", "parallel"),
        name="ada_modulation",
    )(c_rows, ada_w, ada_b.reshape(depth, 1, n))


def _rms(x, g):
    return x * lax.rsqrt(jnp.mean(x * x, axis=-1, keepdims=True) + NORM_EPS) * g


def _norm_mod_kernel(h_ref, g_ref, sh_ref, sc_ref, o_ref):
    y = _rms(h_ref[0], g_ref[...])
    o_ref[0] = (y * (1.0 + sc_ref[0]) + sh_ref[0]).astype(o_ref.dtype)


def norm_modulate(h, g, shift, scale, out_dtype, tl=256):
    b, l, d = h.shape
    tl = min(tl, l)
    return pl.pallas_call(
        _norm_mod_kernel,
        grid=(b, l // tl),
        in_specs=[
            pl.BlockSpec((1, tl, d), lambda i, j: (i, j, 0)),
            pl.BlockSpec((1, d), lambda i, j: (0, 0)),
            pl.BlockSpec((1, 1, d), lambda i, j: (i, 0, 0)),
            pl.BlockSpec((1, 1, d), lambda i, j: (i, 0, 0)),
        ],
        out_specs=pl.BlockSpec((1, tl, d), lambda i, j: (i, j, 0)),
        out_shape=jax.ShapeDtypeStruct((b, l, d), out_dtype),
        compiler_params=_params("parallel", "parallel"),
        name="norm_modulate",
    )(h, g.reshape(1, d), shift.reshape(b, 1, d), scale.reshape(b, 1, d))


def _norm_mod_router_kernel(h_ref, g_ref, sh_ref, sc_ref, rw_ref, o_ref, aff_ref):
    y = _rms(h_ref[0], g_ref[...])
    a = y * (1.0 + sc_ref[0]) + sh_ref[0]
    o_ref[0] = a
    lt = lax.dot_general(rw_ref[...], a, (((1,), (1,)), ((), ())), precision=HIGHEST, preferred_element_type=F32)
    m = jnp.max(lt, axis=0, keepdims=True)
    p = jnp.exp(lt - m)
    aff_ref[0] = p / jnp.sum(p, axis=0, keepdims=True)


def norm_modulate_router(h, g, shift, scale, router_w, tl=256):
    b, l, d = h.shape
    e = router_w.shape[1]
    return pl.pallas_call(
        _norm_mod_router_kernel,
        grid=(b, l // tl),
        in_specs=[
            pl.BlockSpec((1, tl, d), lambda i, j: (i, j, 0)),
            pl.BlockSpec((1, d), lambda i, j: (0, 0)),
            pl.BlockSpec((1, 1, d), lambda i, j: (i, 0, 0)),
            pl.BlockSpec((1, 1, d), lambda i, j: (i, 0, 0)),
            pl.BlockSpec((e, d), lambda i, j: (0, 0)),
        ],
        out_specs=[
            pl.BlockSpec((1, tl, d), lambda i, j: (i, j, 0)),
            pl.BlockSpec((1, e, tl), lambda i, j: (i, 0, j)),
        ],
        out_shape=[jax.ShapeDtypeStruct((b, l, d), F32), jax.ShapeDtypeStruct((b, e, l), F32)],
        compiler_params=_params("parallel", "parallel"),
        name="norm_modulate_router",
    )(h, g.reshape(1, d), shift.reshape(b, 1, d), scale.reshape(b, 1, d), router_w.T)


def _final_norm_kernel(h_ref, g_ref, o_ref):
    o_ref[0] = _rms(h_ref[0], g_ref[...])


def final_norm(h, g, tl=256):
    b, l, d = h.shape
    return pl.pallas_call(
        _final_norm_kernel,
        grid=(b, l // tl),
        in_specs=[pl.BlockSpec((1, tl, d), lambda i, j: (i, j, 0)), pl.BlockSpec((1, d), lambda i, j: (0, 0))],
        out_specs=pl.BlockSpec((1, tl, d), lambda i, j: (i, j, 0)),
        out_shape=jax.ShapeDtypeStruct((b, l, d), F32),
        compiler_params=_params("parallel", "parallel"),
        name="final_norm",
    )(h, g.reshape(1, d))


def _mm_kernel(a_ref, w_ref, o_ref, wb_ref):
    @pl.when(pl.program_id(1) == 0)
    def _():
        wb_ref[...] = w_ref[...].astype(BF16)

    o_ref[...] = jnp.dot(a_ref[...], wb_ref[...], preferred_element_type=F32).astype(o_ref.dtype)


def _mm_res_kernel(a_ref, w_ref, res_ref, gate_ref, o_ref, wb_ref):
    @pl.when(pl.program_id(1) == 0)
    def _():
        wb_ref[...] = w_ref[...].astype(BF16)

    acc = jnp.dot(a_ref[...], wb_ref[...], preferred_element_type=F32)
    o_ref[...] = res_ref[...] + gate_ref[0] * acc


def project(a, w, *, n_start=0, n_cols=None, out_dtype=BF16, tm=512, tn=512):
    m, k = a.shape
    n_cols = w.shape[1] - n_start if n_cols is None else n_cols
    tm = min(tm, m)
    off = n_start // tn
    return pl.pallas_call(
        _mm_kernel,
        grid=(n_cols // tn, m // tm),
        in_specs=[
            pl.BlockSpec((tm, k), lambda j, i: (i, 0)),
            pl.BlockSpec((k, tn), lambda j, i: (0, off + j)),
        ],
        out_specs=pl.BlockSpec((tm, tn), lambda j, i: (i, j)),
        out_shape=jax.ShapeDtypeStruct((m, n_cols), out_dtype),
        scratch_shapes=[pltpu.VMEM((k, tn), BF16)],
        compiler_params=_params("arbitrary", "arbitrary"),
        name="project",
    )(a, w)


def project_residual(a, w, res, gate, *, rows_per_batch, tm=512, tn=512):
    m, k = a.shape
    n = w.shape[1]
    nb = gate.shape[0]
    tiles_per_batch = rows_per_batch // tm
    return pl.pallas_call(
        _mm_res_kernel,
        grid=(n // tn, m // tm),
        in_specs=[
            pl.BlockSpec((tm, k), lambda j, i: (i, 0)),
            pl.BlockSpec((k, tn), lambda j, i: (0, j)),
            pl.BlockSpec((tm, tn), lambda j, i: (i, j)),
            pl.BlockSpec((1, 1, tn), lambda j, i: (i // tiles_per_batch, 0, j)),
        ],
        out_specs=pl.BlockSpec((tm, tn), lambda j, i: (i, j)),
        out_shape=jax.ShapeDtypeStruct((m, n), F32),
        scratch_shapes=[pltpu.VMEM((k, tn), BF16)],
        compiler_params=_params("arbitrary", "arbitrary"),
        name="project_residual",
    )(a, w, res, gate.reshape(nb, 1, n))


def _rope_tables(l):
    rows = l // GRID_W
    row = np.repeat(np.arange(rows), GRID_W).astype(np.float32)
    col = np.tile(np.arange(GRID_W), rows).astype(np.float32)
    half = HEAD_DIM // 2
    inv = (ROPE_THETA ** (-np.arange(0, half, 2, dtype=np.float32) / half)).astype(np.float32)
    ang_r = row[:, None] * inv[None, :]
    ang_c = col[:, None] * inv[None, :]
    ang = np.concatenate([ang_r, ang_r, ang_c, ang_c], axis=1)
    sign = np.tile(np.concatenate([-np.ones(half // 2), np.ones(half // 2)]), 2).astype(np.float32)
    return jnp.asarray(np.cos(ang), F32), jnp.asarray(np.sin(ang) * sign[None, :], F32)


def _rope_kernel(x_ref, cos_ref, sin_ref, o_ref):
    cos = cos_ref[...]
    sin = sin_ref[...]
    quarter = HEAD_DIM // 4
    lane = lax.broadcasted_iota(I32, cos.shape, 1)
    first = (lane % (2 * quarter)) < quarter
    for j in range(x_ref.shape[2] // HEAD_DIM):
        sl = slice(j * HEAD_DIM, (j + 1) * HEAD_DIM)
        x = x_ref[0, :, sl].astype(F32)
        partner = jnp.where(first, pltpu.roll(x, HEAD_DIM - quarter, 1), pltpu.roll(x, quarter, 1))
        o_ref[0, :, sl] = (x * cos + partner * sin).astype(o_ref.dtype)


def rope_qk(qkvuz, tl=256):
    b, l, _ = qkvuz.shape
    cos, sin = _rope_tables(l)
    return pl.pallas_call(
        _rope_kernel,
        grid=(b, l // tl),
        in_specs=[
            pl.BlockSpec((1, tl, ROPE_WIDTH), lambda i, j: (i, j, 0)),
            pl.BlockSpec((tl, HEAD_DIM), lambda i, j: (j, 0)),
            pl.BlockSpec((tl, HEAD_DIM), lambda i, j: (j, 0)),
        ],
        out_specs=pl.BlockSpec((1, tl, ROPE_WIDTH), lambda i, j: (i, j, 0)),
        out_shape=jax.ShapeDtypeStruct((b, l, ROPE_WIDTH), BF16),
        compiler_params=_params("parallel", "parallel"),
        name="rope_qk",
    )(qkvuz, cos, sin)


def _attn_kernel(sink_ref, q_ref, kp_ref, k0_ref, kn_ref, vp_ref, v0_ref, vn_ref, kx_ref, vx_ref, o_ref, *, nb):
    h = pl.program_id(1)
    n = pl.program_id(2)
    blk = ATTN_BLOCK
    scale = HEAD_DIM ** -0.5
    q = jnp.concatenate([q_ref[0, :, g * HEAD_DIM:(g + 1) * HEAD_DIM] for g in range(GQ)], axis=0)

    def scores(k):
        return lax.dot_general(q, k, (((1,), (1,)), ((), ())), preferred_element_type=F32) * scale

    sp, s0, sn, sx = scores(kp_ref[0]), scores(k0_ref[0]), scores(kn_ref[0]), scores(kx_ref[0])
    qi = lax.broadcasted_iota(I32, (GQ * blk, blk), 0) % blk
    ki = lax.broadcasted_iota(I32, (GQ * blk, blk), 1)
    sp = jnp.where(ki >= qi + jnp.where(n > 0, 0, blk), sp, NEG_INF)
    sn = jnp.where(ki <= qi - jnp.where(n < nb - 1, 0, blk), sn, NEG_INF)
    sink = jnp.concatenate([jnp.full((blk, 1), sink_ref[h * GQ + g], F32) for g in range(GQ)], axis=0)

    def fold(parts, op):
        tiles = [p[:, t * blk:(t + 1) * blk] for p in parts for t in range(p.shape[1] // blk)]
        return functools.reduce(op, tiles)

    m = jnp.maximum(jnp.max(fold([sp, s0, sn, sx], jnp.maximum), axis=-1, keepdims=True), sink)
    pp, p0, pn, px = jnp.exp(sp - m), jnp.exp(s0 - m), jnp.exp(sn - m), jnp.exp(sx - m)
    denom = jnp.sum(fold([pp, p0, pn, px], jnp.add), axis=-1, keepdims=True) + jnp.exp(sink - m)

    def pv(p, v):
        return jnp.dot(p.astype(BF16), v, preferred_element_type=F32)

    o = pv(pp, vp_ref[0]) + pv(p0, v0_ref[0]) + pv(pn, vn_ref[0]) + pv(px, vx_ref[0])
    o = o / denom
    for g in range(GQ):
        o_ref[0, :, g * HEAD_DIM:(g + 1) * HEAD_DIM] = o[g * blk:(g + 1) * blk].astype(o_ref.dtype)


def windowed_attention(qk, qkvuz, ctx_kv, sink):
    b, l, _ = qk.shape
    lc = ctx_kv.shape[1]
    nb = l // ATTN_BLOCK
    kcol = ATTN_WIDTH // HEAD_DIM
    vcol = (ATTN_WIDTH + KV_WIDTH) // HEAD_DIM
    blk = (1, ATTN_BLOCK, HEAD_DIM)
    prev = lambda n: jnp.maximum(n - 1, 0)
    nxt = lambda n: jnp.minimum(n + 1, nb - 1)
    return pl.pallas_call(
        functools.partial(_attn_kernel, nb=nb),
        grid=(b, ATTN_KV_HEADS, nb),
        in_specs=[
            pl.BlockSpec(memory_space=pltpu.SMEM),
            pl.BlockSpec((1, ATTN_BLOCK, GQ * HEAD_DIM), lambda i, h, n: (i, n, h)),
            pl.BlockSpec(blk, lambda i, h, n: (i, prev(n), kcol + h)),
            pl.BlockSpec(blk, lambda i, h, n: (i, n, kcol + h)),
            pl.BlockSpec(blk, lambda i, h, n: (i, nxt(n), kcol + h)),
            pl.BlockSpec(blk, lambda i, h, n: (i, prev(n), vcol + h)),
            pl.BlockSpec(blk, lambda i, h, n: (i, n, vcol + h)),
            pl.BlockSpec(blk, lambda i, h, n: (i, nxt(n), vcol + h)),
            pl.BlockSpec((1, lc, HEAD_DIM), lambda i, h, n: (i, 0, h)),
            pl.BlockSpec((1, lc, HEAD_DIM), lambda i, h, n: (i, 0, ATTN_KV_HEADS + h)),
        ],
        out_specs=pl.BlockSpec((1, ATTN_BLOCK, GQ * HEAD_DIM), lambda i, h, n: (i, n, h)),
        out_shape=jax.ShapeDtypeStruct((b, l, ATTN_WIDTH), BF16),
        compiler_params=_params("parallel", "parallel", "parallel"),
        name="windowed_attention",
    )(sink, qk, qk, qk, qk, qkvuz, qkvuz, qkvuz, ctx_kv, ctx_kv)


def _gelu(x):
    return 0.5 * x * (1.0 + lax.erf(x * (1.0 / math.sqrt(2.0))))


def _sgu_kernel(u_ref, z_ref, g_ref, ws_ref, bs_ref, o_ref):
    for gi in range(ws_ref.shape[0]):
        sl = slice(gi * LANES, (gi + 1) * LANES)
        z = _gelu(z_ref[0, :, sl].astype(F32))
        mu = jnp.mean(z, axis=-1, keepdims=True)
        zc = z - mu
        var = jnp.mean(zc * zc, axis=-1, keepdims=True)
        zn = zc * lax.rsqrt(var + NORM_EPS) * g_ref[:, sl]
        mixed = jnp.dot(ws_ref[gi].astype(BF16), zn.astype(BF16), preferred_element_type=F32) + bs_ref[:, sl]
        o_ref[0, :, sl] = (_gelu(u_ref[0, :, sl].astype(F32)) * mixed).astype(o_ref.dtype)


def spatial_gating(qkvuz, g, w_s, b_s, tw=1024):
    b, l, _ = qkvuz.shape
    groups = w_s.shape[0]
    ucol = (ATTN_WIDTH + 2 * KV_WIDTH) // tw
    zcol = (ATTN_WIDTH + 2 * KV_WIDTH + SGU_WIDTH) // tw
    gpt = tw // LANES
    bias = jnp.repeat(jnp.transpose(b_s), LANES, axis=1)
    return pl.pallas_call(
        _sgu_kernel,
        grid=(b, l // SGU_CHUNK, SGU_WIDTH // tw),
        in_specs=[
            pl.BlockSpec((1, SGU_CHUNK, tw), lambda i, n, j: (i, n, ucol + j)),
            pl.BlockSpec((1, SGU_CHUNK, tw), lambda i, n, j: (i, n, zcol + j)),
            pl.BlockSpec((1, tw), lambda i, n, j: (0, j)),
            pl.BlockSpec((gpt, SGU_CHUNK, SGU_CHUNK), lambda i, n, j: (j, 0, 0)),
            pl.BlockSpec((SGU_CHUNK, tw), lambda i, n, j: (0, j)),
        ],
        out_specs=pl.BlockSpec((1, SGU_CHUNK, tw), lambda i, n, j: (i, n, j)),
        out_shape=jax.ShapeDtypeStruct((b, l, SGU_WIDTH), BF16),
        compiler_params=_params("parallel", "parallel", "parallel"),
        name="spatial_gating",
    )(qkvuz, qkvuz, g.reshape(1, groups * LANES), w_s, bias)


HALO = 16


def _hyena_pre_kernel(*refs, tl):
    (p0, m0, n0, p1, m1, n1, p2, m2, n2, w0, w1, w2, b0, b1, b2, x0_ref, z_ref) = refs
    first = pl.program_id(1) == 0
    last = pl.program_id(1) == pl.num_programs(1) - 1
    rows = lax.broadcasted_iota(I32, m0.shape[1:], 0)

    def conv(p_ref, m_ref, n_ref, w_ref, b_ref):
        x = m_ref[0].astype(F32)
        prev_row = jnp.where(first, 0.0, p_ref[0, HALO - 1:HALO, :].astype(F32))
        next_row = jnp.where(last, 0.0, n_ref[0, 0:1, :].astype(F32))
        xm = jnp.where(rows == 0, prev_row, pltpu.roll(x, 1, 0))
        xp = jnp.where(rows == tl - 1, next_row, pltpu.roll(x, tl - 1, 0))
        return xm * w_ref[0:1, :] + x * w_ref[1:2, :] + xp * w_ref[2:3, :] + b_ref[...]

    x0_ref[0] = conv(p0, m0, n0, w0, b0).astype(x0_ref.dtype)
    x1 = conv(p1, m1, n1, w1, b1)
    v = conv(p2, m2, n2, w2, b2)
    z_ref[0] = (v * x1).astype(z_ref.dtype)


def hyena_pre(u3, conv_w, conv_b, tl=512, tc=512):
    b, l, w3 = u3.shape
    c = w3 // 3
    ct = c // tc
    hb = tl // HALO
    nhb = l // HALO
    in_specs = []
    for part in range(3):
        in_specs += [
            pl.BlockSpec((1, HALO, tc), lambda i, t, j, part=part: (i, jnp.maximum(t * hb - 1, 0), part * ct + j)),
            pl.BlockSpec((1, tl, tc), lambda i, t, j, part=part: (i, t, part * ct + j)),
            pl.BlockSpec((1, HALO, tc), lambda i, t, j, part=part: (i, jnp.minimum((t + 1) * hb, nhb - 1), part * ct + j)),
        ]
    for part in range(3):
        in_specs.append(pl.BlockSpec((3, tc), lambda i, t, j, part=part: (0, part * ct + j)))
    for part in range(3):
        in_specs.append(pl.BlockSpec((1, tc), lambda i, t, j, part=part: (0, part * ct + j)))
    out_spec = pl.BlockSpec((1, tl, tc), lambda i, t, j: (i, t, j))
    return pl.pallas_call(
        functools.partial(_hyena_pre_kernel, tl=tl),
        grid=(b, l // tl, ct),
        in_specs=in_specs,
        out_specs=[out_spec, out_spec],
        out_shape=[jax.ShapeDtypeStruct((b, l, c), BF16), jax.ShapeDtypeStruct((b, l, c), BF16)],
        compiler_params=_params("parallel", "parallel", "parallel"),
        name="hyena_pre",
    )(*([u3] * 9), *([conv_w] * 3), *([conv_b.reshape(1, w3)] * 3))


def _filter_features(l):
    pos = np.arange(l, dtype=np.float32)
    t01 = pos / np.float32(max(l - 1, 1))
    bands = np.linspace(1e-4, HYENA_BANDS - 1, HYENA_BANDS, dtype=np.float32)
    ang = np.float32(2.0 * math.pi / l) * pos[:, None] * bands[None, :]
    feats = np.concatenate([t01[:, None], np.cos(ang), -np.sin(ang)], axis=-1).astype(np.float32)
    feats = np.pad(feats, ((0, 0), (0, HYENA_HID - HYENA_EMB)))
    rev = np.concatenate([np.zeros((1, HYENA_HID), np.float32), feats[:0:-1]], axis=0)
    return np.concatenate([feats, rev], axis=0)


def _filter_mlp_kernel(f_ref, w1_ref, b1_ref, w2_ref, b2_ref, fr_ref, o_ref):
    h = jnp.dot(f_ref[...], w1_ref[...], precision=HIGHEST, preferred_element_type=F32) + b1_ref[...]
    h = jnp.sin(fr_ref[0:1, :] * h)
    h = jnp.dot(h, w2_ref[...], precision=HIGHEST, preferred_element_type=F32) + b2_ref[...]
    o_ref[...] = jnp.sin(fr_ref[1:2, :] * h)


def filter_mlp(l, w1, b1, w2, b2, freq):
    feats = jnp.asarray(_filter_features(l))
    w1p = jnp.pad(w1, ((0, HYENA_HID - HYENA_EMB), (0, 0)))
    hid = pl.pallas_call(
        _filter_mlp_kernel,
        out_shape=jax.ShapeDtypeStruct((2 * l, HYENA_HID), F32),
        compiler_params=pltpu.CompilerParams(vmem_limit_bytes=VMEM_LIMIT_BYTES),
        name="filter_mlp",
    )(feats, w1p, b1.reshape(1, -1), w2, b2.reshape(1, -1), freq)
    return jnp.concatenate([hid[:l], hid[l:]], axis=1)


def _dft_constants():
    n1 = np.arange(FFT_N1, dtype=np.float64)
    n2 = np.arange(FFT_N2, dtype=np.float64)
    k1 = n1
    two_pi = 2.0 * np.pi
    theta = two_pi * (k1[None, :, None] * n1[None, None, :] / FFT_N1 + n2[:, None, None] * k1[None, :, None] / FFT_N)
    gr, gi = np.cos(theta), -np.sin(theta)
    g_cplx = np.concatenate([np.concatenate([gr[:, :, :FFT_HALF], -gi[:, :, :FFT_HALF]], axis=2),
                             np.concatenate([gi[:, :, :FFT_HALF], gr[:, :, :FFT_HALF]], axis=2)], axis=1)
    g_real = np.concatenate([gr, gi], axis=1)
    ang2 = two_pi * np.outer(n2, n2) / FFT_N2
    cm, sm = np.cos(ang2), np.sin(ang2)
    f2 = np.block([[cm, sm], [-sm, cm]])
    f2i = np.block([[cm, -sm], [sm, cm]])
    phi = two_pi * (n1[None, :FFT_HALF, None] * k1[None, None, :] / FFT_N1 + n2[:, None, None] * k1[None, None, :] / FFT_N)
    cp, sp = np.cos(phi) / FFT_N, np.sin(phi) / FFT_N
    hm = np.concatenate([np.concatenate([cp, -sp], axis=2), np.concatenate([sp, cp], axis=2)], axis=1)

    def pair(m):
        p, r, c = m.shape[0] // 2, m.shape[1], m.shape[2]
        out = np.zeros((p, 2 * r, 2 * c), np.float64)
        out[:, :r, :c] = m[0::2]
        out[:, r:, c:] = m[1::2]
        return out

    as_bf16 = lambda a: jnp.asarray(a.astype(np.float32)).astype(BF16)
    return as_bf16(pair(g_cplx)), as_bf16(pair(g_real)), as_bf16(f2), as_bf16(f2i), as_bf16(pair(hm))


def _hyena_fft_kernel(z_ref, x0_ref, hid_ref, w3f_ref, w3b_ref, dl_ref, bias_ref, gc_ref, gk_ref, f2_ref, f2i_ref, hm_ref,
                      o_ref, t_ref, a_ref, kf_ref, b_ref, *, seq):
    two_n1 = 2 * FFT_N1
    two_n2 = 2 * FFT_N2
    ct = dl_ref.shape[1]

    def stage1(g_ref):
        def body(p, carry):
            n2 = 2 * p
            x = jnp.concatenate([t_ref[pl.ds(n2, FFT_N1, stride=FFT_N2), :],
                                 t_ref[pl.ds(n2 + 1, FFT_N1, stride=FFT_N2), :]], axis=0).astype(BF16)
            rows = pl.ds(pl.multiple_of(p * 2 * two_n1, 2 * two_n1), 2 * two_n1)
            a_ref[rows, :] = jnp.dot(g_ref[p], x, preferred_element_type=F32)
            return carry
        lax.fori_loop(0, FFT_N2 // 2, body, 0, unroll=4)

    def stage2(mode):
        def body(k1, carry):
            ar = a_ref[pl.ds(k1, FFT_N2, stride=two_n1), :]
            ai = a_ref[pl.ds(FFT_N1 + k1, FFT_N2, stride=two_n1), :]
            s = jnp.concatenate([ar, ai], axis=0).astype(BF16)
            x = jnp.dot(f2_ref[...], s, preferred_element_type=F32)
            rows = pl.ds(pl.multiple_of(k1 * two_n2, two_n2), two_n2)
            if mode == "filter":
                kf_ref[rows, :] = x.astype(kf_ref.dtype)
            else:
                kf = kf_ref[rows, :].astype(F32)
                xr, xi = x[:FFT_N2], x[FFT_N2:]
                kr, ki = kf[:FFT_N2], kf[FFT_N2:]
                y = jnp.concatenate([xr * kr - xi * ki, xr * ki + xi * kr], axis=0).astype(BF16)
                b_ref[rows, :] = jnp.dot(f2i_ref[...], y, preferred_element_type=F32)
            return carry
        lax.fori_loop(0, FFT_N1, body, 0, unroll=8)

    def stage_out():
        def body(p, carry):
            n2 = 2 * p
            s = jnp.concatenate([b_ref[pl.ds(n2, FFT_N1, stride=two_n2), :],
                                 b_ref[pl.ds(FFT_N2 + n2, FFT_N1, stride=two_n2), :],
                                 b_ref[pl.ds(n2 + 1, FFT_N1, stride=two_n2), :],
                                 b_ref[pl.ds(FFT_N2 + n2 + 1, FFT_N1, stride=two_n2), :]], axis=0).astype(BF16)
            y = jnp.dot(hm_ref[p], s, preferred_element_type=F32)
            for q in range(2):
                lo = q * 2 * FFT_HALF
                t_ref[pl.ds(n2 + q, FFT_HALF, stride=FFT_N2), :] = y[lo:lo + FFT_HALF]
                t_ref[pl.ds(seq + n2 + q, FFT_HALF, stride=FFT_N2), :] = y[lo + FFT_HALF:lo + 2 * FFT_HALF]
            return carry
        lax.fori_loop(0, FFT_N2 // 2, body, 0, unroll=4)

    rb = 512
    span = float(max(seq - 1, 1))

    def build_filter(r, acc):
        r0 = pl.multiple_of(r * rb, rb)
        t_idx = r0 + lax.broadcasted_iota(I32, (rb, ct), 0)
        tf = t_idx.astype(F32)
        hid = hid_ref[pl.ds(r0, rb), :]
        hf = jnp.dot(hid, w3f_ref[...], precision=HIGHEST, preferred_element_type=F32)
        hf = hf * jnp.exp(-(tf / span) * dl_ref[...])
        hb = jnp.dot(hid, w3b_ref[...], precision=HIGHEST, preferred_element_type=F32)
        hb = jnp.where(t_idx == 0, 0.0, hb * jnp.exp(-((float(seq) - tf) / span) * dl_ref[...]))
        t_ref[pl.ds(r0, rb), :] = hf
        t_ref[pl.ds(seq + r0, rb), :] = hb
        return acc + jnp.sum(jnp.abs(hf), axis=0, keepdims=True) + jnp.sum(jnp.abs(hb), axis=0, keepdims=True)

    norm = lax.fori_loop(0, seq // rb, build_filter, jnp.zeros((1, ct), F32))
    stage1(gk_ref)
    stage2("filter")

    t_ref[0:seq, :] = z_ref[0].astype(F32)
    t_ref[seq:2 * seq, :] = z_ref[1].astype(F32)
    stage1(gc_ref)
    stage2("data")
    stage_out()

    inv_norm = 1.0 / norm
    for bi in range(2):
        def finish(r, carry, bi=bi):
            r0 = pl.multiple_of(r * rb, rb)
            zb = z_ref[bi, pl.ds(r0, rb), :].astype(F32)
            y = t_ref[pl.ds(bi * seq + r0, rb), :] * inv_norm + zb * bias_ref[...]
            o_ref[bi, pl.ds(r0, rb), :] = (x0_ref[bi, pl.ds(r0, rb), :].astype(F32) * y).astype(o_ref.dtype)
            return carry
        lax.fori_loop(0, seq // rb, finish, 0)


def hyena_long_conv(z, x0, hid, w3, bias, ct=128):
    b, l, c = z.shape
    assert b == 2 and 2 * l == FFT_N
    gc, gk, f2, f2i, hm = _dft_constants()
    lo, hi = math.log(1e-2) / 1.5, math.log(1e-2) / 0.3
    deltas = jnp.asarray(np.abs(np.linspace(lo, hi, c, dtype=np.float32)).reshape(1, c))
    zeros = jnp.zeros((HYENA_HID, c), F32)
    w3f = jnp.concatenate([w3[:, :c], zeros], axis=0)
    w3b = jnp.concatenate([zeros, w3[:, c:]], axis=0)
    nct = c // ct
    full = lambda a: pl.BlockSpec(a.shape, lambda j: (0,) * a.ndim, pipeline_mode=pl.Buffered(1))
    return pl.pallas_call(
        functools.partial(_hyena_fft_kernel, seq=l),
        grid=(nct,),
        in_specs=[
            pl.BlockSpec((b, l, ct), lambda j: (0, 0, j)),
            pl.BlockSpec((b, l, ct), lambda j: (0, 0, j)),
            full(hid),
            pl.BlockSpec((2 * HYENA_HID, ct), lambda j: (0, j)),
            pl.BlockSpec((2 * HYENA_HID, ct), lambda j: (0, j)),
            pl.BlockSpec((1, ct), lambda j: (0, j)),
            pl.BlockSpec((1, ct), lambda j: (0, j)),
            full(gc), full(gk), full(f2), full(f2i), full(hm),
        ],
        out_specs=pl.BlockSpec((b, l, ct), lambda j: (0, 0, j)),
        out_shape=jax.ShapeDtypeStruct((b, l, c), BF16),
        scratch_shapes=[
            pltpu.VMEM((2 * l, ct), F32),
            pltpu.VMEM((FFT_N2 * 2 * FFT_N1, ct), F32),
            pltpu.VMEM((FFT_N1 * 2 * FFT_N2, ct), BF16),
            pltpu.VMEM((FFT_N1 * 2 * FFT_N2, ct), F32),
        ],
        compiler_params=_params("parallel"),
        name="hyena_long_conv",
    )(z, x0, hid, w3f, w3b, deltas, bias.reshape(1, c), gc, gk, f2, f2i, hm)


SEL_BLK = 512


def _select_kernel(aff_ref, tri_ref, idx_ref, gate_ref, *, cap):
    x = aff_ref[0]
    l = x.shape[1]
    bits = pltpu.bitcast(x, I32)

    def search(i, prefix):
        cand = prefix | lax.shift_left(jnp.int32(1), 30 - i)
        cnt = jnp.sum((bits >= cand).astype(I32), axis=1, keepdims=True)
        return jnp.where(cnt >= cap, cand, prefix)

    tau = lax.fori_loop(0, 31, search, jnp.zeros((1, 1), I32))
    gt = bits > tau
    eq = bits == tau
    need = (cap - jnp.sum(gt.astype(I32), axis=1, keepdims=True)).astype(F32)

    def prefix_count(mask):
        run = jnp.zeros((1, 1), F32)
        parts = []
        for blk in range(l // SEL_BLK):
            m = mask[:, blk * SEL_BLK:(blk + 1) * SEL_BLK].astype(BF16)
            c = jnp.dot(m, tri_ref[...], preferred_element_type=F32) + run
            parts.append(c)
            run = c[:, SEL_BLK - 1:SEL_BLK]
        return jnp.concatenate(parts, axis=1)

    eq_rank = prefix_count(jnp.where(eq, 1.0, 0.0))
    sel = jnp.logical_or(gt, jnp.logical_and(eq, eq_rank <= need))
    pos = jnp.where(sel, prefix_count(jnp.where(sel, 1.0, 0.0)), 0.0)

    slot = (lax.broadcasted_iota(I32, (cap, SEL_BLK), 0) + 1).astype(F32)
    tok = lax.broadcasted_iota(I32, (1, SEL_BLK), 1).astype(F32)
    acc_i = jnp.zeros((cap, SEL_BLK), F32)
    acc_g = jnp.zeros((cap, SEL_BLK), F32)
    for blk in range(l // SEL_BLK):
        sl = slice(blk * SEL_BLK, (blk + 1) * SEL_BLK)
        hit = pos[:, sl] == slot
        acc_i = acc_i + jnp.where(hit, tok + float(blk * SEL_BLK), 0.0)
        acc_g = acc_g + jnp.where(hit, x[:, sl], 0.0)
    idx_ref[0] = jnp.sum(acc_i, axis=1, keepdims=True).astype(I32)
    gate_ref[0] = jnp.sum(acc_g, axis=1, keepdims=True)


def expert_select(aff_t, cap):
    b, e, l = aff_t.shape
    r = b * e
    tri = jnp.asarray(np.triu(np.ones((SEL_BLK, SEL_BLK), np.float32))).astype(BF16)
    idx, gate = pl.pallas_call(
        functools.partial(_select_kernel, cap=cap),
        grid=(r,),
        in_specs=[pl.BlockSpec((1, 1, l), lambda i: (i, 0, 0)), pl.BlockSpec((SEL_BLK, SEL_BLK), lambda i: (0, 0))],
        out_specs=[pl.BlockSpec((1, cap, 1), lambda i: (i, 0, 0))] * 2,
        out_shape=[jax.ShapeDtypeStruct((r, cap, 1), I32), jax.ShapeDtypeStruct((r, cap, 1), F32)],
        compiler_params=_params("parallel"),
        name="expert_select",
    )(aff_t.reshape(r, 1, l), tri)
    return idx.reshape(r, cap), gate.reshape(r, cap)


def _row_copy(src_ref, src_row, dst_ref, dst_row, sem):
    return pltpu.make_async_copy(src_ref.at[pl.ds(src_row, 1)], dst_ref.at[pl.ds(dst_row, 1)], sem)


def _moe_ffn_kernel(idx_ref, x_hbm, wg_ref, wu_ref, wd_ref, o_ref, stage_ref, xb_ref, act_ref, sem,
                    *, nb, cap, seq, n_f):
    e = pl.program_id(0)
    s = pl.program_id(1)
    ne = pl.num_programs(0)
    tf = wg_ref.shape[2]
    slot = e % 2
    has_next = e + 1 < ne

    def issue(expert, bi):
        base = (bi * ne + expert) * cap

        def body(j, carry):
            _row_copy(x_hbm, bi * seq + idx_ref[base + j], stage_ref, j, sem).start()
            return carry
        lax.fori_loop(0, cap, body, 0, unroll=8)

    def land(dst_slot, bi):
        def body(j, carry):
            _row_copy(x_hbm, 0, stage_ref, j, sem).wait()
            return carry
        lax.fori_loop(0, cap, body, 0, unroll=8)
        xb_ref[dst_slot, bi * cap:(bi + 1) * cap, :] = stage_ref[...].astype(BF16)

    @pl.when(jnp.logical_and(e == 0, s == 0))
    def _():
        for bi in range(nb):
            issue(0, bi)
            land(0, bi)

    @pl.when(jnp.logical_and(has_next, s == 0))
    def _():
        issue(e + 1, 0)

    @pl.when(jnp.logical_and(has_next, s == n_f + 1))
    def _():
        land(1 - slot, 0)
        issue(e + 1, 1)

    @pl.when(jnp.logical_and(has_next, s == pl.num_programs(1) - 1))
    def _():
        land(1 - slot, 1)

    @pl.when(s < n_f)
    def _():
        x = xb_ref[slot]
        g = jnp.dot(x, wg_ref[0].astype(BF16), preferred_element_type=F32)
        u = jnp.dot(x, wu_ref[0].astype(BF16), preferred_element_type=F32)
        act = (jax.nn.silu(g) * u).astype(BF16)
        for f in range(n_f):
            @pl.when(s == f)
            def _(f=f):
                act_ref[:, f * tf:(f + 1) * tf] = act

    @pl.when(s >= n_f)
    def _():
        o_ref[0] = jnp.dot(act_ref[...], wd_ref[0].astype(BF16), preferred_element_type=F32).astype(o_ref.dtype)


def moe_ffn(idx, x_rows, w_gate, w_up, w_down, *, layer, nb, seq, tf=256, tn=512):
    _, ne, d, ff = w_gate.shape
    assert nb == 2
    cap = idx.shape[0] // (nb * ne)
    n_f = ff // tf
    n_d = d // tn
    assert n_d >= 3
    grid_spec = pltpu.PrefetchScalarGridSpec(
        num_scalar_prefetch=1,
        grid=(ne, n_f + n_d),
        in_specs=[
            pl.BlockSpec(memory_space=pl.ANY),
            pl.BlockSpec((None, 1, d, tf), lambda e, s, idx: (layer, e, 0, jnp.minimum(s, n_f - 1))),
            pl.BlockSpec((None, 1, d, tf), lambda e, s, idx: (layer, e, 0, jnp.minimum(s, n_f - 1))),
            pl.BlockSpec((None, 1, ff, tn), lambda e, s, idx: (layer, e, 0, jnp.maximum(s - n_f, 0))),
        ],
        out_specs=pl.BlockSpec((1, nb * cap, tn), lambda e, s, idx: (e, 0, jnp.maximum(s - n_f, 0))),
        scratch_shapes=[
            pltpu.VMEM((cap, d), F32),
            pltpu.VMEM((2, nb * cap, d), BF16),
            pltpu.VMEM((nb * cap, ff), BF16),
            pltpu.SemaphoreType.DMA(()),
        ],
    )
    return pl.pallas_call(
        functools.partial(_moe_ffn_kernel, nb=nb, cap=cap, seq=seq, n_f=n_f),
        grid_spec=grid_spec,
        out_shape=jax.ShapeDtypeStruct((ne, nb * cap, d), BF16),
        compiler_params=_params("arbitrary", "arbitrary"),
        name="moe_ffn",
    )(idx, x_rows, w_gate, w_up, w_down)


def _moe_scatter_kernel(idx_ref, h_in, ys_ref, gate_ref, mod_ref, h_out, rows_ref, sem_in, sem_out, *, nb, cap, seq):
    del h_in
    e = pl.program_id(0)
    ne = pl.num_programs(0)
    sub = 16

    def for_rows(fn):
        def body(j, carry):
            fn(j)
            return carry
        lax.fori_loop(0, cap, body, 0, unroll=8)

    def token_row(bi, j):
        return bi * seq + idx_ref[(bi * ne + e) * cap + j]

    for bi in range(nb):
        for_rows(lambda j, bi=bi: _row_copy(h_out, token_row(bi, j), rows_ref, bi * cap + j, sem_in.at[bi]).start())
    for bi in range(nb):
        for_rows(lambda j, bi=bi: _row_copy(h_out, 0, rows_ref, bi * cap + j, sem_in.at[bi]).wait())

        def update(i, carry, bi=bi):
            r0 = pl.multiple_of(bi * cap + i * sub, sub)
            w = gate_ref[0, pl.ds(r0, sub), :] * mod_ref[bi:bi + 1, :]
            rows_ref[pl.ds(r0, sub), :] = rows_ref[pl.ds(r0, sub), :] + w * ys_ref[0, pl.ds(r0, sub), :].astype(F32)
            return carry
        lax.fori_loop(0, cap // sub, update, 0, unroll=2)
        for_rows(lambda j, bi=bi: _row_copy(rows_ref, bi * cap + j, h_out, token_row(bi, j), sem_out.at[bi]).start())
    for bi in range(nb):
        for_rows(lambda j, bi=bi: _row_copy(rows_ref, bi * cap + j, h_out, 0, sem_out.at[bi]).wait())


def moe_scatter(idx, h_rows, ys, gates, mod, *, nb, seq):
    ne, n, d = ys.shape
    cap = n // nb
    grid_spec = pltpu.PrefetchScalarGridSpec(
        num_scalar_prefetch=1,
        grid=(ne,),
        in_specs=[
            pl.BlockSpec(memory_space=pl.ANY),
            pl.BlockSpec((1, n, d), lambda e, idx: (e, 0, 0)),
            pl.BlockSpec((1, n, 1), lambda e, idx: (e, 0, 0)),
            pl.BlockSpec((nb, d), lambda e, idx: (0, 0)),
        ],
        out_specs=pl.BlockSpec(memory_space=pl.ANY),
        scratch_shapes=[pltpu.VMEM((n, d), F32), pltpu.SemaphoreType.DMA((nb,)), pltpu.SemaphoreType.DMA((nb,))],
    )
    return pl.pallas_call(
        functools.partial(_moe_scatter_kernel, nb=nb, cap=cap, seq=seq),
        grid_spec=grid_spec,
        out_shape=jax.ShapeDtypeStruct(h_rows.shape, F32),
        input_output_aliases={1: 0},
        compiler_params=_params("arbitrary"),
        name="moe_scatter",
    )(idx, h_rows, ys, gates, mod)


def expert_choice_moe_residual(h, norm_g, shift, scale, gate_mod, router_w, w_gate, w_up, w_down, layer):
    b, l, d = h.shape
    ne = router_w.shape[1]
    cap = 2 * l // ne
    x, aff_t = norm_modulate_router(h, norm_g, shift, scale, router_w)
    idx, gates = expert_select(aff_t, cap)
    idx_flat = idx.reshape(-1)
    ys = moe_ffn(idx_flat, x.reshape(b * l, d), w_gate, w_up, w_down, layer=layer, nb=b, seq=l)
    gates_e = jnp.transpose(gates.reshape(b, ne, cap), (1, 0, 2)).reshape(ne, b * cap, 1)
    out = moe_scatter(idx_flat, h.reshape(b * l, d), ys, gates_e, gate_mod, nb=b, seq=l)
    return out.reshape(b, l, d)


def kernel(x, c, ctx, c_ctx, ada_w, ada_b, norm_mix_g, norm_ffn_g, attn_sgu_w_in, attn_sink, sgu_norm_g, sgu_w_s, sgu_b_s, attn_sgu_w_out, hyena_w_in, hyena_conv_w, hyena_conv_b, hyena_filt_w1, hyena_filt_b1, hyena_filt_w2, hyena_filt_b2, hyena_filt_freq, hyena_filt_w3, hyena_bias, hyena_w_out, router_w, expert_w_gate, expert_w_up, expert_w_down, final_norm_g):
    b, l, d = x.shape
    lc = ctx.shape[1]
    c_rows = jnp.concatenate([c, c_ctx[None, :], jnp.zeros((8 - b - 1, d), F32)], axis=0)
    mods = ada_modulation(c_rows, ada_w, ada_b)

    def mod(layer, rows, k):
        return mods[layer, rows, k * d:(k + 1) * d]

    lat = slice(0, b)
    ctx_rows = jnp.full((b,), b, I32)

    a_lat = norm_modulate(x, norm_mix_g[0], mod(0, lat, 0), mod(0, lat, 1), BF16)
    a_ctx = norm_modulate(ctx, norm_mix_g[0], mod(0, ctx_rows, 0), mod(0, ctx_rows, 1), BF16)
    w_in = attn_sgu_w_in[0]
    qkvuz = project(a_lat.reshape(b * l, d), w_in).reshape(b, l, -1)
    ctx_kv = project(a_ctx.reshape(b * lc, d), w_in, n_start=ATTN_WIDTH, n_cols=2 * KV_WIDTH).reshape(b, lc, -1)
    qk = rope_qk(qkvuz)
    o = windowed_attention(qk, qkvuz, ctx_kv, attn_sink[0])
    s = spatial_gating(qkvuz, sgu_norm_g[0], sgu_w_s[0], sgu_b_s[0])
    mixed = jnp.concatenate([o, s], axis=-1).reshape(b * l, -1)
    h = project_residual(mixed, attn_sgu_w_out[0], x.reshape(b * l, d), mod(0, lat, 2), rows_per_batch=l).reshape(b, l, d)
    h = expert_choice_moe_residual(h, norm_ffn_g[0], mod(0, lat, 3), mod(0, lat, 4), mod(0, lat, 5),
                                   router_w[0], expert_w_gate, expert_w_up, expert_w_down, 0)

    a_lat = norm_modulate(h, norm_mix_g[1], mod(1, lat, 0), mod(1, lat, 1), BF16)
    u3 = project(a_lat.reshape(b * l, d), hyena_w_in[0]).reshape(b, l, -1)
    x0, z = hyena_pre(u3, hyena_conv_w[0], hyena_conv_b[0])
    hid = filter_mlp(l, hyena_filt_w1[0], hyena_filt_b1[0], hyena_filt_w2[0], hyena_filt_b2[0], hyena_filt_freq[0])
    xy = hyena_long_conv(z, x0, hid, hyena_filt_w3[0], hyena_bias[0])
    h = project_residual(xy.reshape(b * l, d), hyena_w_out[0], h.reshape(b * l, d), mod(1, lat, 2), rows_per_batch=l).reshape(b, l, d)
    h = expert_choice_moe_residual(h, norm_ffn_g[1], mod(1, lat, 3), mod(1, lat, 4), mod(1, lat, 5),
                                   router_w[1], expert_w_gate, expert_w_up, expert_w_down, 1)
    return final_norm(h, final_norm_g)
```

```python
import functools
import math

import numpy as np
import jax
import jax.numpy as jnp
from jax import lax
from jax.experimental import pallas as pl
from jax.experimental.pallas import tpu as pltpu

F32 = jnp.float32
BF16 = jnp.bfloat16
I32 = jnp.int32
HIGHEST = lax.Precision.HIGHEST

D_MODEL = 4096
GRID_W = 64
NORM_EPS = 1e-6
N_MOD = 6
NEG_INF = -1e30
HEAD_DIM = 128
ATTN_HEADS = 16
ATTN_KV_HEADS = 4
GQ = 4
ATTN_BLOCK = 128
ROPE_THETA = 10000.0
ATTN_WIDTH = 2048
KV_WIDTH = 512
SGU_WIDTH = 2048
SGU_CHUNK = 128
ROPE_WIDTH = ATTN_WIDTH + KV_WIDTH
HYENA_BANDS = 16
HYENA_EMB = 33
HYENA_HID = 64
N_EXPERTS = 16
EXPERT_FF = 1024

LANES = 128
VMEM_LIMIT_BYTES = 56 * 1024 * 1024

FFT_N1 = 64
FFT_N2 = 128
FFT_N = FFT_N1 * FFT_N2
FFT_HALF = FFT_N1 // 2


def _params(*sem):
    return pltpu.CompilerParams(dimension_semantics=sem, vmem_limit_bytes=VMEM_LIMIT_BYTES)


def _ada_kernel(c_ref, w_ref, b_ref, o_ref):
    s = jax.nn.silu(c_ref[...]).astype(BF16)
    o_ref[0] = jnp.dot(s, w_ref[0].astype(BF16), preferred_element_type=F32) + b_ref[0]


def ada_modulation(c_rows, ada_w, ada_b, tn=512):
    depth, d, n = ada_w.shape
    rows = c_rows.shape[0]
    return pl.pallas_call(
        _ada_kernel,
        grid=(depth, n // tn),
        in_specs=[
            pl.BlockSpec((rows, d), lambda l, j: (0, 0)),
            pl.BlockSpec((1, d, tn), lambda l, j: (l, 0, j)),
            pl.BlockSpec((1, 1, tn), lambda l, j: (l, 0, j)),
        ],
        out_specs=pl.BlockSpec((1, rows, tn), lambda l, j: (l, 0, j)),
        out_shape=jax.ShapeDtypeStruct((depth, rows, n), F32),
        compiler_params=_params("parallel", "parallel"),
        name="ada_modulation",
    )(c_rows, ada_w, ada_b.reshape(depth, 1, n))


def _rms(x, g):
    return x * lax.rsqrt(jnp.mean(x * x, axis=-1, keepdims=True) + NORM_EPS) * g


def _norm_mod_kernel(h_ref, g_ref, sh_ref, sc_ref, o_ref):
    y = _rms(h_ref[0], g_ref[...])
    o_ref[0] = (y * (1.0 + sc_ref[0]) + sh_ref[0]).astype(o_ref.dtype)


def norm_modulate(h, g, shift, scale, out_dtype, tl=256):
    b, l, d = h.shape
    tl = min(tl, l)
    return pl.pallas_call(
        _norm_mod_kernel,
        grid=(b, l // tl),
        in_specs=[
            pl.BlockSpec((1, tl, d), lambda i, j: (i, j, 0)),
            pl.BlockSpec((1, d), lambda i, j: (0, 0)),
            pl.BlockSpec((1, 1, d), lambda i, j: (i, 0, 0)),
            pl.BlockSpec((1, 1, d), lambda i, j: (i, 0, 0)),
        ],
        out_specs=pl.BlockSpec((1, tl, d), lambda i, j: (i, j, 0)),
        out_shape=jax.ShapeDtypeStruct((b, l, d), out_dtype),
        compiler_params=_params("parallel", "parallel"),
        name="norm_modulate",
    )(h, g.reshape(1, d), shift.reshape(b, 1, d), scale.reshape(b, 1, d))


def _norm_mod_router_kernel(h_ref, g_ref, sh_ref, sc_ref, rw_ref, o_ref, aff_ref):
    y = _rms(h_ref[0], g_ref[...])
    a = y * (1.0 + sc_ref[0]) + sh_ref[0]
    o_ref[0] = a
    lt = lax.dot_general(rw_ref[...], a, (((1,), (1,)), ((), ())), precision=HIGHEST, preferred_element_type=F32)
    m = jnp.max(lt, axis=0, keepdims=True)
    p = jnp.exp(lt - m)
    aff_ref[0] = p / jnp.sum(p, axis=0, keepdims=True)


def norm_modulate_router(h, g, shift, scale, router_w, tl=256):
    b, l, d = h.shape
    e = router_w.shape[1]
    return pl.pallas_call(
        _norm_mod_router_kernel,
        grid=(b, l // tl),
        in_specs=[
            pl.BlockSpec((1, tl, d), lambda i, j: (i, j, 0)),
            pl.BlockSpec((1, d), lambda i, j: (0, 0)),
            pl.BlockSpec((1, 1, d), lambda i, j: (i, 0, 0)),
            pl.BlockSpec((1, 1, d), lambda i, j: (i, 0, 0)),
            pl.BlockSpec((e, d), lambda i, j: (0, 0)),
        ],
        out_specs=[
            pl.BlockSpec((1, tl, d), lambda i, j: (i, j, 0)),
            pl.BlockSpec((1, e, tl), lambda i, j: (i, 0, j)),
        ],
        out_shape=[jax.ShapeDtypeStruct((b, l, d), F32), jax.ShapeDtypeStruct((b, e, l), F32)],
        compiler_params=_params("parallel", "parallel"),
        name="norm_modulate_router",
    )(h, g.reshape(1, d), shift.reshape(b, 1, d), scale.reshape(b, 1, d), router_w.T)


def _final_norm_kernel(h_ref, g_ref, o_ref):
    o_ref[0] = _rms(h_ref[0], g_ref[...])


def final_norm(h, g, tl=256):
    b, l, d = h.shape
    return pl.pallas_call(
        _final_norm_kernel,
        grid=(b, l // tl),
        in_specs=[pl.BlockSpec((1, tl, d), lambda i, j: (i, j, 0)), pl.BlockSpec((1, d), lambda i, j: (0, 0))],
        out_specs=pl.BlockSpec((1, tl, d), lambda i, j: (i, j, 0)),
        out_shape=jax.ShapeDtypeStruct((b, l, d), F32),
        compiler_params=_params("parallel", "parallel"),
        name="final_norm",
    )(h, g.reshape(1, d))


def _mm_kernel(a_ref, w_ref, o_ref, wb_ref):
    @pl.when(pl.program_id(1) == 0)
    def _():
        wb_ref[...] = w_ref[...].astype(BF16)

    o_ref[...] = jnp.dot(a_ref[...], wb_ref[...], preferred_element_type=F32).astype(o_ref.dtype)


def _mm_res_kernel(a_ref, w_ref, res_ref, gate_ref, o_ref, wb_ref):
    @pl.when(pl.program_id(1) == 0)
    def _():
        wb_ref[...] = w_ref[...].astype(BF16)

    acc = jnp.dot(a_ref[...], wb_ref[...], preferred_element_type=F32)
    o_ref[...] = res_ref[...] + gate_ref[0] * acc


def project(a, w, *, n_start=0, n_cols=None, out_dtype=BF16, tm=1024, tn=512):
    m, k = a.shape
    n_cols = w.shape[1] - n_start if n_cols is None else n_cols
    tm = min(tm, m)
    off = n_start // tn
    return pl.pallas_call(
        _mm_kernel,
        grid=(n_cols // tn, m // tm),
        in_specs=[
            pl.BlockSpec((tm, k), lambda j, i: (i, 0)),
            pl.BlockSpec((k, tn), lambda j, i: (0, off + j)),
        ],
        out_specs=pl.BlockSpec((tm, tn), lambda j, i: (i, j)),
        out_shape=jax.ShapeDtypeStruct((m, n_cols), out_dtype),
        scratch_shapes=[pltpu.VMEM((k, tn), BF16)],
        compiler_params=_params("arbitrary", "arbitrary"),
        name="project",
    )(a, w)


def project_residual(a, w, res, gate, *, rows_per_batch, tm=1024, tn=512):
    m, k = a.shape
    n = w.shape[1]
    nb = gate.shape[0]
    assert rows_per_batch % tm == 0, "a row tile must not straddle two batch entries (one gate row per tile)"
    tiles_per_batch = rows_per_batch // tm
    return pl.pallas_call(
        _mm_res_kernel,
        grid=(n // tn, m // tm),
        in_specs=[
            pl.BlockSpec((tm, k), lambda j, i: (i, 0)),
            pl.BlockSpec((k, tn), lambda j, i: (0, j)),
            pl.BlockSpec((tm, tn), lambda j, i: (i, j)),
            pl.BlockSpec((1, 1, tn), lambda j, i: (i // tiles_per_batch, 0, j)),
        ],
        out_specs=pl.BlockSpec((tm, tn), lambda j, i: (i, j)),
        out_shape=jax.ShapeDtypeStruct((m, n), F32),
        scratch_shapes=[pltpu.VMEM((k, tn), BF16)],
        compiler_params=_params("arbitrary", "arbitrary"),
        name="project_residual",
    )(a, w, res, gate.reshape(nb, 1, n))


def _rope_tables(l):
    rows = l // GRID_W
    row = np.repeat(np.arange(rows), GRID_W).astype(np.float32)
    col = np.tile(np.arange(GRID_W), rows).astype(np.float32)
    half = HEAD_DIM // 2
    inv = (ROPE_THETA ** (-np.arange(0, half, 2, dtype=np.float32) / half)).astype(np.float32)
    ang_r = row[:, None] * inv[None, :]
    ang_c = col[:, None] * inv[None, :]
    ang = np.concatenate([ang_r, ang_r, ang_c, ang_c], axis=1)
    sign = np.tile(np.concatenate([-np.ones(half // 2), np.ones(half // 2)]), 2).astype(np.float32)
    return jnp.asarray(np.cos(ang), F32), jnp.asarray(np.sin(ang) * sign[None, :], F32)


def _rope_kernel(x_ref, cos_ref, sin_ref, o_ref):
    cos = cos_ref[...]
    sin = sin_ref[...]
    quarter = HEAD_DIM // 4
    lane = lax.broadcasted_iota(I32, cos.shape, 1)
    first = (lane % (2 * quarter)) < quarter
    for j in range(x_ref.shape[2] // HEAD_DIM):
        sl = slice(j * HEAD_DIM, (j + 1) * HEAD_DIM)
        x = x_ref[0, :, sl].astype(F32)
        partner = jnp.where(first, pltpu.roll(x, HEAD_DIM - quarter, 1), pltpu.roll(x, quarter, 1))
        o_ref[0, :, sl] = (x * cos + partner * sin).astype(o_ref.dtype)


def rope_qk(qkvuz, tl=256):
    b, l, _ = qkvuz.shape
    cos, sin = _rope_tables(l)
    return pl.pallas_call(
        _rope_kernel,
        grid=(b, l // tl),
        in_specs=[
            pl.BlockSpec((1, tl, ROPE_WIDTH), lambda i, j: (i, j, 0)),
            pl.BlockSpec((tl, HEAD_DIM), lambda i, j: (j, 0)),
            pl.BlockSpec((tl, HEAD_DIM), lambda i, j: (j, 0)),
        ],
        out_specs=pl.BlockSpec((1, tl, ROPE_WIDTH), lambda i, j: (i, j, 0)),
        out_shape=jax.ShapeDtypeStruct((b, l, ROPE_WIDTH), BF16),
        compiler_params=_params("parallel", "parallel"),
        name="rope_qk",
    )(qkvuz, cos, sin)


ATTN_QB = 2


def _attn_kernel(sink_ref, q_ref, kp_ref, k0_ref, kn_ref, vp_ref, v0_ref, vn_ref, kx_ref, vx_ref, o_ref):
    h = pl.program_id(1)
    step = pl.program_id(2)
    n_steps = pl.num_programs(2)
    blk = ATTN_BLOCK
    scale = HEAD_DIM ** -0.5
    rows = lambda ref, j: ref[0, j * blk:(j + 1) * blk, :]
    keys = [kp_ref[0]] + [rows(k0_ref, j) for j in range(ATTN_QB)] + [kn_ref[0]]
    vals = [vp_ref[0]] + [rows(v0_ref, j) for j in range(ATTN_QB)] + [vn_ref[0]]
    qi = lax.broadcasted_iota(I32, (GQ * blk, blk), 0) % blk
    ki = lax.broadcasted_iota(I32, (GQ * blk, blk), 1)
    sink = jnp.concatenate([jnp.full((blk, 1), sink_ref[h * GQ + g], F32) for g in range(GQ)], axis=0)

    def fold(parts, op):
        tiles = [p[:, t * blk:(t + 1) * blk] for p in parts for t in range(p.shape[1] // blk)]
        return functools.reduce(op, tiles)

    def pv(p, v):
        return jnp.dot(p.astype(BF16), v, preferred_element_type=F32)

    for j in range(ATTN_QB):
        q = jnp.concatenate([q_ref[0, j * blk:(j + 1) * blk, g * HEAD_DIM:(g + 1) * HEAD_DIM] for g in range(GQ)], axis=0)

        def scores(k):
            return lax.dot_general(q, k, (((1,), (1,)), ((), ())), preferred_element_type=F32) * scale

        sp, s0, sn, sx = scores(keys[j]), scores(keys[j + 1]), scores(keys[j + 2]), scores(kx_ref[0])
        off_p = jnp.where(step > 0, 0, blk) if j == 0 else 0
        off_n = jnp.where(step < n_steps - 1, 0, blk) if j == ATTN_QB - 1 else 0
        sp = jnp.where(ki >= qi + off_p, sp, NEG_INF)
        sn = jnp.where(ki <= qi - off_n, sn, NEG_INF)
        m = jnp.maximum(jnp.max(fold([sp, s0, sn, sx], jnp.maximum), axis=-1, keepdims=True), sink)
        pp, p0, pn, px = jnp.exp(sp - m), jnp.exp(s0 - m), jnp.exp(sn - m), jnp.exp(sx - m)
        denom = jnp.sum(fold([pp, p0, pn, px], jnp.add), axis=-1, keepdims=True) + jnp.exp(sink - m)
        o = pv(pp, vals[j]) + pv(p0, vals[j + 1]) + pv(pn, vals[j + 2]) + pv(px, vx_ref[0])
        o = o / denom
        for g in range(GQ):
            o_ref[0, j * blk:(j + 1) * blk, g * HEAD_DIM:(g + 1) * HEAD_DIM] = o[g * blk:(g + 1) * blk].astype(o_ref.dtype)


def windowed_attention(qk, qkvuz, ctx_kv, sink):
    b, l, _ = qk.shape
    lc = ctx_kv.shape[1]
    nb = l // ATTN_BLOCK
    n_steps = nb // ATTN_QB
    kcol = ATTN_WIDTH // HEAD_DIM
    vcol = (ATTN_WIDTH + KV_WIDTH) // HEAD_DIM
    edge = (1, ATTN_BLOCK, HEAD_DIM)
    cur = (1, ATTN_QB * ATTN_BLOCK, HEAD_DIM)
    prev = lambda s: jnp.maximum(s * ATTN_QB - 1, 0)
    nxt = lambda s: jnp.minimum((s + 1) * ATTN_QB, nb - 1)
    return pl.pallas_call(
        _attn_kernel,
        grid=(b, ATTN_KV_HEADS, n_steps),
        in_specs=[
            pl.BlockSpec(memory_space=pltpu.SMEM),
            pl.BlockSpec((1, ATTN_QB * ATTN_BLOCK, GQ * HEAD_DIM), lambda i, h, s: (i, s, h)),
            pl.BlockSpec(edge, lambda i, h, s: (i, prev(s), kcol + h)),
            pl.BlockSpec(cur, lambda i, h, s: (i, s, kcol + h)),
            pl.BlockSpec(edge, lambda i, h, s: (i, nxt(s), kcol + h)),
            pl.BlockSpec(edge, lambda i, h, s: (i, prev(s), vcol + h)),
            pl.BlockSpec(cur, lambda i, h, s: (i, s, vcol + h)),
            pl.BlockSpec(edge, lambda i, h, s: (i, nxt(s), vcol + h)),
            pl.BlockSpec((1, lc, HEAD_DIM), lambda i, h, s: (i, 0, h)),
            pl.BlockSpec((1, lc, HEAD_DIM), lambda i, h, s: (i, 0, ATTN_KV_HEADS + h)),
        ],
        out_specs=pl.BlockSpec((1, ATTN_QB * ATTN_BLOCK, GQ * HEAD_DIM), lambda i, h, s: (i, s, h)),
        out_shape=jax.ShapeDtypeStruct((b, l, ATTN_WIDTH), BF16),
        compiler_params=_params("parallel", "parallel", "parallel"),
        name="windowed_attention",
    )(sink, qk, qk, qk, qk, qkvuz, qkvuz, qkvuz, ctx_kv, ctx_kv)


def _gelu(x):
    return 0.5 * x * (1.0 + lax.erf(x * (1.0 / math.sqrt(2.0))))


def _sgu_kernel(u_ref, z_ref, g_ref, ws_ref, bs_ref, o_ref):
    for gi in range(ws_ref.shape[0]):
        sl = slice(gi * LANES, (gi + 1) * LANES)
        z = _gelu(z_ref[0, :, sl].astype(F32))
        mu = jnp.mean(z, axis=-1, keepdims=True)
        zc = z - mu
        var = jnp.mean(zc * zc, axis=-1, keepdims=True)
        zn = zc * lax.rsqrt(var + NORM_EPS) * g_ref[:, sl]
        mixed = jnp.dot(ws_ref[gi].astype(BF16), zn.astype(BF16), preferred_element_type=F32) + bs_ref[:, sl]
        o_ref[0, :, sl] = (_gelu(u_ref[0, :, sl].astype(F32)) * mixed).astype(o_ref.dtype)


def spatial_gating(qkvuz, g, w_s, b_s, tw=1024):
    b, l, _ = qkvuz.shape
    groups = w_s.shape[0]
    ucol = (ATTN_WIDTH + 2 * KV_WIDTH) // tw
    zcol = (ATTN_WIDTH + 2 * KV_WIDTH + SGU_WIDTH) // tw
    gpt = tw // LANES
    bias = jnp.repeat(jnp.transpose(b_s), LANES, axis=1)
    return pl.pallas_call(
        _sgu_kernel,
        grid=(b, l // SGU_CHUNK, SGU_WIDTH // tw),
        in_specs=[
            pl.BlockSpec((1, SGU_CHUNK, tw), lambda i, n, j: (i, n, ucol + j)),
            pl.BlockSpec((1, SGU_CHUNK, tw), lambda i, n, j: (i, n, zcol + j)),
            pl.BlockSpec((1, tw), lambda i, n, j: (0, j)),
            pl.BlockSpec((gpt, SGU_CHUNK, SGU_CHUNK), lambda i, n, j: (j, 0, 0)),
            pl.BlockSpec((SGU_CHUNK, tw), lambda i, n, j: (0, j)),
        ],
        out_specs=pl.BlockSpec((1, SGU_CHUNK, tw), lambda i, n, j: (i, n, j)),
        out_shape=jax.ShapeDtypeStruct((b, l, SGU_WIDTH), BF16),
        compiler_params=_params("parallel", "parallel", "parallel"),
        name="spatial_gating",
    )(qkvuz, qkvuz, g.reshape(1, groups * LANES), w_s, bias)


HALO = 16


def _hyena_pre_kernel(*refs, tl):
    (p0, m0, n0, p1, m1, n1, p2, m2, n2, w0, w1, w2, b0, b1, b2, x0_ref, z_ref) = refs
    first = pl.program_id(1) == 0
    last = pl.program_id(1) == pl.num_programs(1) - 1
    rows = lax.broadcasted_iota(I32, m0.shape[1:], 0)

    def conv(p_ref, m_ref, n_ref, w_ref, b_ref):
        x = m_ref[0].astype(F32)
        prev_row = jnp.where(first, 0.0, p_ref[0, HALO - 1:HALO, :].astype(F32))
        next_row = jnp.where(last, 0.0, n_ref[0, 0:1, :].astype(F32))
        xm = jnp.where(rows == 0, prev_row, pltpu.roll(x, 1, 0))
        xp = jnp.where(rows == tl - 1, next_row, pltpu.roll(x, tl - 1, 0))
        return xm * w_ref[0:1, :] + x * w_ref[1:2, :] + xp * w_ref[2:3, :] + b_ref[...]

    x0_ref[0] = conv(p0, m0, n0, w0, b0).astype(x0_ref.dtype)
    x1 = conv(p1, m1, n1, w1, b1)
    v = conv(p2, m2, n2, w2, b2)
    z_ref[0] = (v * x1).astype(z_ref.dtype)


def hyena_pre(u3, conv_w, conv_b, tl=512, tc=512):
    b, l, w3 = u3.shape
    c = w3 // 3
    ct = c // tc
    hb = tl // HALO
    nhb = l // HALO
    in_specs = []
    for part in range(3):
        in_specs += [
            pl.BlockSpec((1, HALO, tc), lambda i, t, j, part=part: (i, jnp.maximum(t * hb - 1, 0), part * ct + j)),
            pl.BlockSpec((1, tl, tc), lambda i, t, j, part=part: (i, t, part * ct + j)),
            pl.BlockSpec((1, HALO, tc), lambda i, t, j, part=part: (i, jnp.minimum((t + 1) * hb, nhb - 1), part * ct + j)),
        ]
    for part in range(3):
        in_specs.append(pl.BlockSpec((3, tc), lambda i, t, j, part=part: (0, part * ct + j)))
    for part in range(3):
        in_specs.append(pl.BlockSpec((1, tc), lambda i, t, j, part=part: (0, part * ct + j)))
    out_spec = pl.BlockSpec((1, tl, tc), lambda i, t, j: (i, t, j))
    return pl.pallas_call(
        functools.partial(_hyena_pre_kernel, tl=tl),
        grid=(b, l // tl, ct),
        in_specs=in_specs,
        out_specs=[out_spec, out_spec],
        out_shape=[jax.ShapeDtypeStruct((b, l, c), BF16), jax.ShapeDtypeStruct((b, l, c), BF16)],
        compiler_params=_params("parallel", "parallel", "parallel"),
        name="hyena_pre",
    )(*([u3] * 9), *([conv_w] * 3), *([conv_b.reshape(1, w3)] * 3))


def _filter_features(l):
    pos = np.arange(l, dtype=np.float32)
    t01 = pos / np.float32(max(l - 1, 1))
    bands = np.linspace(1e-4, HYENA_BANDS - 1, HYENA_BANDS, dtype=np.float32)
    ang = np.float32(2.0 * math.pi / l) * pos[:, None] * bands[None, :]
    feats = np.concatenate([t01[:, None], np.cos(ang), -np.sin(ang)], axis=-1).astype(np.float32)
    feats = np.pad(feats, ((0, 0), (0, HYENA_HID - HYENA_EMB)))
    rev = np.concatenate([np.zeros((1, HYENA_HID), np.float32), feats[:0:-1]], axis=0)
    return np.concatenate([feats, rev], axis=0)


def _filter_mlp_kernel(f_ref, w1_ref, b1_ref, w2_ref, b2_ref, fr_ref, o_ref):
    h = jnp.dot(f_ref[...], w1_ref[...], precision=HIGHEST, preferred_element_type=F32) + b1_ref[...]
    h = jnp.sin(fr_ref[0:1, :] * h)
    h = jnp.dot(h, w2_ref[...], precision=HIGHEST, preferred_element_type=F32) + b2_ref[...]
    o_ref[...] = jnp.sin(fr_ref[1:2, :] * h)


def filter_mlp(l, w1, b1, w2, b2, freq):
    feats = jnp.asarray(_filter_features(l))
    w1p = jnp.pad(w1, ((0, HYENA_HID - HYENA_EMB), (0, 0)))
    hid = pl.pallas_call(
        _filter_mlp_kernel,
        out_shape=jax.ShapeDtypeStruct((2 * l, HYENA_HID), F32),
        compiler_params=pltpu.CompilerParams(vmem_limit_bytes=VMEM_LIMIT_BYTES),
        name="filter_mlp",
    )(feats, w1p, b1.reshape(1, -1), w2, b2.reshape(1, -1), freq)
    return jnp.concatenate([hid[:l], hid[l:]], axis=1)


def _dft_constants():
    n1 = np.arange(FFT_N1, dtype=np.float64)
    n2 = np.arange(FFT_N2, dtype=np.float64)
    k1 = n1
    two_pi = 2.0 * np.pi
    theta = two_pi * (k1[None, :, None] * n1[None, None, :] / FFT_N1 + n2[:, None, None] * k1[None, :, None] / FFT_N)
    gr, gi = np.cos(theta), -np.sin(theta)
    g_cplx = np.concatenate([np.concatenate([gr[:, :, :FFT_HALF], -gi[:, :, :FFT_HALF]], axis=2),
                             np.concatenate([gi[:, :, :FFT_HALF], gr[:, :, :FFT_HALF]], axis=2)], axis=1)
    g_real = np.concatenate([gr, gi], axis=1)
    ang2 = two_pi * np.outer(n2, n2) / FFT_N2
    cm, sm = np.cos(ang2), np.sin(ang2)
    f2 = np.block([[cm, sm], [-sm, cm]])
    f2i = np.block([[cm, -sm], [sm, cm]])
    phi = two_pi * (n1[None, :FFT_HALF, None] * k1[None, None, :] / FFT_N1 + n2[:, None, None] * k1[None, None, :] / FFT_N)
    cp, sp = np.cos(phi) / FFT_N, np.sin(phi) / FFT_N
    hm = np.concatenate([np.concatenate([cp, -sp], axis=2), np.concatenate([sp, cp], axis=2)], axis=1)

    def pair(m):
        p, r, c = m.shape[0] // 2, m.shape[1], m.shape[2]
        out = np.zeros((p, 2 * r, 2 * c), np.float64)
        out[:, :r, :c] = m[0::2]
        out[:, r:, c:] = m[1::2]
        return out

    as_bf16 = lambda a: jnp.asarray(a.astype(np.float32)).astype(BF16)
    return as_bf16(pair(g_cplx)), as_bf16(pair(g_real)), as_bf16(f2), as_bf16(f2i), as_bf16(pair(hm))


def _hyena_fft_kernel(z_ref, x0_ref, hid_ref, w3f_ref, w3b_ref, dl_ref, bias_ref, gc_ref, gk_ref, f2_ref, f2i_ref, hm_ref,
                      o_ref, t_ref, a_ref, kf_ref, b_ref, *, seq):
    two_n1 = 2 * FFT_N1
    two_n2 = 2 * FFT_N2
    ct = dl_ref.shape[1]

    def stage1(g_ref):
        def body(p, carry):
            n2 = 2 * p
            x = jnp.concatenate([t_ref[pl.ds(n2, FFT_N1, stride=FFT_N2), :],
                                 t_ref[pl.ds(n2 + 1, FFT_N1, stride=FFT_N2), :]], axis=0).astype(BF16)
            rows = pl.ds(pl.multiple_of(p * 2 * two_n1, 2 * two_n1), 2 * two_n1)
            a_ref[rows, :] = jnp.dot(g_ref[p], x, preferred_element_type=F32)
            return carry
        lax.fori_loop(0, FFT_N2 // 2, body, 0, unroll=4)

    def stage2(mode):
        def body(k1, carry):
            ar = a_ref[pl.ds(k1, FFT_N2, stride=two_n1), :]
            ai = a_ref[pl.ds(FFT_N1 + k1, FFT_N2, stride=two_n1), :]
            s = jnp.concatenate([ar, ai], axis=0).astype(BF16)
            x = jnp.dot(f2_ref[...], s, preferred_element_type=F32)
            rows = pl.ds(pl.multiple_of(k1 * two_n2, two_n2), two_n2)
            if mode == "filter":
                kf_ref[rows, :] = x.astype(kf_ref.dtype)
            else:
                kf = kf_ref[rows, :].astype(F32)
                xr, xi = x[:FFT_N2], x[FFT_N2:]
                kr, ki = kf[:FFT_N2], kf[FFT_N2:]
                y = jnp.concatenate([xr * kr - xi * ki, xr * ki + xi * kr], axis=0).astype(BF16)
                b_ref[rows, :] = jnp.dot(f2i_ref[...], y, preferred_element_type=F32)
            return carry
        lax.fori_loop(0, FFT_N1, body, 0, unroll=8)

    def stage_out():
        def body(p, carry):
            n2 = 2 * p
            s = jnp.concatenate([b_ref[pl.ds(n2, FFT_N1, stride=two_n2), :],
                                 b_ref[pl.ds(FFT_N2 + n2, FFT_N1, stride=two_n2), :],
                                 b_ref[pl.ds(n2 + 1, FFT_N1, stride=two_n2), :],
                                 b_ref[pl.ds(FFT_N2 + n2 + 1, FFT_N1, stride=two_n2), :]], axis=0).astype(BF16)
            y = jnp.dot(hm_ref[p], s, preferred_element_type=F32)
            for q in range(2):
                lo = q * 2 * FFT_HALF
                t_ref[pl.ds(n2 + q, FFT_HALF, stride=FFT_N2), :] = y[lo:lo + FFT_HALF]
                t_ref[pl.ds(seq + n2 + q, FFT_HALF, stride=FFT_N2), :] = y[lo + FFT_HALF:lo + 2 * FFT_HALF]
            return carry
        lax.fori_loop(0, FFT_N2 // 2, body, 0, unroll=4)

    rb = 512
    span = float(max(seq - 1, 1))

    def build_filter(r, acc):
        r0 = pl.multiple_of(r * rb, rb)
        t_idx = r0 + lax.broadcasted_iota(I32, (rb, ct), 0)
        tf = t_idx.astype(F32)
        hid = hid_ref[pl.ds(r0, rb), :].astype(BF16)
        hf = jnp.dot(hid, w3f_ref[...].astype(BF16), preferred_element_type=F32)
        hf = hf * jnp.exp(-(tf / span) * dl_ref[...])
        hb = jnp.dot(hid, w3b_ref[...].astype(BF16), preferred_element_type=F32)
        hb = jnp.where(t_idx == 0, 0.0, hb * jnp.exp(-((float(seq) - tf) / span) * dl_ref[...]))
        t_ref[pl.ds(r0, rb), :] = hf
        t_ref[pl.ds(seq + r0, rb), :] = hb
        return acc + jnp.sum(jnp.abs(hf), axis=0, keepdims=True) + jnp.sum(jnp.abs(hb), axis=0, keepdims=True)

    norm = lax.fori_loop(0, seq // rb, build_filter, jnp.zeros((1, ct), F32))
    stage1(gk_ref)
    stage2("filter")

    t_ref[0:seq, :] = z_ref[0].astype(F32)
    t_ref[seq:2 * seq, :] = z_ref[1].astype(F32)
    stage1(gc_ref)
    stage2("data")
    stage_out()

    inv_norm = 1.0 / norm
    for bi in range(2):
        def finish(r, carry, bi=bi):
            r0 = pl.multiple_of(r * rb, rb)
            zb = z_ref[bi, pl.ds(r0, rb), :].astype(F32)
            y = t_ref[pl.ds(bi * seq + r0, rb), :] * inv_norm + zb * bias_ref[...]
            o_ref[bi, pl.ds(r0, rb), :] = (x0_ref[bi, pl.ds(r0, rb), :].astype(F32) * y).astype(o_ref.dtype)
            return carry
        lax.fori_loop(0, seq // rb, finish, 0)


def hyena_long_conv(z, x0, hid, w3, bias, ct=128):
    b, l, c = z.shape
    assert b == 2 and 2 * l == FFT_N
    gc, gk, f2, f2i, hm = _dft_constants()
    lo, hi = math.log(1e-2) / 1.5, math.log(1e-2) / 0.3
    deltas = jnp.asarray(np.abs(np.linspace(lo, hi, c, dtype=np.float32)).reshape(1, c))
    zeros = jnp.zeros((HYENA_HID, c), F32)
    w3f = jnp.concatenate([w3[:, :c], zeros], axis=0)
    w3b = jnp.concatenate([zeros, w3[:, c:]], axis=0)
    nct = c // ct
    full = lambda a: pl.BlockSpec(a.shape, lambda j: (0,) * a.ndim, pipeline_mode=pl.Buffered(1))
    return pl.pallas_call(
        functools.partial(_hyena_fft_kernel, seq=l),
        grid=(nct,),
        in_specs=[
            pl.BlockSpec((b, l, ct), lambda j: (0, 0, j)),
            pl.BlockSpec((b, l, ct), lambda j: (0, 0, j)),
            full(hid),
            pl.BlockSpec((2 * HYENA_HID, ct), lambda j: (0, j)),
            pl.BlockSpec((2 * HYENA_HID, ct), lambda j: (0, j)),
            pl.BlockSpec((1, ct), lambda j: (0, j)),
            pl.BlockSpec((1, ct), lambda j: (0, j)),
            full(gc), full(gk), full(f2), full(f2i), full(hm),
        ],
        out_specs=pl.BlockSpec((b, l, ct), lambda j: (0, 0, j)),
        out_shape=jax.ShapeDtypeStruct((b, l, c), BF16),
        scratch_shapes=[
            pltpu.VMEM((2 * l, ct), F32),
            pltpu.VMEM((FFT_N2 * 2 * FFT_N1, ct), F32),
            pltpu.VMEM((FFT_N1 * 2 * FFT_N2, ct), BF16),
            pltpu.VMEM((FFT_N1 * 2 * FFT_N2, ct), F32),
        ],
        compiler_params=_params("parallel"),
        name="hyena_long_conv",
    )(z, x0, hid, w3f, w3b, deltas, bias.reshape(1, c), gc, gk, f2, f2i, hm)


SEL_ROWS = 8
SEL_BLK = 512


def _select_kernel(aff_ref, tri_ref, idx_ref, gate_ref, pos_ref, *, cap):
    x = aff_ref[...]
    rows, l = x.shape
    bits = pltpu.bitcast(x, I32)

    def search(i, prefix):
        cand = prefix | lax.shift_left(jnp.int32(1), 30 - i)
        cnt = jnp.sum((bits >= cand).astype(I32), axis=1, keepdims=True)
        return jnp.where(cnt >= cap, cand, prefix)

    tau = lax.fori_loop(0, 31, search, jnp.zeros((rows, 1), I32))
    gt = bits > tau
    eq = bits == tau
    need = (cap - jnp.sum(gt.astype(I32), axis=1, keepdims=True)).astype(F32)

    def prefix_count(mask):
        run = jnp.zeros((rows, 1), F32)
        parts = []
        for blk in range(l // SEL_BLK):
            m = mask[:, blk * SEL_BLK:(blk + 1) * SEL_BLK].astype(BF16)
            c = jnp.dot(m, tri_ref[...], preferred_element_type=F32) + run
            parts.append(c)
            run = c[:, SEL_BLK - 1:SEL_BLK]
        return jnp.concatenate(parts, axis=1)

    eq_rank = prefix_count(jnp.where(eq, 1.0, 0.0))
    sel = jnp.logical_or(gt, jnp.logical_and(eq, eq_rank <= need))
    pos_ref[...] = jnp.where(sel, prefix_count(jnp.where(sel, 1.0, 0.0)), 0.0)

    slot = (lax.broadcasted_iota(I32, (cap, SEL_BLK), 0) + 1).astype(F32)
    tok = lax.broadcasted_iota(I32, (1, SEL_BLK), 1).astype(F32)

    def compact(r, carry):
        acc_i = jnp.zeros((cap, SEL_BLK), F32)
        acc_g = jnp.zeros((cap, SEL_BLK), F32)
        for blk in range(l // SEL_BLK):
            sl = pl.ds(blk * SEL_BLK, SEL_BLK)
            hit = pos_ref[pl.ds(r, 1), sl] == slot
            acc_i = acc_i + jnp.where(hit, tok + float(blk * SEL_BLK), 0.0)
            acc_g = acc_g + jnp.where(hit, aff_ref[pl.ds(r, 1), sl], 0.0)
        idx_ref[r] = jnp.sum(acc_i, axis=1, keepdims=True).astype(I32)
        gate_ref[r] = jnp.sum(acc_g, axis=1, keepdims=True)
        return carry
    lax.fori_loop(0, rows, compact, 0)


def expert_select(aff_t, cap):
    b, e, l = aff_t.shape
    r = b * e
    tri = jnp.asarray(np.triu(np.ones((SEL_BLK, SEL_BLK), np.float32))).astype(BF16)
    idx, gate = pl.pallas_call(
        functools.partial(_select_kernel, cap=cap),
        grid=(r // SEL_ROWS,),
        in_specs=[pl.BlockSpec((SEL_ROWS, l), lambda i: (i, 0)), pl.BlockSpec((SEL_BLK, SEL_BLK), lambda i: (0, 0))],
        out_specs=[pl.BlockSpec((SEL_ROWS, cap, 1), lambda i: (i, 0, 0))] * 2,
        out_shape=[jax.ShapeDtypeStruct((r, cap, 1), I32), jax.ShapeDtypeStruct((r, cap, 1), F32)],
        scratch_shapes=[pltpu.VMEM((SEL_ROWS, l), F32)],
        compiler_params=_params("parallel"),
        name="expert_select",
    )(aff_t.reshape(r, l), tri)
    return idx.reshape(r, cap), gate.reshape(r, cap)


def _row_copy(src_ref, src_row, dst_ref, dst_row, sem):
    return pltpu.make_async_copy(src_ref.at[pl.ds(src_row, 1)], dst_ref.at[pl.ds(dst_row, 1)], sem)


def _moe_ffn_kernel(idx_ref, x_hbm, wg_ref, wu_ref, wd_ref, o_ref, stage_ref, xb_ref, act_ref, sem,
                    *, nb, cap, seq, n_f):
    e = pl.program_id(0)
    s = pl.program_id(1)
    ne = pl.num_programs(0)
    tf = wg_ref.shape[2]
    slot = e % 2
    has_next = e + 1 < ne

    def issue(expert, bi):
        base = (bi * ne + expert) * cap

        def body(j, carry):
            _row_copy(x_hbm, bi * seq + idx_ref[base + j], stage_ref, j, sem).start()
            return carry
        lax.fori_loop(0, cap, body, 0, unroll=8)

    def land(dst_slot, bi):
        def body(j, carry):
            _row_copy(x_hbm, 0, stage_ref, j, sem).wait()
            return carry
        lax.fori_loop(0, cap, body, 0, unroll=8)
        xb_ref[dst_slot, bi * cap:(bi + 1) * cap, :] = stage_ref[...].astype(BF16)

    @pl.when(jnp.logical_and(e == 0, s == 0))
    def _():
        for bi in range(nb):
            issue(0, bi)
            land(0, bi)

    steps_per_batch = n_f // nb
    rows_per_step = cap // steps_per_batch
    nxt = jnp.where(has_next, e + 1, e)
    for bi in range(1, nb):
        @pl.when(s == bi * steps_per_batch)
        def _(bi=bi):
            land(1 - slot, bi - 1)

    @pl.when(s == n_f + 1)
    def _():
        land(1 - slot, nb - 1)

    @pl.when(s < n_f)
    def _():
        bi = s // steps_per_batch
        row0 = (s - bi * steps_per_batch) * rows_per_step
        base = (bi * ne + nxt) * cap + row0
        for j in range(rows_per_step):
            _row_copy(x_hbm, bi * seq + idx_ref[base + j], stage_ref, row0 + j, sem).start()
        x = xb_ref[slot]
        g = jnp.dot(x, wg_ref[0].astype(BF16), preferred_element_type=F32)
        u = jnp.dot(x, wu_ref[0].astype(BF16), preferred_element_type=F32)
        act = (jax.nn.silu(g) * u).astype(BF16)
        for f in range(n_f):
            @pl.when(s == f)
            def _(f=f):
                act_ref[:, f * tf:(f + 1) * tf] = act

    @pl.when(s >= n_f)
    def _():
        o_ref[0] = jnp.dot(act_ref[...], wd_ref[0].astype(BF16), preferred_element_type=F32).astype(o_ref.dtype)


def moe_ffn(idx, x_rows, w_gate, w_up, w_down, *, layer, nb, seq, tf=256, tn=512):
    _, ne, d, ff = w_gate.shape
    assert nb == 2
    cap = idx.shape[0] // (nb * ne)
    n_f = ff // tf
    n_d = d // tn
    assert n_d >= 2 and n_f % nb == 0 and cap % (n_f // nb) == 0
    grid_spec = pltpu.PrefetchScalarGridSpec(
        num_scalar_prefetch=1,
        grid=(ne, n_f + n_d),
        in_specs=[
            pl.BlockSpec(memory_space=pl.ANY),
            pl.BlockSpec((None, 1, d, tf), lambda e, s, idx: (layer, e, 0, jnp.minimum(s, n_f - 1))),
            pl.BlockSpec((None, 1, d, tf), lambda e, s, idx: (layer, e, 0, jnp.minimum(s, n_f - 1))),
            pl.BlockSpec((None, 1, ff, tn), lambda e, s, idx: (layer, e, 0, jnp.maximum(s - n_f, 0))),
        ],
        out_specs=pl.BlockSpec((1, nb * cap, tn), lambda e, s, idx: (e, 0, jnp.maximum(s - n_f, 0))),
        scratch_shapes=[
            pltpu.VMEM((cap, d), F32),
            pltpu.VMEM((2, nb * cap, d), BF16),
            pltpu.VMEM((nb * cap, ff), BF16),
            pltpu.SemaphoreType.DMA(()),
        ],
    )
    return pl.pallas_call(
        functools.partial(_moe_ffn_kernel, nb=nb, cap=cap, seq=seq, n_f=n_f),
        grid_spec=grid_spec,
        out_shape=jax.ShapeDtypeStruct((ne, nb * cap, d), BF16),
        compiler_params=_params("arbitrary", "arbitrary"),
        name="moe_ffn",
    )(idx, x_rows, w_gate, w_up, w_down)


def _moe_scatter_kernel(idx_ref, h_in, ys_ref, gate_ref, mod_ref, h_out, rows_ref, sem_in, sem_out, *, nb, cap, seq):
    del h_in
    e = pl.program_id(0)
    ne = pl.num_programs(0)
    sub = 16

    def for_rows(fn):
        def body(j, carry):
            fn(j)
            return carry
        lax.fori_loop(0, cap, body, 0, unroll=8)

    def token_row(bi, j):
        return bi * seq + idx_ref[(bi * ne + e) * cap + j]

    for bi in range(nb):
        for_rows(lambda j, bi=bi: _row_copy(h_out, token_row(bi, j), rows_ref, bi * cap + j, sem_in.at[bi]).start())
    for bi in range(nb):
        for_rows(lambda j, bi=bi: _row_copy(h_out, 0, rows_ref, bi * cap + j, sem_in.at[bi]).wait())

        def update(i, carry, bi=bi):
            r0 = pl.multiple_of(bi * cap + i * sub, sub)
            w = gate_ref[0, pl.ds(r0, sub), :] * mod_ref[bi:bi + 1, :]
            rows_ref[pl.ds(r0, sub), :] = rows_ref[pl.ds(r0, sub), :] + w * ys_ref[0, pl.ds(r0, sub), :].astype(F32)
            for k in range(sub):
                j = i * sub + k
                _row_copy(rows_ref, bi * cap + j, h_out, token_row(bi, j), sem_out.at[bi]).start()
            return carry
        lax.fori_loop(0, cap // sub, update, 0, unroll=2)
    for bi in range(nb):
        for_rows(lambda j, bi=bi: _row_copy(rows_ref, bi * cap + j, h_out, 0, sem_out.at[bi]).wait())


def moe_scatter(idx, h_rows, ys, gates, mod, *, nb, seq):
    ne, n, d = ys.shape
    cap = n // nb
    grid_spec = pltpu.PrefetchScalarGridSpec(
        num_scalar_prefetch=1,
        grid=(ne,),
        in_specs=[
            pl.BlockSpec(memory_space=pl.ANY),
            pl.BlockSpec((1, n, d), lambda e, idx: (e, 0, 0)),
            pl.BlockSpec((1, n, 1), lambda e, idx: (e, 0, 0)),
            pl.BlockSpec((nb, d), lambda e, idx: (0, 0)),
        ],
        out_specs=pl.BlockSpec(memory_space=pl.ANY),
        scratch_shapes=[pltpu.VMEM((n, d), F32), pltpu.SemaphoreType.DMA((nb,)), pltpu.SemaphoreType.DMA((nb,))],
    )
    return pl.pallas_call(
        functools.partial(_moe_scatter_kernel, nb=nb, cap=cap, seq=seq),
        grid_spec=grid_spec,
        out_shape=jax.ShapeDtypeStruct(h_rows.shape, F32),
        input_output_aliases={1: 0},
        compiler_params=_params("arbitrary"),
        name="moe_scatter",
    )(idx, h_rows, ys, gates, mod)


def expert_choice_moe_residual(h, norm_g, shift, scale, gate_mod, router_w, w_gate, w_up, w_down, layer):
    b, l, d = h.shape
    ne = router_w.shape[1]
    cap = 2 * l // ne
    x, aff_t = norm_modulate_router(h, norm_g, shift, scale, router_w)
    idx, gates = expert_select(aff_t, cap)
    idx_flat = idx.reshape(-1)
    ys = moe_ffn(idx_flat, x.reshape(b * l, d), w_gate, w_up, w_down, layer=layer, nb=b, seq=l)
    gates_e = jnp.transpose(gates.reshape(b, ne, cap), (1, 0, 2)).reshape(ne, b * cap, 1)
    out = moe_scatter(idx_flat, h.reshape(b * l, d), ys, gates_e, gate_mod, nb=b, seq=l)
    return out.reshape(b, l, d)


def kernel(x, c, ctx, c_ctx, ada_w, ada_b, norm_mix_g, norm_ffn_g, attn_sgu_w_in, attn_sink, sgu_norm_g, sgu_w_s, sgu_b_s, attn_sgu_w_out, hyena_w_in, hyena_conv_w, hyena_conv_b, hyena_filt_w1, hyena_filt_b1, hyena_filt_w2, hyena_filt_b2, hyena_filt_freq, hyena_filt_w3, hyena_bias, hyena_w_out, router_w, expert_w_gate, expert_w_up, expert_w_down, final_norm_g):
    b, l, d = x.shape
    lc = ctx.shape[1]
    c_rows = jnp.concatenate([c, c_ctx[None, :], jnp.zeros((8 - b - 1, d), F32)], axis=0)
    mods = ada_modulation(c_rows, ada_w, ada_b)

    def mod(layer, rows, k):
        return mods[layer, rows, k * d:(k + 1) * d]

    lat = slice(0, b)
    ctx_rows = jnp.full((b,), b, I32)

    a_lat = norm_modulate(x, norm_mix_g[0], mod(0, lat, 0), mod(0, lat, 1), BF16)
    a_ctx = norm_modulate(ctx, norm_mix_g[0], mod(0, ctx_rows, 0), mod(0, ctx_rows, 1), BF16)
    w_in = attn_sgu_w_in[0]
    qkvuz = project(a_lat.reshape(b * l, d), w_in).reshape(b, l, -1)
    ctx_kv = project(a_ctx.reshape(b * lc, d), w_in, n_start=ATTN_WIDTH, n_cols=2 * KV_WIDTH).reshape(b, lc, -1)
    qk = rope_qk(qkvuz)
    o = windowed_attention(qk, qkvuz, ctx_kv, attn_sink[0])
    s = spatial_gating(qkvuz, sgu_norm_g[0], sgu_w_s[0], sgu_b_s[0])
    mixed = jnp.concatenate([o, s], axis=-1).reshape(b * l, -1)
    h = project_residual(mixed, attn_sgu_w_out[0], x.reshape(b * l, d), mod(0, lat, 2), rows_per_batch=l).reshape(b, l, d)
    h = expert_choice_moe_residual(h, norm_ffn_g[0], mod(0, lat, 3), mod(0, lat, 4), mod(0, lat, 5),
                                   router_w[0], expert_w_gate, expert_w_up, expert_w_down, 0)

    a_lat = norm_modulate(h, norm_mix_g[1], mod(1, lat, 0), mod(1, lat, 1), BF16)
    u3 = project(a_lat.reshape(b * l, d), hyena_w_in[0]).reshape(b, l, -1)
    x0, z = hyena_pre(u3, hyena_conv_w[0], hyena_conv_b[0])
    hid = filter_mlp(l, hyena_filt_w1[0], hyena_filt_b1[0], hyena_filt_w2[0], hyena_filt_b2[0], hyena_filt_freq[0])
    xy = hyena_long_conv(z, x0, hid, hyena_filt_w3[0], hyena_bias[0])
    h = project_residual(xy.reshape(b * l, d), hyena_w_out[0], h.reshape(b * l, d), mod(1, lat, 2), rows_per_batch=l).reshape(b, l, d)
    h = expert_choice_moe_residual(h, norm_ffn_g[1], mod(1, lat, 3), mod(1, lat, 4), mod(1, lat, 5),
                                   router_w[1], expert_w_gate, expert_w_up, expert_w_down, 1)
    return final_norm(h, final_norm_g)
```

```python
import functools
import math

import numpy as np
import jax
import jax.numpy as jnp
from jax import lax
from jax.experimental import pallas as pl
from jax.experimental.pallas import tpu as pltpu

F32 = jnp.float32
BF16 = jnp.bfloat16
I32 = jnp.int32
HIGHEST = lax.Precision.HIGHEST

D_MODEL = 4096
GRID_W = 64
NORM_EPS = 1e-6
N_MOD = 6
NEG_INF = -1e30
HEAD_DIM = 128
ATTN_HEADS = 16
ATTN_KV_HEADS = 4
GQ = 4
ATTN_BLOCK = 128
ROPE_THETA = 10000.0
ATTN_WIDTH = 2048
KV_WIDTH = 512
SGU_WIDTH = 2048
SGU_CHUNK = 128
ROPE_WIDTH = ATTN_WIDTH + KV_WIDTH
HYENA_BANDS = 16
HYENA_EMB = 33
HYENA_HID = 64
N_EXPERTS = 16
EXPERT_FF = 1024

LANES = 128
VMEM_LIMIT_BYTES = 56 * 1024 * 1024

FFT_N1 = 64
FFT_N2 = 128
FFT_N = FFT_N1 * FFT_N2
FFT_HALF = FFT_N1 // 2


def _params(*sem):
    return pltpu.CompilerParams(dimension_semantics=sem, vmem_limit_bytes=VMEM_LIMIT_BYTES)


def _ada_kernel(c_ref, w_ref, b_ref, o_ref):
    s = jax.nn.silu(c_ref[...]).astype(BF16)
    o_ref[0] = jnp.dot(s, w_ref[0].astype(BF16), preferred_element_type=F32) + b_ref[0]


def ada_modulation(c_rows, ada_w, ada_b, tn=512):
    depth, d, n = ada_w.shape
    rows = c_rows.shape[0]
    return pl.pallas_call(
        _ada_kernel,
        grid=(depth, n // tn),
        in_specs=[
            pl.BlockSpec((rows, d), lambda l, j: (0, 0)),
            pl.BlockSpec((1, d, tn), lambda l, j: (l, 0, j)),
            pl.BlockSpec((1, 1, tn), lambda l, j: (l, 0, j)),
        ],
        out_specs=pl.BlockSpec((1, rows, tn), lambda l, j: (l, 0, j)),
        out_shape=jax.ShapeDtypeStruct((depth, rows, n), F32),
        compiler_params=_params("parallel", "parallel"),
        name="ada_modulation",
    )(c_rows, ada_w, ada_b.reshape(depth, 1, n))


def _rms(x, g):
    return x * lax.rsqrt(jnp.mean(x * x, axis=-1, keepdims=True) + NORM_EPS) * g


def _norm_mod_kernel(h_ref, g_ref, sh_ref, sc_ref, o_ref):
    y = _rms(h_ref[0], g_ref[...])
    o_ref[0] = (y * (1.0 + sc_ref[0]) + sh_ref[0]).astype(o_ref.dtype)


def norm_modulate(h, g, shift, scale, out_dtype, tl=256):
    b, l, d = h.shape
    tl = min(tl, l)
    return pl.pallas_call(
        _norm_mod_kernel,
        grid=(b, l // tl),
        in_specs=[
            pl.BlockSpec((1, tl, d), lambda i, j: (i, j, 0)),
            pl.BlockSpec((1, d), lambda i, j: (0, 0)),
            pl.BlockSpec((1, 1, d), lambda i, j: (i, 0, 0)),
            pl.BlockSpec((1, 1, d), lambda i, j: (i, 0, 0)),
        ],
        out_specs=pl.BlockSpec((1, tl, d), lambda i, j: (i, j, 0)),
        out_shape=jax.ShapeDtypeStruct((b, l, d), out_dtype),
        compiler_params=_params("parallel", "parallel"),
        name="norm_modulate",
    )(h, g.reshape(1, d), shift.reshape(b, 1, d), scale.reshape(b, 1, d))


def _norm_mod_router_kernel(h_ref, g_ref, sh_ref, sc_ref, rw_ref, o_ref, aff_ref):
    y = _rms(h_ref[0], g_ref[...])
    a = y * (1.0 + sc_ref[0]) + sh_ref[0]
    o_ref[0] = a
    lt = lax.dot_general(rw_ref[...], a, (((1,), (1,)), ((), ())), precision=HIGHEST, preferred_element_type=F32)
    m = jnp.max(lt, axis=0, keepdims=True)
    p = jnp.exp(lt - m)
    aff_ref[0] = p / jnp.sum(p, axis=0, keepdims=True)


def norm_modulate_router(h, g, shift, scale, router_w, tl=256):
    b, l, d = h.shape
    e = router_w.shape[1]
    return pl.pallas_call(
        _norm_mod_router_kernel,
        grid=(b, l // tl),
        in_specs=[
            pl.BlockSpec((1, tl, d), lambda i, j: (i, j, 0)),
            pl.BlockSpec((1, d), lambda i, j: (0, 0)),
            pl.BlockSpec((1, 1, d), lambda i, j: (i, 0, 0)),
            pl.BlockSpec((1, 1, d), lambda i, j: (i, 0, 0)),
            pl.BlockSpec((e, d), lambda i, j: (0, 0)),
        ],
        out_specs=[
            pl.BlockSpec((1, tl, d), lambda i, j: (i, j, 0)),
            pl.BlockSpec((1, e, tl), lambda i, j: (i, 0, j)),
        ],
        out_shape=[jax.ShapeDtypeStruct((b, l, d), F32), jax.ShapeDtypeStruct((b, e, l), F32)],
        compiler_params=_params("parallel", "parallel"),
        name="norm_modulate_router",
    )(h, g.reshape(1, d), shift.reshape(b, 1, d), scale.reshape(b, 1, d), router_w.T)


def _final_norm_kernel(h_ref, g_ref, o_ref):
    o_ref[0] = _rms(h_ref[0], g_ref[...])


def final_norm(h, g, tl=256):
    b, l, d = h.shape
    return pl.pallas_call(
        _final_norm_kernel,
        grid=(b, l // tl),
        in_specs=[pl.BlockSpec((1, tl, d), lambda i, j: (i, j, 0)), pl.BlockSpec((1, d), lambda i, j: (0, 0))],
        out_specs=pl.BlockSpec((1, tl, d), lambda i, j: (i, j, 0)),
        out_shape=jax.ShapeDtypeStruct((b, l, d), F32),
        compiler_params=_params("parallel", "parallel"),
        name="final_norm",
    )(h, g.reshape(1, d))


def _mm_kernel(a_ref, w_ref, o_ref, wb_ref):
    @pl.when(pl.program_id(1) == 0)
    def _():
        wb_ref[...] = w_ref[...].astype(BF16)

    o_ref[...] = jnp.dot(a_ref[...], wb_ref[...], preferred_element_type=F32).astype(o_ref.dtype)


def _mm_res_kernel(*refs, n_parts):
    a_refs = refs[:n_parts]
    w_ref, res_ref, gate_ref, o_ref, wb_ref = refs[n_parts:]

    @pl.when(pl.program_id(1) == 0)
    def _():
        wb_ref[...] = w_ref[...].astype(BF16)

    acc = None
    k0 = 0
    for a_ref in a_refs:
        kp = a_ref.shape[1]
        part = jnp.dot(a_ref[...], wb_ref[k0:k0 + kp, :], preferred_element_type=F32)
        acc = part if acc is None else acc + part
        k0 += kp
    o_ref[...] = res_ref[...] + gate_ref[0] * acc


def project(a, w, *, n_start=0, n_cols=None, out_dtype=BF16, tm=1024, tn=512):
    m, k = a.shape
    n_cols = w.shape[1] - n_start if n_cols is None else n_cols
    tm = min(tm, m)
    off = n_start // tn
    return pl.pallas_call(
        _mm_kernel,
        grid=(n_cols // tn, m // tm),
        in_specs=[
            pl.BlockSpec((tm, k), lambda j, i: (i, 0)),
            pl.BlockSpec((k, tn), lambda j, i: (0, off + j)),
        ],
        out_specs=pl.BlockSpec((tm, tn), lambda j, i: (i, j)),
        out_shape=jax.ShapeDtypeStruct((m, n_cols), out_dtype),
        scratch_shapes=[pltpu.VMEM((k, tn), BF16)],
        compiler_params=_params("arbitrary", "arbitrary"),
        name="project",
    )(a, w)


def project_residual(parts, w, res, gate, *, rows_per_batch, tm=1024, tn=512):
    m = parts[0].shape[0]
    k, n = w.shape
    assert sum(p.shape[1] for p in parts) == k
    nb = gate.shape[0]
    assert rows_per_batch % tm == 0, "a row tile must not straddle two batch entries (one gate row per tile)"
    tiles_per_batch = rows_per_batch // tm
    return pl.pallas_call(
        functools.partial(_mm_res_kernel, n_parts=len(parts)),
        grid=(n // tn, m // tm),
        in_specs=[pl.BlockSpec((tm, p.shape[1]), lambda j, i: (i, 0)) for p in parts] + [
            pl.BlockSpec((k, tn), lambda j, i: (0, j)),
            pl.BlockSpec((tm, tn), lambda j, i: (i, j)),
            pl.BlockSpec((1, 1, tn), lambda j, i: (i // tiles_per_batch, 0, j)),
        ],
        out_specs=pl.BlockSpec((tm, tn), lambda j, i: (i, j)),
        out_shape=jax.ShapeDtypeStruct((m, n), F32),
        scratch_shapes=[pltpu.VMEM((k, tn), BF16)],
        compiler_params=_params("arbitrary", "arbitrary"),
        name="project_residual",
    )(*parts, w, res, gate.reshape(nb, 1, n))


def _rope_tables(l):
    rows = l // GRID_W
    row = np.repeat(np.arange(rows), GRID_W).astype(np.float32)
    col = np.tile(np.arange(GRID_W), rows).astype(np.float32)
    half = HEAD_DIM // 2
    inv = (ROPE_THETA ** (-np.arange(0, half, 2, dtype=np.float32) / half)).astype(np.float32)
    ang_r = row[:, None] * inv[None, :]
    ang_c = col[:, None] * inv[None, :]
    ang = np.concatenate([ang_r, ang_r, ang_c, ang_c], axis=1)
    sign = np.tile(np.concatenate([-np.ones(half // 2), np.ones(half // 2)]), 2).astype(np.float32)
    return jnp.asarray(np.cos(ang), F32), jnp.asarray(np.sin(ang) * sign[None, :], F32)


def _rope_kernel(x_ref, cos_ref, sin_ref, o_ref):
    cos = cos_ref[...]
    sin = sin_ref[...]
    quarter = HEAD_DIM // 4
    lane = lax.broadcasted_iota(I32, cos.shape, 1)
    first = (lane % (2 * quarter)) < quarter
    for j in range(x_ref.shape[2] // HEAD_DIM):
        sl = slice(j * HEAD_DIM, (j + 1) * HEAD_DIM)
        x = x_ref[0, :, sl].astype(F32)
        partner = jnp.where(first, pltpu.roll(x, HEAD_DIM - quarter, 1), pltpu.roll(x, quarter, 1))
        o_ref[0, :, sl] = (x * cos + partner * sin).astype(o_ref.dtype)


def rope_qk(qkvuz, tl=256):
    b, l, _ = qkvuz.shape
    cos, sin = _rope_tables(l)
    return pl.pallas_call(
        _rope_kernel,
        grid=(b, l // tl),
        in_specs=[
            pl.BlockSpec((1, tl, ROPE_WIDTH), lambda i, j: (i, j, 0)),
            pl.BlockSpec((tl, HEAD_DIM), lambda i, j: (j, 0)),
            pl.BlockSpec((tl, HEAD_DIM), lambda i, j: (j, 0)),
        ],
        out_specs=pl.BlockSpec((1, tl, ROPE_WIDTH), lambda i, j: (i, j, 0)),
        out_shape=jax.ShapeDtypeStruct((b, l, ROPE_WIDTH), BF16),
        compiler_params=_params("parallel", "parallel"),
        name="rope_qk",
    )(qkvuz, cos, sin)


ATTN_QB = 4


def _attn_kernel(sink_ref, q_ref, kp_ref, k0_ref, kn_ref, vp_ref, v0_ref, vn_ref, kx_ref, vx_ref, o_ref):
    h = pl.program_id(1)
    step = pl.program_id(2)
    n_steps = pl.num_programs(2)
    blk = ATTN_BLOCK
    scale = HEAD_DIM ** -0.5
    rows = lambda ref, j: ref[0, j * blk:(j + 1) * blk, :]
    keys = [kp_ref[0]] + [rows(k0_ref, j) for j in range(ATTN_QB)] + [kn_ref[0]]
    vals = [vp_ref[0]] + [rows(v0_ref, j) for j in range(ATTN_QB)] + [vn_ref[0]]
    qi = lax.broadcasted_iota(I32, (GQ * blk, blk), 0) % blk
    ki = lax.broadcasted_iota(I32, (GQ * blk, blk), 1)
    sink = jnp.concatenate([jnp.full((blk, 1), sink_ref[h * GQ + g], F32) for g in range(GQ)], axis=0)

    def fold(parts, op):
        tiles = [p[:, t * blk:(t + 1) * blk] for p in parts for t in range(p.shape[1] // blk)]
        return functools.reduce(op, tiles)

    def pv(p, v):
        return jnp.dot(p.astype(BF16), v, preferred_element_type=F32)

    for j in range(ATTN_QB):
        q = jnp.concatenate([q_ref[0, j * blk:(j + 1) * blk, g * HEAD_DIM:(g + 1) * HEAD_DIM] for g in range(GQ)], axis=0)

        def scores(k):
            return lax.dot_general(q, k, (((1,), (1,)), ((), ())), preferred_element_type=F32) * scale

        sp, s0, sn, sx = scores(keys[j]), scores(keys[j + 1]), scores(keys[j + 2]), scores(kx_ref[0])
        off_p = jnp.where(step > 0, 0, blk) if j == 0 else 0
        off_n = jnp.where(step < n_steps - 1, 0, blk) if j == ATTN_QB - 1 else 0
        sp = jnp.where(ki >= qi + off_p, sp, NEG_INF)
        sn = jnp.where(ki <= qi - off_n, sn, NEG_INF)
        m = jnp.maximum(jnp.max(fold([sp, s0, sn, sx], jnp.maximum), axis=-1, keepdims=True), sink)
        pp, p0, pn, px = jnp.exp(sp - m), jnp.exp(s0 - m), jnp.exp(sn - m), jnp.exp(sx - m)
        denom = jnp.sum(fold([pp, p0, pn, px], jnp.add), axis=-1, keepdims=True) + jnp.exp(sink - m)
        o = pv(pp, vals[j]) + pv(p0, vals[j + 1]) + pv(pn, vals[j + 2]) + pv(px, vx_ref[0])
        o = o / denom
        for g in range(GQ):
            o_ref[0, j * blk:(j + 1) * blk, g * HEAD_DIM:(g + 1) * HEAD_DIM] = o[g * blk:(g + 1) * blk].astype(o_ref.dtype)


def windowed_attention(qk, qkvuz, ctx_kv, sink):
    b, l, _ = qk.shape
    lc = ctx_kv.shape[1]
    nb = l // ATTN_BLOCK
    n_steps = nb // ATTN_QB
    kcol = ATTN_WIDTH // HEAD_DIM
    vcol = (ATTN_WIDTH + KV_WIDTH) // HEAD_DIM
    edge = (1, ATTN_BLOCK, HEAD_DIM)
    cur = (1, ATTN_QB * ATTN_BLOCK, HEAD_DIM)
    prev = lambda s: jnp.maximum(s * ATTN_QB - 1, 0)
    nxt = lambda s: jnp.minimum((s + 1) * ATTN_QB, nb - 1)
    return pl.pallas_call(
        _attn_kernel,
        grid=(b, ATTN_KV_HEADS, n_steps),
        in_specs=[
            pl.BlockSpec(memory_space=pltpu.SMEM),
            pl.BlockSpec((1, ATTN_QB * ATTN_BLOCK, GQ * HEAD_DIM), lambda i, h, s: (i, s, h)),
            pl.BlockSpec(edge, lambda i, h, s: (i, prev(s), kcol + h)),
            pl.BlockSpec(cur, lambda i, h, s: (i, s, kcol + h)),
            pl.BlockSpec(edge, lambda i, h, s: (i, nxt(s), kcol + h)),
            pl.BlockSpec(edge, lambda i, h, s: (i, prev(s), vcol + h)),
            pl.BlockSpec(cur, lambda i, h, s: (i, s, vcol + h)),
            pl.BlockSpec(edge, lambda i, h, s: (i, nxt(s), vcol + h)),
            pl.BlockSpec((1, lc, HEAD_DIM), lambda i, h, s: (i, 0, h)),
            pl.BlockSpec((1, lc, HEAD_DIM), lambda i, h, s: (i, 0, ATTN_KV_HEADS + h)),
        ],
        out_specs=pl.BlockSpec((1, ATTN_QB * ATTN_BLOCK, GQ * HEAD_DIM), lambda i, h, s: (i, s, h)),
        out_shape=jax.ShapeDtypeStruct((b, l, ATTN_WIDTH), BF16),
        compiler_params=_params("parallel", "parallel", "parallel"),
        name="windowed_attention",
    )(sink, qk, qk, qk, qk, qkvuz, qkvuz, qkvuz, ctx_kv, ctx_kv)


def _gelu(x):
    return 0.5 * x * (1.0 + lax.erf(x * (1.0 / math.sqrt(2.0))))


def _sgu_kernel(u_ref, z_ref, g_ref, ws_ref, bs_ref, o_ref):
    for gi in range(ws_ref.shape[0]):
        sl = slice(gi * LANES, (gi + 1) * LANES)
        z = _gelu(z_ref[0, :, sl].astype(F32))
        mu = jnp.mean(z, axis=-1, keepdims=True)
        zc = z - mu
        var = jnp.mean(zc * zc, axis=-1, keepdims=True)
        zn = zc * lax.rsqrt(var + NORM_EPS) * g_ref[:, sl]
        mixed = jnp.dot(ws_ref[gi].astype(BF16), zn.astype(BF16), preferred_element_type=F32) + bs_ref[:, sl]
        o_ref[0, :, sl] = (_gelu(u_ref[0, :, sl].astype(F32)) * mixed).astype(o_ref.dtype)


def spatial_gating(qkvuz, g, w_s, b_s, tw=1024):
    b, l, _ = qkvuz.shape
    groups = w_s.shape[0]
    ucol = (ATTN_WIDTH + 2 * KV_WIDTH) // tw
    zcol = (ATTN_WIDTH + 2 * KV_WIDTH + SGU_WIDTH) // tw
    gpt = tw // LANES
    bias = jnp.repeat(jnp.transpose(b_s), LANES, axis=1)
    return pl.pallas_call(
        _sgu_kernel,
        grid=(b, l // SGU_CHUNK, SGU_WIDTH // tw),
        in_specs=[
            pl.BlockSpec((1, SGU_CHUNK, tw), lambda i, n, j: (i, n, ucol + j)),
            pl.BlockSpec((1, SGU_CHUNK, tw), lambda i, n, j: (i, n, zcol + j)),
            pl.BlockSpec((1, tw), lambda i, n, j: (0, j)),
            pl.BlockSpec((gpt, SGU_CHUNK, SGU_CHUNK), lambda i, n, j: (j, 0, 0)),
            pl.BlockSpec((SGU_CHUNK, tw), lambda i, n, j: (0, j)),
        ],
        out_specs=pl.BlockSpec((1, SGU_CHUNK, tw), lambda i, n, j: (i, n, j)),
        out_shape=jax.ShapeDtypeStruct((b, l, SGU_WIDTH), BF16),
        compiler_params=_params("parallel", "parallel", "parallel"),
        name="spatial_gating",
    )(qkvuz, qkvuz, g.reshape(1, groups * LANES), w_s, bias)


HALO = 16


def _hyena_pre_kernel(*refs, tl):
    (p0, m0, n0, p1, m1, n1, p2, m2, n2, w0, w1, w2, b0, b1, b2, x0_ref, z_ref) = refs
    first = pl.program_id(1) == 0
    last = pl.program_id(1) == pl.num_programs(1) - 1
    rows = lax.broadcasted_iota(I32, m0.shape[1:], 0)

    def conv(p_ref, m_ref, n_ref, w_ref, b_ref):
        x = m_ref[0].astype(F32)
        prev_row = jnp.where(first, 0.0, p_ref[0, HALO - 1:HALO, :].astype(F32))
        next_row = jnp.where(last, 0.0, n_ref[0, 0:1, :].astype(F32))
        xm = jnp.where(rows == 0, prev_row, pltpu.roll(x, 1, 0))
        xp = jnp.where(rows == tl - 1, next_row, pltpu.roll(x, tl - 1, 0))
        return xm * w_ref[0:1, :] + x * w_ref[1:2, :] + xp * w_ref[2:3, :] + b_ref[...]

    x0_ref[0] = conv(p0, m0, n0, w0, b0).astype(x0_ref.dtype)
    x1 = conv(p1, m1, n1, w1, b1)
    v = conv(p2, m2, n2, w2, b2)
    z_ref[0] = (v * x1).astype(z_ref.dtype)


def hyena_pre(u3, conv_w, conv_b, tl=512, tc=512):
    b, l, w3 = u3.shape
    c = w3 // 3
    ct = c // tc
    hb = tl // HALO
    nhb = l // HALO
    in_specs = []
    for part in range(3):
        in_specs += [
            pl.BlockSpec((1, HALO, tc), lambda i, t, j, part=part: (i, jnp.maximum(t * hb - 1, 0), part * ct + j)),
            pl.BlockSpec((1, tl, tc), lambda i, t, j, part=part: (i, t, part * ct + j)),
            pl.BlockSpec((1, HALO, tc), lambda i, t, j, part=part: (i, jnp.minimum((t + 1) * hb, nhb - 1), part * ct + j)),
        ]
    for part in range(3):
        in_specs.append(pl.BlockSpec((3, tc), lambda i, t, j, part=part: (0, part * ct + j)))
    for part in range(3):
        in_specs.append(pl.BlockSpec((1, tc), lambda i, t, j, part=part: (0, part * ct + j)))
    out_spec = pl.BlockSpec((1, tl, tc), lambda i, t, j: (i, t, j))
    return pl.pallas_call(
        functools.partial(_hyena_pre_kernel, tl=tl),
        grid=(b, l // tl, ct),
        in_specs=in_specs,
        out_specs=[out_spec, out_spec],
        out_shape=[jax.ShapeDtypeStruct((b, l, c), BF16), jax.ShapeDtypeStruct((b, l, c), BF16)],
        compiler_params=_params("parallel", "parallel", "parallel"),
        name="hyena_pre",
    )(*([u3] * 9), *([conv_w] * 3), *([conv_b.reshape(1, w3)] * 3))


def _filter_features(l):
    pos = np.arange(l, dtype=np.float32)
    t01 = pos / np.float32(max(l - 1, 1))
    bands = np.linspace(1e-4, HYENA_BANDS - 1, HYENA_BANDS, dtype=np.float32)
    ang = np.float32(2.0 * math.pi / l) * pos[:, None] * bands[None, :]
    feats = np.concatenate([t01[:, None], np.cos(ang), -np.sin(ang)], axis=-1).astype(np.float32)
    feats = np.pad(feats, ((0, 0), (0, HYENA_HID - HYENA_EMB)))
    rev = np.concatenate([np.zeros((1, HYENA_HID), np.float32), feats[:0:-1]], axis=0)
    return np.concatenate([feats, rev], axis=0)


def _filter_mlp_kernel(f_ref, w1_ref, b1_ref, w2_ref, b2_ref, fr_ref, o_ref):
    h = jnp.dot(f_ref[...], w1_ref[...], precision=HIGHEST, preferred_element_type=F32) + b1_ref[...]
    h = jnp.sin(fr_ref[0:1, :] * h)
    h = jnp.dot(h, w2_ref[...], precision=HIGHEST, preferred_element_type=F32) + b2_ref[...]
    o_ref[...] = jnp.sin(fr_ref[1:2, :] * h)


def filter_mlp(l, w1, b1, w2, b2, freq):
    feats = jnp.asarray(_filter_features(l))
    w1p = jnp.pad(w1, ((0, HYENA_HID - HYENA_EMB), (0, 0)))
    hid = pl.pallas_call(
        _filter_mlp_kernel,
        out_shape=jax.ShapeDtypeStruct((2 * l, HYENA_HID), F32),
        compiler_params=pltpu.CompilerParams(vmem_limit_bytes=VMEM_LIMIT_BYTES),
        name="filter_mlp",
    )(feats, w1p, b1.reshape(1, -1), w2, b2.reshape(1, -1), freq)
    return jnp.concatenate([hid[:l], hid[l:]], axis=1)


def _dft_constants():
    n1 = np.arange(FFT_N1, dtype=np.float64)
    n2 = np.arange(FFT_N2, dtype=np.float64)
    k1 = n1
    two_pi = 2.0 * np.pi
    theta = two_pi * (k1[None, :, None] * n1[None, None, :] / FFT_N1 + n2[:, None, None] * k1[None, :, None] / FFT_N)
    gr, gi = np.cos(theta), -np.sin(theta)
    g_cplx = np.concatenate([np.concatenate([gr[:, :, :FFT_HALF], -gi[:, :, :FFT_HALF]], axis=2),
                             np.concatenate([gi[:, :, :FFT_HALF], gr[:, :, :FFT_HALF]], axis=2)], axis=1)
    g_real = np.concatenate([gr, gi], axis=1)
    ang2 = two_pi * np.outer(n2, n2) / FFT_N2
    cm, sm = np.cos(ang2), np.sin(ang2)
    f2 = np.block([[cm, sm], [-sm, cm]])
    f2i = np.block([[cm, -sm], [sm, cm]])
    phi = two_pi * (n1[None, :FFT_HALF, None] * k1[None, None, :] / FFT_N1 + n2[:, None, None] * k1[None, None, :] / FFT_N)
    cp, sp = np.cos(phi) / FFT_N, np.sin(phi) / FFT_N
    hm = np.concatenate([np.concatenate([cp, -sp], axis=2), np.concatenate([sp, cp], axis=2)], axis=1)

    def pair(m):
        p, r, c = m.shape[0] // 2, m.shape[1], m.shape[2]
        out = np.zeros((p, 2 * r, 2 * c), np.float64)
        out[:, :r, :c] = m[0::2]
        out[:, r:, c:] = m[1::2]
        return out

    as_bf16 = lambda a: jnp.asarray(a.astype(np.float32)).astype(BF16)
    return as_bf16(pair(g_cplx)), as_bf16(pair(g_real)), as_bf16(f2), as_bf16(f2i), as_bf16(pair(hm))


def _hyena_fft_kernel(z_ref, x0_ref, hid_ref, w3f_ref, w3b_ref, dl_ref, bias_ref, gc_ref, gk_ref, f2_ref, f2i_ref, hm_ref,
                      o_ref, t_ref, a_ref, kf_ref, b_ref, *, seq):
    two_n1 = 2 * FFT_N1
    two_n2 = 2 * FFT_N2
    ct = dl_ref.shape[1]

    def stage1(g_ref):
        def body(p, carry):
            n2 = 2 * p
            x = jnp.concatenate([t_ref[pl.ds(n2, FFT_N1, stride=FFT_N2), :],
                                 t_ref[pl.ds(n2 + 1, FFT_N1, stride=FFT_N2), :]], axis=0).astype(BF16)
            rows = pl.ds(pl.multiple_of(p * 2 * two_n1, 2 * two_n1), 2 * two_n1)
            a_ref[rows, :] = jnp.dot(g_ref[p], x, preferred_element_type=F32)
            return carry
        lax.fori_loop(0, FFT_N2 // 2, body, 0, unroll=16)

    def stage2(mode):
        def body(k1, carry):
            ar = a_ref[pl.ds(k1, FFT_N2, stride=two_n1), :]
            ai = a_ref[pl.ds(FFT_N1 + k1, FFT_N2, stride=two_n1), :]
            s = jnp.concatenate([ar, ai], axis=0).astype(BF16)
            x = jnp.dot(f2_ref[...], s, preferred_element_type=F32)
            rows = pl.ds(pl.multiple_of(k1 * two_n2, two_n2), two_n2)
            if mode == "filter":
                kf_ref[rows, :] = x.astype(kf_ref.dtype)
            else:
                kf = kf_ref[rows, :].astype(F32)
                xr, xi = x[:FFT_N2], x[FFT_N2:]
                kr, ki = kf[:FFT_N2], kf[FFT_N2:]
                y = jnp.concatenate([xr * kr - xi * ki, xr * ki + xi * kr], axis=0).astype(BF16)
                b_ref[rows, :] = jnp.dot(f2i_ref[...], y, preferred_element_type=F32)
            return carry
        lax.fori_loop(0, FFT_N1, body, 0, unroll=16)

    def stage_out():
        def body(p, carry):
            n2 = 2 * p
            s = jnp.concatenate([b_ref[pl.ds(n2, FFT_N1, stride=two_n2), :],
                                 b_ref[pl.ds(FFT_N2 + n2, FFT_N1, stride=two_n2), :],
                                 b_ref[pl.ds(n2 + 1, FFT_N1, stride=two_n2), :],
                                 b_ref[pl.ds(FFT_N2 + n2 + 1, FFT_N1, stride=two_n2), :]], axis=0).astype(BF16)
            y = jnp.dot(hm_ref[p], s, preferred_element_type=F32)
            for q in range(2):
                lo = q * 2 * FFT_HALF
                t_ref[pl.ds(n2 + q, FFT_HALF, stride=FFT_N2), :] = y[lo:lo + FFT_HALF]
                t_ref[pl.ds(seq + n2 + q, FFT_HALF, stride=FFT_N2), :] = y[lo + FFT_HALF:lo + 2 * FFT_HALF]
            return carry
        lax.fori_loop(0, FFT_N2 // 2, body, 0, unroll=16)

    rb = 512
    span = float(max(seq - 1, 1))

    def build_filter(r, acc):
        r0 = pl.multiple_of(r * rb, rb)
        t_idx = r0 + lax.broadcasted_iota(I32, (rb, ct), 0)
        tf = t_idx.astype(F32)
        hid = hid_ref[pl.ds(r0, rb), :].astype(BF16)
        hf = jnp.dot(hid, w3f_ref[...].astype(BF16), preferred_element_type=F32)
        hf = hf * jnp.exp(-(tf / span) * dl_ref[...])
        hb = jnp.dot(hid, w3b_ref[...].astype(BF16), preferred_element_type=F32)
        hb = jnp.where(t_idx == 0, 0.0, hb * jnp.exp(-((float(seq) - tf) / span) * dl_ref[...]))
        t_ref[pl.ds(r0, rb), :] = hf
        t_ref[pl.ds(seq + r0, rb), :] = hb
        return acc + jnp.sum(jnp.abs(hf), axis=0, keepdims=True) + jnp.sum(jnp.abs(hb), axis=0, keepdims=True)

    norm = lax.fori_loop(0, seq // rb, build_filter, jnp.zeros((1, ct), F32))
    stage1(gk_ref)
    stage2("filter")

    t_ref[0:seq, :] = z_ref[0].astype(F32)
    t_ref[seq:2 * seq, :] = z_ref[1].astype(F32)
    stage1(gc_ref)
    stage2("data")
    stage_out()

    inv_norm = 1.0 / norm
    for bi in range(2):
        def finish(r, carry, bi=bi):
            r0 = pl.multiple_of(r * rb, rb)
            zb = z_ref[bi, pl.ds(r0, rb), :].astype(F32)
            y = t_ref[pl.ds(bi * seq + r0, rb), :] * inv_norm + zb * bias_ref[...]
            o_ref[bi, pl.ds(r0, rb), :] = (x0_ref[bi, pl.ds(r0, rb), :].astype(F32) * y).astype(o_ref.dtype)
            return carry
        lax.fori_loop(0, seq // rb, finish, 0)


def hyena_long_conv(z, x0, hid, w3, bias, ct=128):
    b, l, c = z.shape
    assert b == 2 and 2 * l == FFT_N
    gc, gk, f2, f2i, hm = _dft_constants()
    lo, hi = math.log(1e-2) / 1.5, math.log(1e-2) / 0.3
    deltas = jnp.asarray(np.abs(np.linspace(lo, hi, c, dtype=np.float32)).reshape(1, c))
    zeros = jnp.zeros((HYENA_HID, c), F32)
    w3f = jnp.concatenate([w3[:, :c], zeros], axis=0)
    w3b = jnp.concatenate([zeros, w3[:, c:]], axis=0)
    nct = c // ct
    full = lambda a: pl.BlockSpec(a.shape, lambda j: (0,) * a.ndim, pipeline_mode=pl.Buffered(1))
    return pl.pallas_call(
        functools.partial(_hyena_fft_kernel, seq=l),
        grid=(nct,),
        in_specs=[
            pl.BlockSpec((b, l, ct), lambda j: (0, 0, j)),
            pl.BlockSpec((b, l, ct), lambda j: (0, 0, j)),
            full(hid),
            pl.BlockSpec((2 * HYENA_HID, ct), lambda j: (0, j)),
            pl.BlockSpec((2 * HYENA_HID, ct), lambda j: (0, j)),
            pl.BlockSpec((1, ct), lambda j: (0, j)),
            pl.BlockSpec((1, ct), lambda j: (0, j)),
            full(gc), full(gk), full(f2), full(f2i), full(hm),
        ],
        out_specs=pl.BlockSpec((b, l, ct), lambda j: (0, 0, j)),
        out_shape=jax.ShapeDtypeStruct((b, l, c), BF16),
        scratch_shapes=[
            pltpu.VMEM((2 * l, ct), F32),
            pltpu.VMEM((FFT_N2 * 2 * FFT_N1, ct), F32),
            pltpu.VMEM((FFT_N1 * 2 * FFT_N2, ct), BF16),
            pltpu.VMEM((FFT_N1 * 2 * FFT_N2, ct), F32),
        ],
        compiler_params=_params("parallel"),
        name="hyena_long_conv",
    )(z, x0, hid, w3f, w3b, deltas, bias.reshape(1, c), gc, gk, f2, f2i, hm)


SEL_ROWS = 8
SEL_BLK = 512


def _select_kernel(aff_ref, tri_ref, idx_ref, gate_ref, pos_ref, *, cap):
    x = aff_ref[...]
    rows, l = x.shape
    bits = pltpu.bitcast(x, I32)

    def search(i, prefix):
        cand = prefix | lax.shift_left(jnp.int32(1), 30 - i)
        cnt = jnp.sum((bits >= cand).astype(I32), axis=1, keepdims=True)
        return jnp.where(cnt >= cap, cand, prefix)

    tau = lax.fori_loop(0, 31, search, jnp.zeros((rows, 1), I32))
    gt = bits > tau
    eq = bits == tau
    need = (cap - jnp.sum(gt.astype(I32), axis=1, keepdims=True)).astype(F32)

    def prefix_count(mask):
        run = jnp.zeros((rows, 1), F32)
        parts = []
        for blk in range(l // SEL_BLK):
            m = mask[:, blk * SEL_BLK:(blk + 1) * SEL_BLK].astype(BF16)
            c = jnp.dot(m, tri_ref[...], preferred_element_type=F32) + run
            parts.append(c)
            run = c[:, SEL_BLK - 1:SEL_BLK]
        return jnp.concatenate(parts, axis=1)

    eq_rank = prefix_count(jnp.where(eq, 1.0, 0.0))
    sel = jnp.logical_or(gt, jnp.logical_and(eq, eq_rank <= need))
    pos_ref[...] = jnp.where(sel, prefix_count(jnp.where(sel, 1.0, 0.0)), 0.0)

    slot = (lax.broadcasted_iota(I32, (cap, SEL_BLK), 0) + 1).astype(F32)
    tok = lax.broadcasted_iota(I32, (1, SEL_BLK), 1).astype(F32)

    def compact(r, carry):
        acc_i = jnp.zeros((cap, SEL_BLK), F32)
        acc_g = jnp.zeros((cap, SEL_BLK), F32)
        for blk in range(l // SEL_BLK):
            sl = pl.ds(blk * SEL_BLK, SEL_BLK)
            hit = pos_ref[pl.ds(r, 1), sl] == slot
            acc_i = acc_i + jnp.where(hit, tok + float(blk * SEL_BLK), 0.0)
            acc_g = acc_g + jnp.where(hit, aff_ref[pl.ds(r, 1), sl], 0.0)
        idx_ref[r] = jnp.sum(acc_i, axis=1, keepdims=True).astype(I32)
        gate_ref[r] = jnp.sum(acc_g, axis=1, keepdims=True)
        return carry
    lax.fori_loop(0, rows, compact, 0)


def expert_select(aff_t, cap):
    b, e, l = aff_t.shape
    r = b * e
    tri = jnp.asarray(np.triu(np.ones((SEL_BLK, SEL_BLK), np.float32))).astype(BF16)
    idx, gate = pl.pallas_call(
        functools.partial(_select_kernel, cap=cap),
        grid=(r // SEL_ROWS,),
        in_specs=[pl.BlockSpec((SEL_ROWS, l), lambda i: (i, 0)), pl.BlockSpec((SEL_BLK, SEL_BLK), lambda i: (0, 0))],
        out_specs=[pl.BlockSpec((SEL_ROWS, cap, 1), lambda i: (i, 0, 0))] * 2,
        out_shape=[jax.ShapeDtypeStruct((r, cap, 1), I32), jax.ShapeDtypeStruct((r, cap, 1), F32)],
        scratch_shapes=[pltpu.VMEM((SEL_ROWS, l), F32)],
        compiler_params=_params("parallel"),
        name="expert_select",
    )(aff_t.reshape(r, l), tri)
    return idx.reshape(r, cap), gate.reshape(r, cap)


def _row_copy(src_ref, src_row, dst_ref, dst_row, sem):
    return pltpu.make_async_copy(src_ref.at[pl.ds(src_row, 1)], dst_ref.at[pl.ds(dst_row, 1)], sem)


def _moe_ffn_kernel(idx_ref, x_hbm, wg_ref, wu_ref, wd_ref, o_ref, stage_ref, xb_ref, act_ref, sem,
                    *, nb, cap, seq, n_f):
    e = pl.program_id(0)
    s = pl.program_id(1)
    ne = pl.num_programs(0)
    tf = wg_ref.shape[2]
    slot = e % 2
    has_next = e + 1 < ne

    def issue(expert, bi):
        base = (bi * ne + expert) * cap

        def body(j, carry):
            _row_copy(x_hbm, bi * seq + idx_ref[base + j], stage_ref, j, sem).start()
            return carry
        lax.fori_loop(0, cap, body, 0, unroll=8)

    def land(dst_slot, bi):
        def body(j, carry):
            _row_copy(x_hbm, 0, stage_ref, j, sem).wait()
            return carry
        lax.fori_loop(0, cap, body, 0, unroll=8)
        xb_ref[dst_slot, bi * cap:(bi + 1) * cap, :] = stage_ref[...].astype(BF16)

    @pl.when(jnp.logical_and(e == 0, s == 0))
    def _():
        for bi in range(nb):
            issue(0, bi)
            land(0, bi)

    steps_per_batch = n_f // nb
    rows_per_step = cap // steps_per_batch
    nxt = jnp.where(has_next, e + 1, e)
    for bi in range(1, nb):
        @pl.when(s == bi * steps_per_batch)
        def _(bi=bi):
            land(1 - slot, bi - 1)

    @pl.when(s == n_f + 1)
    def _():
        land(1 - slot, nb - 1)

    @pl.when(s < n_f)
    def _():
        bi = s // steps_per_batch
        row0 = (s - bi * steps_per_batch) * rows_per_step
        base = (bi * ne + nxt) * cap + row0
        for j in range(rows_per_step):
            _row_copy(x_hbm, bi * seq + idx_ref[base + j], stage_ref, row0 + j, sem).start()
        x = xb_ref[slot]
        g = jnp.dot(x, wg_ref[0].astype(BF16), preferred_element_type=F32)
        u = jnp.dot(x, wu_ref[0].astype(BF16), preferred_element_type=F32)
        act = (jax.nn.silu(g) * u).astype(BF16)
        for f in range(n_f):
            @pl.when(s == f)
            def _(f=f):
                act_ref[:, f * tf:(f + 1) * tf] = act

    @pl.when(s >= n_f)
    def _():
        o_ref[0] = jnp.dot(act_ref[...], wd_ref[0].astype(BF16), preferred_element_type=F32).astype(o_ref.dtype)


def moe_ffn(idx, x_rows, w_gate, w_up, w_down, *, layer, nb, seq, tf=256, tn=1024):
    _, ne, d, ff = w_gate.shape
    assert nb == 2
    cap = idx.shape[0] // (nb * ne)
    n_f = ff // tf
    n_d = d // tn
    assert n_d >= 2 and n_f % nb == 0 and cap % (n_f // nb) == 0
    grid_spec = pltpu.PrefetchScalarGridSpec(
        num_scalar_prefetch=1,
        grid=(ne, n_f + n_d),
        in_specs=[
            pl.BlockSpec(memory_space=pl.ANY),
            pl.BlockSpec((None, 1, d, tf), lambda e, s, idx: (layer, e, 0, jnp.minimum(s, n_f - 1))),
            pl.BlockSpec((None, 1, d, tf), lambda e, s, idx: (layer, e, 0, jnp.minimum(s, n_f - 1))),
            pl.BlockSpec((None, 1, ff, tn), lambda e, s, idx: (layer, e, 0, jnp.maximum(s - n_f, 0))),
        ],
        out_specs=pl.BlockSpec((1, nb * cap, tn), lambda e, s, idx: (e, 0, jnp.maximum(s - n_f, 0))),
        scratch_shapes=[
            pltpu.VMEM((cap, d), F32),
            pltpu.VMEM((2, nb * cap, d), BF16),
            pltpu.VMEM((nb * cap, ff), BF16),
            pltpu.SemaphoreType.DMA(()),
        ],
    )
    return pl.pallas_call(
        functools.partial(_moe_ffn_kernel, nb=nb, cap=cap, seq=seq, n_f=n_f),
        grid_spec=grid_spec,
        out_shape=jax.ShapeDtypeStruct((ne, nb * cap, d), BF16),
        compiler_params=_params("arbitrary", "arbitrary"),
        name="moe_ffn",
    )(idx, x_rows, w_gate, w_up, w_down)


def _moe_scatter_kernel(idx_ref, h_in, ys_ref, gate_ref, mod_ref, h_out, rows_ref, sem_in, sem_out, *, nb, cap, seq):
    del h_in
    e = pl.program_id(0)
    ne = pl.num_programs(0)
    sub = 16

    def for_rows(fn):
        def body(j, carry):
            fn(j)
            return carry
        lax.fori_loop(0, cap, body, 0, unroll=8)

    def token_row(bi, j):
        return bi * seq + idx_ref[(bi * ne + e) * cap + j]

    for bi in range(nb):
        for_rows(lambda j, bi=bi: _row_copy(h_out, token_row(bi, j), rows_ref, bi * cap + j, sem_in.at[bi]).start())
    for bi in range(nb):
        for_rows(lambda j, bi=bi: _row_copy(h_out, 0, rows_ref, bi * cap + j, sem_in.at[bi]).wait())

        def update(i, carry, bi=bi):
            r0 = pl.multiple_of(bi * cap + i * sub, sub)
            w = gate_ref[0, pl.ds(r0, sub), :] * mod_ref[bi:bi + 1, :]
            rows_ref[pl.ds(r0, sub), :] = rows_ref[pl.ds(r0, sub), :] + w * ys_ref[0, pl.ds(r0, sub), :].astype(F32)
            for k in range(sub):
                j = i * sub + k
                _row_copy(rows_ref, bi * cap + j, h_out, token_row(bi, j), sem_out.at[bi]).start()
            return carry
        lax.fori_loop(0, cap // sub, update, 0, unroll=2)
    for bi in range(nb):
        for_rows(lambda j, bi=bi: _row_copy(rows_ref, bi * cap + j, h_out, 0, sem_out.at[bi]).wait())


def moe_scatter(idx, h_rows, ys, gates, mod, *, nb, seq):
    ne, n, d = ys.shape
    cap = n // nb
    grid_spec = pltpu.PrefetchScalarGridSpec(
        num_scalar_prefetch=1,
        grid=(ne,),
        in_specs=[
            pl.BlockSpec(memory_space=pl.ANY),
            pl.BlockSpec((1, n, d), lambda e, idx: (e, 0, 0)),
            pl.BlockSpec((1, n, 1), lambda e, idx: (e, 0, 0)),
            pl.BlockSpec((nb, d), lambda e, idx: (0, 0)),
        ],
        out_specs=pl.BlockSpec(memory_space=pl.ANY),
        scratch_shapes=[pltpu.VMEM((n, d), F32), pltpu.SemaphoreType.DMA((nb,)), pltpu.SemaphoreType.DMA((nb,))],
    )
    return pl.pallas_call(
        functools.partial(_moe_scatter_kernel, nb=nb, cap=cap, seq=seq),
        grid_spec=grid_spec,
        out_shape=jax.ShapeDtypeStruct(h_rows.shape, F32),
        input_output_aliases={1: 0},
        compiler_params=_params("arbitrary"),
        name="moe_scatter",
    )(idx, h_rows, ys, gates, mod)


def expert_choice_moe_residual(h, norm_g, shift, scale, gate_mod, router_w, w_gate, w_up, w_down, layer):
    b, l, d = h.shape
    ne = router_w.shape[1]
    cap = 2 * l // ne
    x, aff_t = norm_modulate_router(h, norm_g, shift, scale, router_w)
    idx, gates = expert_select(aff_t, cap)
    idx_flat = idx.reshape(-1)
    ys = moe_ffn(idx_flat, x.reshape(b * l, d), w_gate, w_up, w_down, layer=layer, nb=b, seq=l)
    gates_e = jnp.transpose(gates.reshape(b, ne, cap), (1, 0, 2)).reshape(ne, b * cap, 1)
    out = moe_scatter(idx_flat, h.reshape(b * l, d), ys, gates_e, gate_mod, nb=b, seq=l)
    return out.reshape(b, l, d)


def kernel(x, c, ctx, c_ctx, ada_w, ada_b, norm_mix_g, norm_ffn_g, attn_sgu_w_in, attn_sink, sgu_norm_g, sgu_w_s, sgu_b_s, attn_sgu_w_out, hyena_w_in, hyena_conv_w, hyena_conv_b, hyena_filt_w1, hyena_filt_b1, hyena_filt_w2, hyena_filt_b2, hyena_filt_freq, hyena_filt_w3, hyena_bias, hyena_w_out, router_w, expert_w_gate, expert_w_up, expert_w_down, final_norm_g):
    b, l, d = x.shape
    lc = ctx.shape[1]
    c_rows = jnp.concatenate([c, c_ctx[None, :], jnp.zeros((8 - b - 1, d), F32)], axis=0)
    mods = ada_modulation(c_rows, ada_w, ada_b)

    def mod(layer, rows, k):
        return mods[layer, rows, k * d:(k + 1) * d]

    lat = slice(0, b)
    ctx_rows = jnp.full((b,), b, I32)

    a_lat = norm_modulate(x, norm_mix_g[0], mod(0, lat, 0), mod(0, lat, 1), BF16)
    a_ctx = norm_modulate(ctx, norm_mix_g[0], mod(0, ctx_rows, 0), mod(0, ctx_rows, 1), BF16)
    w_in = attn_sgu_w_in[0]
    qkvuz = project(a_lat.reshape(b * l, d), w_in).reshape(b, l, -1)
    ctx_kv = project(a_ctx.reshape(b * lc, d), w_in, n_start=ATTN_WIDTH, n_cols=2 * KV_WIDTH).reshape(b, lc, -1)
    qk = rope_qk(qkvuz)
    o = windowed_attention(qk, qkvuz, ctx_kv, attn_sink[0])
    s = spatial_gating(qkvuz, sgu_norm_g[0], sgu_w_s[0], sgu_b_s[0])
    mixed = [o.reshape(b * l, -1), s.reshape(b * l, -1)]
    h = project_residual(mixed, attn_sgu_w_out[0], x.reshape(b * l, d), mod(0, lat, 2), rows_per_batch=l).reshape(b, l, d)
    h = expert_choice_moe_residual(h, norm_ffn_g[0], mod(0, lat, 3), mod(0, lat, 4), mod(0, lat, 5),
                                   router_w[0], expert_w_gate, expert_w_up, expert_w_down, 0)

    a_lat = norm_modulate(h, norm_mix_g[1], mod(1, lat, 0), mod(1, lat, 1), BF16)
    u3 = project(a_lat.reshape(b * l, d), hyena_w_in[0]).reshape(b, l, -1)
    x0, z = hyena_pre(u3, hyena_conv_w[0], hyena_conv_b[0])
    hid = filter_mlp(l, hyena_filt_w1[0], hyena_filt_b1[0], hyena_filt_w2[0], hyena_filt_b2[0], hyena_filt_freq[0])
    xy = hyena_long_conv(z, x0, hid, hyena_filt_w3[0], hyena_bias[0])
    h = project_residual([xy.reshape(b * l, d)], hyena_w_out[0], h.reshape(b * l, d), mod(1, lat, 2), rows_per_batch=l).reshape(b, l, d)
    h = expert_choice_moe_residual(h, norm_ffn_g[1], mod(1, lat, 3), mod(1, lat, 4), mod(1, lat, 5),
                                   router_w[1], expert_w_gate, expert_w_up, expert_w_down, 1)
    return final_norm(h, final_norm_g)
```

```python
import functools
import math

import numpy as np
import jax
import jax.numpy as jnp
from jax import lax
from jax.experimental import pallas as pl
from jax.experimental.pallas import tpu as pltpu

F32 = jnp.float32
BF16 = jnp.bfloat16
I32 = jnp.int32
HIGHEST = lax.Precision.HIGHEST

D_MODEL = 4096
GRID_W = 64
NORM_EPS = 1e-6
N_MOD = 6
NEG_INF = -1e30
HEAD_DIM = 128
ATTN_HEADS = 16
ATTN_KV_HEADS = 4
GQ = 4
ATTN_BLOCK = 128
ROPE_THETA = 10000.0
ATTN_WIDTH = 2048
KV_WIDTH = 512
SGU_WIDTH = 2048
SGU_CHUNK = 128
ROPE_WIDTH = ATTN_WIDTH + KV_WIDTH
HYENA_BANDS = 16
HYENA_EMB = 33
HYENA_HID = 64
N_EXPERTS = 16
EXPERT_FF = 1024

LANES = 128
VMEM_LIMIT_BYTES = 56 * 1024 * 1024

FFT_N1 = 64
FFT_N2 = 128
FFT_N = FFT_N1 * FFT_N2
FFT_HALF = FFT_N1 // 2


def _params(*sem):
    return pltpu.CompilerParams(dimension_semantics=sem, vmem_limit_bytes=VMEM_LIMIT_BYTES)


def _ada_kernel(c_ref, w_ref, b_ref, o_ref):
    s = jax.nn.silu(c_ref[...]).astype(BF16)
    o_ref[0] = jnp.dot(s, w_ref[0].astype(BF16), preferred_element_type=F32) + b_ref[0]


def ada_modulation(c_rows, ada_w, ada_b, tn=512):
    depth, d, n = ada_w.shape
    rows = c_rows.shape[0]
    return pl.pallas_call(
        _ada_kernel,
        grid=(depth, n // tn),
        in_specs=[
            pl.BlockSpec((rows, d), lambda l, j: (0, 0)),
            pl.BlockSpec((1, d, tn), lambda l, j: (l, 0, j)),
            pl.BlockSpec((1, 1, tn), lambda l, j: (l, 0, j)),
        ],
        out_specs=pl.BlockSpec((1, rows, tn), lambda l, j: (l, 0, j)),
        out_shape=jax.ShapeDtypeStruct((depth, rows, n), F32),
        compiler_params=_params("parallel", "parallel"),
        name="ada_modulation",
    )(c_rows, ada_w, ada_b.reshape(depth, 1, n))


def _rms(x, g):
    return x * lax.rsqrt(jnp.mean(x * x, axis=-1, keepdims=True) + NORM_EPS) * g


def _norm_mod_kernel(h_ref, g_ref, sh_ref, sc_ref, o_ref):
    y = _rms(h_ref[0], g_ref[...])
    o_ref[0] = (y * (1.0 + sc_ref[0]) + sh_ref[0]).astype(o_ref.dtype)


def norm_modulate(h, g, shift, scale, out_dtype, tl=256):
    b, l, d = h.shape
    tl = min(tl, l)
    return pl.pallas_call(
        _norm_mod_kernel,
        grid=(b, l // tl),
        in_specs=[
            pl.BlockSpec((1, tl, d), lambda i, j: (i, j, 0)),
            pl.BlockSpec((1, d), lambda i, j: (0, 0)),
            pl.BlockSpec((1, 1, d), lambda i, j: (i, 0, 0)),
            pl.BlockSpec((1, 1, d), lambda i, j: (i, 0, 0)),
        ],
        out_specs=pl.BlockSpec((1, tl, d), lambda i, j: (i, j, 0)),
        out_shape=jax.ShapeDtypeStruct((b, l, d), out_dtype),
        compiler_params=_params("parallel", "parallel"),
        name="norm_modulate",
    )(h, g.reshape(1, d), shift.reshape(b, 1, d), scale.reshape(b, 1, d))


def _norm_mod_router_kernel(h_ref, g_ref, sh_ref, sc_ref, rw_ref, o_ref, aff_ref):
    y = _rms(h_ref[0], g_ref[...])
    a = y * (1.0 + sc_ref[0]) + sh_ref[0]
    o_ref[0] = a
    def split(v):
        hi = v.astype(BF16)
        return hi, (v - hi.astype(F32)).astype(BF16)

    def nt(x, y):
        return lax.dot_general(x, y, (((1,), (1,)), ((), ())), preferred_element_type=F32)

    a_hi, a_lo = split(a)
    r_hi, r_lo = split(rw_ref[...])
    lt = nt(r_hi, a_hi) + (nt(r_hi, a_lo) + nt(r_lo, a_hi))
    m = jnp.max(lt, axis=0, keepdims=True)
    p = jnp.exp(lt - m)
    aff_ref[0] = p / jnp.sum(p, axis=0, keepdims=True)


def norm_modulate_router(h, g, shift, scale, router_w, tl=256):
    b, l, d = h.shape
    e = router_w.shape[1]
    return pl.pallas_call(
        _norm_mod_router_kernel,
        grid=(b, l // tl),
        in_specs=[
            pl.BlockSpec((1, tl, d), lambda i, j: (i, j, 0)),
            pl.BlockSpec((1, d), lambda i, j: (0, 0)),
            pl.BlockSpec((1, 1, d), lambda i, j: (i, 0, 0)),
            pl.BlockSpec((1, 1, d), lambda i, j: (i, 0, 0)),
            pl.BlockSpec((e, d), lambda i, j: (0, 0)),
        ],
        out_specs=[
            pl.BlockSpec((1, tl, d), lambda i, j: (i, j, 0)),
            pl.BlockSpec((1, e, tl), lambda i, j: (i, 0, j)),
        ],
        out_shape=[jax.ShapeDtypeStruct((b, l, d), F32), jax.ShapeDtypeStruct((b, e, l), F32)],
        compiler_params=_params("parallel", "parallel"),
        name="norm_modulate_router",
    )(h, g.reshape(1, d), shift.reshape(b, 1, d), scale.reshape(b, 1, d), router_w.T)


def _final_norm_kernel(h_ref, g_ref, o_ref):
    o_ref[0] = _rms(h_ref[0], g_ref[...])


def final_norm(h, g, tl=256):
    b, l, d = h.shape
    return pl.pallas_call(
        _final_norm_kernel,
        grid=(b, l // tl),
        in_specs=[pl.BlockSpec((1, tl, d), lambda i, j: (i, j, 0)), pl.BlockSpec((1, d), lambda i, j: (0, 0))],
        out_specs=pl.BlockSpec((1, tl, d), lambda i, j: (i, j, 0)),
        out_shape=jax.ShapeDtypeStruct((b, l, d), F32),
        compiler_params=_params("parallel", "parallel"),
        name="final_norm",
    )(h, g.reshape(1, d))


def _mm_kernel(a_ref, w_ref, o_ref, wb_ref):
    @pl.when(pl.program_id(1) == 0)
    def _():
        wb_ref[...] = w_ref[...].astype(BF16)

    o_ref[...] = jnp.dot(a_ref[...], wb_ref[...], preferred_element_type=F32).astype(o_ref.dtype)


def _mm_res_kernel(*refs, n_parts):
    a_refs = refs[:n_parts]
    w_ref, res_ref, gate_ref, o_ref, wb_ref = refs[n_parts:]

    @pl.when(pl.program_id(1) == 0)
    def _():
        wb_ref[...] = w_ref[...].astype(BF16)

    acc = None
    k0 = 0
    for a_ref in a_refs:
        kp = a_ref.shape[1]
        part = jnp.dot(a_ref[...], wb_ref[k0:k0 + kp, :], preferred_element_type=F32)
        acc = part if acc is None else acc + part
        k0 += kp
    o_ref[...] = res_ref[...] + gate_ref[0] * acc


def project(a, w, *, n_start=0, n_cols=None, out_dtype=BF16, tm=1024, tn=512):
    m, k = a.shape
    n_cols = w.shape[1] - n_start if n_cols is None else n_cols
    tm = min(tm, m)
    off = n_start // tn
    return pl.pallas_call(
        _mm_kernel,
        grid=(n_cols // tn, m // tm),
        in_specs=[
            pl.BlockSpec((tm, k), lambda j, i: (i, 0)),
            pl.BlockSpec((k, tn), lambda j, i: (0, off + j)),
        ],
        out_specs=pl.BlockSpec((tm, tn), lambda j, i: (i, j)),
        out_shape=jax.ShapeDtypeStruct((m, n_cols), out_dtype),
        scratch_shapes=[pltpu.VMEM((k, tn), BF16)],
        compiler_params=_params("arbitrary", "arbitrary"),
        name="project",
    )(a, w)


def project_residual(parts, w, res, gate, *, rows_per_batch, tm=1024, tn=512):
    m = parts[0].shape[0]
    k, n = w.shape
    assert sum(p.shape[1] for p in parts) == k
    nb = gate.shape[0]
    assert rows_per_batch % tm == 0, "a row tile must not straddle two batch entries (one gate row per tile)"
    tiles_per_batch = rows_per_batch // tm
    return pl.pallas_call(
        functools.partial(_mm_res_kernel, n_parts=len(parts)),
        grid=(n // tn, m // tm),
        in_specs=[pl.BlockSpec((tm, p.shape[1]), lambda j, i: (i, 0)) for p in parts] + [
            pl.BlockSpec((k, tn), lambda j, i: (0, j)),
            pl.BlockSpec((tm, tn), lambda j, i: (i, j)),
            pl.BlockSpec((1, 1, tn), lambda j, i: (i // tiles_per_batch, 0, j)),
        ],
        out_specs=pl.BlockSpec((tm, tn), lambda j, i: (i, j)),
        out_shape=jax.ShapeDtypeStruct((m, n), F32),
        scratch_shapes=[pltpu.VMEM((k, tn), BF16)],
        compiler_params=_params("arbitrary", "arbitrary"),
        name="project_residual",
    )(*parts, w, res, gate.reshape(nb, 1, n))


def _rope_tables(l):
    rows = l // GRID_W
    row = np.repeat(np.arange(rows), GRID_W).astype(np.float32)
    col = np.tile(np.arange(GRID_W), rows).astype(np.float32)
    half = HEAD_DIM // 2
    inv = (ROPE_THETA ** (-np.arange(0, half, 2, dtype=np.float32) / half)).astype(np.float32)
    ang_r = row[:, None] * inv[None, :]
    ang_c = col[:, None] * inv[None, :]
    ang = np.concatenate([ang_r, ang_r, ang_c, ang_c], axis=1)
    sign = np.tile(np.concatenate([-np.ones(half // 2), np.ones(half // 2)]), 2).astype(np.float32)
    return jnp.asarray(np.cos(ang), F32), jnp.asarray(np.sin(ang) * sign[None, :], F32)


def _rope_kernel(x_ref, cos_ref, sin_ref, o_ref):
    cos = cos_ref[...]
    sin = sin_ref[...]
    quarter = HEAD_DIM // 4
    lane = lax.broadcasted_iota(I32, cos.shape, 1)
    first = (lane % (2 * quarter)) < quarter
    for j in range(x_ref.shape[2] // HEAD_DIM):
        sl = slice(j * HEAD_DIM, (j + 1) * HEAD_DIM)
        x = x_ref[0, :, sl].astype(F32)
        partner = jnp.where(first, pltpu.roll(x, HEAD_DIM - quarter, 1), pltpu.roll(x, quarter, 1))
        o_ref[0, :, sl] = (x * cos + partner * sin).astype(o_ref.dtype)


def rope_qk(qkvuz, tl=256):
    b, l, _ = qkvuz.shape
    cos, sin = _rope_tables(l)
    return pl.pallas_call(
        _rope_kernel,
        grid=(b, l // tl),
        in_specs=[
            pl.BlockSpec((1, tl, ROPE_WIDTH), lambda i, j: (i, j, 0)),
            pl.BlockSpec((tl, HEAD_DIM), lambda i, j: (j, 0)),
            pl.BlockSpec((tl, HEAD_DIM), lambda i, j: (j, 0)),
        ],
        out_specs=pl.BlockSpec((1, tl, ROPE_WIDTH), lambda i, j: (i, j, 0)),
        out_shape=jax.ShapeDtypeStruct((b, l, ROPE_WIDTH), BF16),
        compiler_params=_params("parallel", "parallel"),
        name="rope_qk",
    )(qkvuz, cos, sin)


ATTN_QB = 4


def _attn_kernel(sink_ref, q_ref, kp_ref, k0_ref, kn_ref, vp_ref, v0_ref, vn_ref, kx_ref, vx_ref, o_ref):
    h = pl.program_id(1)
    step = pl.program_id(2)
    n_steps = pl.num_programs(2)
    blk = ATTN_BLOCK
    scale = HEAD_DIM ** -0.5
    rows = lambda ref, j: ref[0, j * blk:(j + 1) * blk, :]
    keys = [kp_ref[0]] + [rows(k0_ref, j) for j in range(ATTN_QB)] + [kn_ref[0]]
    vals = [vp_ref[0]] + [rows(v0_ref, j) for j in range(ATTN_QB)] + [vn_ref[0]]
    qi = lax.broadcasted_iota(I32, (GQ * blk, blk), 0) % blk
    ki = lax.broadcasted_iota(I32, (GQ * blk, blk), 1)
    sink = jnp.concatenate([jnp.full((blk, 1), sink_ref[h * GQ + g], F32) for g in range(GQ)], axis=0)

    def fold(parts, op):
        tiles = [p[:, t * blk:(t + 1) * blk] for p in parts for t in range(p.shape[1] // blk)]
        return functools.reduce(op, tiles)

    def pv(p, v):
        return jnp.dot(p.astype(BF16), v, preferred_element_type=F32)

    for j in range(ATTN_QB):
        q = jnp.concatenate([q_ref[0, j * blk:(j + 1) * blk, g * HEAD_DIM:(g + 1) * HEAD_DIM] for g in range(GQ)], axis=0)

        def scores(k):
            return lax.dot_general(q, k, (((1,), (1,)), ((), ())), preferred_element_type=F32) * scale

        sp, s0, sn, sx = scores(keys[j]), scores(keys[j + 1]), scores(keys[j + 2]), scores(kx_ref[0])
        off_p = jnp.where(step > 0, 0, blk) if j == 0 else 0
        off_n = jnp.where(step < n_steps - 1, 0, blk) if j == ATTN_QB - 1 else 0
        sp = jnp.where(ki >= qi + off_p, sp, NEG_INF)
        sn = jnp.where(ki <= qi - off_n, sn, NEG_INF)
        m = jnp.maximum(jnp.max(fold([sp, s0, sn, sx], jnp.maximum), axis=-1, keepdims=True), sink)
        pp, p0, pn, px = jnp.exp(sp - m), jnp.exp(s0 - m), jnp.exp(sn - m), jnp.exp(sx - m)
        denom = jnp.sum(fold([pp, p0, pn, px], jnp.add), axis=-1, keepdims=True) + jnp.exp(sink - m)
        o = pv(pp, vals[j]) + pv(p0, vals[j + 1]) + pv(pn, vals[j + 2]) + pv(px, vx_ref[0])
        o = o / denom
        for g in range(GQ):
            o_ref[0, j * blk:(j + 1) * blk, g * HEAD_DIM:(g + 1) * HEAD_DIM] = o[g * blk:(g + 1) * blk].astype(o_ref.dtype)


def windowed_attention(qk, qkvuz, ctx_kv, sink):
    b, l, _ = qk.shape
    lc = ctx_kv.shape[1]
    nb = l // ATTN_BLOCK
    n_steps = nb // ATTN_QB
    kcol = ATTN_WIDTH // HEAD_DIM
    vcol = (ATTN_WIDTH + KV_WIDTH) // HEAD_DIM
    edge = (1, ATTN_BLOCK, HEAD_DIM)
    cur = (1, ATTN_QB * ATTN_BLOCK, HEAD_DIM)
    prev = lambda s: jnp.maximum(s * ATTN_QB - 1, 0)
    nxt = lambda s: jnp.minimum((s + 1) * ATTN_QB, nb - 1)
    return pl.pallas_call(
        _attn_kernel,
        grid=(b, ATTN_KV_HEADS, n_steps),
        in_specs=[
            pl.BlockSpec(memory_space=pltpu.SMEM),
            pl.BlockSpec((1, ATTN_QB * ATTN_BLOCK, GQ * HEAD_DIM), lambda i, h, s: (i, s, h)),
            pl.BlockSpec(edge, lambda i, h, s: (i, prev(s), kcol + h)),
            pl.BlockSpec(cur, lambda i, h, s: (i, s, kcol + h)),
            pl.BlockSpec(edge, lambda i, h, s: (i, nxt(s), kcol + h)),
            pl.BlockSpec(edge, lambda i, h, s: (i, prev(s), vcol + h)),
            pl.BlockSpec(cur, lambda i, h, s: (i, s, vcol + h)),
            pl.BlockSpec(edge, lambda i, h, s: (i, nxt(s), vcol + h)),
            pl.BlockSpec((1, lc, HEAD_DIM), lambda i, h, s: (i, 0, h)),
            pl.BlockSpec((1, lc, HEAD_DIM), lambda i, h, s: (i, 0, ATTN_KV_HEADS + h)),
        ],
        out_specs=pl.BlockSpec((1, ATTN_QB * ATTN_BLOCK, GQ * HEAD_DIM), lambda i, h, s: (i, s, h)),
        out_shape=jax.ShapeDtypeStruct((b, l, ATTN_WIDTH), BF16),
        compiler_params=_params("parallel", "parallel", "parallel"),
        name="windowed_attention",
    )(sink, qk, qk, qk, qk, qkvuz, qkvuz, qkvuz, ctx_kv, ctx_kv)


def _gelu(x):
    return 0.5 * x * (1.0 + lax.erf(x * (1.0 / math.sqrt(2.0))))


def _sgu_kernel(u_ref, z_ref, g_ref, ws_ref, bs_ref, o_ref):
    for gi in range(ws_ref.shape[0]):
        sl = slice(gi * LANES, (gi + 1) * LANES)
        z = _gelu(z_ref[0, :, sl].astype(F32))
        mu = jnp.mean(z, axis=-1, keepdims=True)
        zc = z - mu
        var = jnp.mean(zc * zc, axis=-1, keepdims=True)
        zn = zc * lax.rsqrt(var + NORM_EPS) * g_ref[:, sl]
        mixed = jnp.dot(ws_ref[gi].astype(BF16), zn.astype(BF16), preferred_element_type=F32) + bs_ref[:, sl]
        o_ref[0, :, sl] = (_gelu(u_ref[0, :, sl].astype(F32)) * mixed).astype(o_ref.dtype)


def spatial_gating(qkvuz, g, w_s, b_s, tw=1024):
    b, l, _ = qkvuz.shape
    groups = w_s.shape[0]
    ucol = (ATTN_WIDTH + 2 * KV_WIDTH) // tw
    zcol = (ATTN_WIDTH + 2 * KV_WIDTH + SGU_WIDTH) // tw
    gpt = tw // LANES
    bias = jnp.repeat(jnp.transpose(b_s), LANES, axis=1)
    return pl.pallas_call(
        _sgu_kernel,
        grid=(b, l // SGU_CHUNK, SGU_WIDTH // tw),
        in_specs=[
            pl.BlockSpec((1, SGU_CHUNK, tw), lambda i, n, j: (i, n, ucol + j)),
            pl.BlockSpec((1, SGU_CHUNK, tw), lambda i, n, j: (i, n, zcol + j)),
            pl.BlockSpec((1, tw), lambda i, n, j: (0, j)),
            pl.BlockSpec((gpt, SGU_CHUNK, SGU_CHUNK), lambda i, n, j: (j, 0, 0)),
            pl.BlockSpec((SGU_CHUNK, tw), lambda i, n, j: (0, j)),
        ],
        out_specs=pl.BlockSpec((1, SGU_CHUNK, tw), lambda i, n, j: (i, n, j)),
        out_shape=jax.ShapeDtypeStruct((b, l, SGU_WIDTH), BF16),
        compiler_params=_params("parallel", "parallel", "parallel"),
        name="spatial_gating",
    )(qkvuz, qkvuz, g.reshape(1, groups * LANES), w_s, bias)


HALO = 16


def _hyena_pre_kernel(*refs, tl):
    (p0, m0, n0, p1, m1, n1, p2, m2, n2, w0, w1, w2, b0, b1, b2, x0_ref, z_ref) = refs
    first = pl.program_id(1) == 0
    last = pl.program_id(1) == pl.num_programs(1) - 1
    rows = lax.broadcasted_iota(I32, m0.shape[1:], 0)

    def conv(p_ref, m_ref, n_ref, w_ref, b_ref):
        x = m_ref[0].astype(F32)
        prev_row = jnp.where(first, 0.0, p_ref[0, HALO - 1:HALO, :].astype(F32))
        next_row = jnp.where(last, 0.0, n_ref[0, 0:1, :].astype(F32))
        xm = jnp.where(rows == 0, prev_row, pltpu.roll(x, 1, 0))
        xp = jnp.where(rows == tl - 1, next_row, pltpu.roll(x, tl - 1, 0))
        return xm * w_ref[0:1, :] + x * w_ref[1:2, :] + xp * w_ref[2:3, :] + b_ref[...]

    x0_ref[0] = conv(p0, m0, n0, w0, b0).astype(x0_ref.dtype)
    x1 = conv(p1, m1, n1, w1, b1)
    v = conv(p2, m2, n2, w2, b2)
    z_ref[0] = (v * x1).astype(z_ref.dtype)


def hyena_pre(u3, conv_w, conv_b, tl=512, tc=512):
    b, l, w3 = u3.shape
    c = w3 // 3
    ct = c // tc
    hb = tl // HALO
    nhb = l // HALO
    in_specs = []
    for part in range(3):
        in_specs += [
            pl.BlockSpec((1, HALO, tc), lambda i, t, j, part=part: (i, jnp.maximum(t * hb - 1, 0), part * ct + j)),
            pl.BlockSpec((1, tl, tc), lambda i, t, j, part=part: (i, t, part * ct + j)),
            pl.BlockSpec((1, HALO, tc), lambda i, t, j, part=part: (i, jnp.minimum((t + 1) * hb, nhb - 1), part * ct + j)),
        ]
    for part in range(3):
        in_specs.append(pl.BlockSpec((3, tc), lambda i, t, j, part=part: (0, part * ct + j)))
    for part in range(3):
        in_specs.append(pl.BlockSpec((1, tc), lambda i, t, j, part=part: (0, part * ct + j)))
    out_spec = pl.BlockSpec((1, tl, tc), lambda i, t, j: (i, t, j))
    return pl.pallas_call(
        functools.partial(_hyena_pre_kernel, tl=tl),
        grid=(b, l // tl, ct),
        in_specs=in_specs,
        out_specs=[out_spec, out_spec],
        out_shape=[jax.ShapeDtypeStruct((b, l, c), BF16), jax.ShapeDtypeStruct((b, l, c), BF16)],
        compiler_params=_params("parallel", "parallel", "parallel"),
        name="hyena_pre",
    )(*([u3] * 9), *([conv_w] * 3), *([conv_b.reshape(1, w3)] * 3))


def _filter_features(l):
    pos = np.arange(l, dtype=np.float32)
    t01 = pos / np.float32(max(l - 1, 1))
    bands = np.linspace(1e-4, HYENA_BANDS - 1, HYENA_BANDS, dtype=np.float32)
    ang = np.float32(2.0 * math.pi / l) * pos[:, None] * bands[None, :]
    feats = np.concatenate([t01[:, None], np.cos(ang), -np.sin(ang)], axis=-1).astype(np.float32)
    feats = np.pad(feats, ((0, 0), (0, HYENA_HID - HYENA_EMB)))
    rev = np.concatenate([np.zeros((1, HYENA_HID), np.float32), feats[:0:-1]], axis=0)
    return np.concatenate([feats, rev], axis=0)


def _filter_mlp_kernel(f_ref, w1_ref, b1_ref, w2_ref, b2_ref, fr_ref, o_ref):
    h = jnp.dot(f_ref[...], w1_ref[...], precision=HIGHEST, preferred_element_type=F32) + b1_ref[...]
    h = jnp.sin(fr_ref[0:1, :] * h)
    h = jnp.dot(h, w2_ref[...], precision=HIGHEST, preferred_element_type=F32) + b2_ref[...]
    o_ref[...] = jnp.sin(fr_ref[1:2, :] * h)


def filter_mlp(l, w1, b1, w2, b2, freq):
    feats = jnp.asarray(_filter_features(l))
    w1p = jnp.pad(w1, ((0, HYENA_HID - HYENA_EMB), (0, 0)))
    hid = pl.pallas_call(
        _filter_mlp_kernel,
        out_shape=jax.ShapeDtypeStruct((2 * l, HYENA_HID), F32),
        compiler_params=pltpu.CompilerParams(vmem_limit_bytes=VMEM_LIMIT_BYTES),
        name="filter_mlp",
    )(feats, w1p, b1.reshape(1, -1), w2, b2.reshape(1, -1), freq)
    return jnp.concatenate([hid[:l], hid[l:]], axis=1)


def _dft_constants():
    n1 = np.arange(FFT_N1, dtype=np.float64)
    n2 = np.arange(FFT_N2, dtype=np.float64)
    k1 = n1
    two_pi = 2.0 * np.pi
    theta = two_pi * (k1[None, :, None] * n1[None, None, :] / FFT_N1 + n2[:, None, None] * k1[None, :, None] / FFT_N)
    gr, gi = np.cos(theta), -np.sin(theta)
    g_cplx = np.concatenate([np.concatenate([gr[:, :, :FFT_HALF], -gi[:, :, :FFT_HALF]], axis=2),
                             np.concatenate([gi[:, :, :FFT_HALF], gr[:, :, :FFT_HALF]], axis=2)], axis=1)
    g_real = np.concatenate([gr, gi], axis=1)
    ang2 = two_pi * np.outer(n2, n2) / FFT_N2
    cm, sm = np.cos(ang2), np.sin(ang2)
    f2 = np.block([[cm, sm], [-sm, cm]])
    f2i = np.block([[cm, -sm], [sm, cm]])
    phi = two_pi * (n1[None, :FFT_HALF, None] * k1[None, None, :] / FFT_N1 + n2[:, None, None] * k1[None, None, :] / FFT_N)
    cp, sp = np.cos(phi) / FFT_N, np.sin(phi) / FFT_N
    hm = np.concatenate([np.concatenate([cp, -sp], axis=2), np.concatenate([sp, cp], axis=2)], axis=1)

    def pair(m):
        p, r, c = m.shape[0] // 2, m.shape[1], m.shape[2]
        out = np.zeros((p, 2 * r, 2 * c), np.float64)
        out[:, :r, :c] = m[0::2]
        out[:, r:, c:] = m[1::2]
        return out

    as_bf16 = lambda a: jnp.asarray(a.astype(np.float32)).astype(BF16)
    return as_bf16(pair(g_cplx)), as_bf16(pair(g_real)), as_bf16(f2), as_bf16(f2i), as_bf16(pair(hm))


def _hyena_fft_kernel(z_ref, x0_ref, hid_ref, w3f_ref, w3b_ref, dl_ref, bias_ref, gc_ref, gk_ref, f2_ref, f2i_ref, hm_ref,
                      o_ref, t_ref, a_ref, kf_ref, b_ref, *, seq):
    two_n1 = 2 * FFT_N1
    two_n2 = 2 * FFT_N2
    ct = dl_ref.shape[1]

    def stage1(g_ref):
        def body(p, carry):
            n2 = 2 * p
            x = jnp.concatenate([t_ref[pl.ds(n2, FFT_N1, stride=FFT_N2), :],
                                 t_ref[pl.ds(n2 + 1, FFT_N1, stride=FFT_N2), :]], axis=0).astype(BF16)
            rows = pl.ds(pl.multiple_of(p * 2 * two_n1, 2 * two_n1), 2 * two_n1)
            a_ref[rows, :] = jnp.dot(g_ref[p], x, preferred_element_type=F32)
            return carry
        lax.fori_loop(0, FFT_N2 // 2, body, 0, unroll=16)

    def stage2(mode):
        def body(k1, carry):
            ar = a_ref[pl.ds(k1, FFT_N2, stride=two_n1), :]
            ai = a_ref[pl.ds(FFT_N1 + k1, FFT_N2, stride=two_n1), :]
            s = jnp.concatenate([ar, ai], axis=0).astype(BF16)
            x = jnp.dot(f2_ref[...], s, preferred_element_type=F32)
            rows = pl.ds(pl.multiple_of(k1 * two_n2, two_n2), two_n2)
            if mode == "filter":
                kf_ref[rows, :] = x.astype(kf_ref.dtype)
            else:
                kf = kf_ref[rows, :].astype(F32)
                xr, xi = x[:FFT_N2], x[FFT_N2:]
                kr, ki = kf[:FFT_N2], kf[FFT_N2:]
                y = jnp.concatenate([xr * kr - xi * ki, xr * ki + xi * kr], axis=0).astype(BF16)
                b_ref[rows, :] = jnp.dot(f2i_ref[...], y, preferred_element_type=F32)
            return carry
        lax.fori_loop(0, FFT_N1, body, 0, unroll=16)

    def stage_out():
        def body(p, carry):
            n2 = 2 * p
            s = jnp.concatenate([b_ref[pl.ds(n2, FFT_N1, stride=two_n2), :],
                                 b_ref[pl.ds(FFT_N2 + n2, FFT_N1, stride=two_n2), :],
                                 b_ref[pl.ds(n2 + 1, FFT_N1, stride=two_n2), :],
                                 b_ref[pl.ds(FFT_N2 + n2 + 1, FFT_N1, stride=two_n2), :]], axis=0).astype(BF16)
            y = jnp.dot(hm_ref[p], s, preferred_element_type=F32)
            for q in range(2):
                lo = q * 2 * FFT_HALF
                t_ref[pl.ds(n2 + q, FFT_HALF, stride=FFT_N2), :] = y[lo:lo + FFT_HALF]
                t_ref[pl.ds(seq + n2 + q, FFT_HALF, stride=FFT_N2), :] = y[lo + FFT_HALF:lo + 2 * FFT_HALF]
            return carry
        lax.fori_loop(0, FFT_N2 // 2, body, 0, unroll=16)

    rb = 512
    span = float(max(seq - 1, 1))

    def build_filter(r, acc):
        r0 = pl.multiple_of(r * rb, rb)
        t_idx = r0 + lax.broadcasted_iota(I32, (rb, ct), 0)
        tf = t_idx.astype(F32)
        hid = hid_ref[pl.ds(r0, rb), :].astype(BF16)
        hf = jnp.dot(hid, w3f_ref[...].astype(BF16), preferred_element_type=F32)
        hf = hf * jnp.exp(-(tf / span) * dl_ref[...])
        hb = jnp.dot(hid, w3b_ref[...].astype(BF16), preferred_element_type=F32)
        hb = jnp.where(t_idx == 0, 0.0, hb * jnp.exp(-((float(seq) - tf) / span) * dl_ref[...]))
        t_ref[pl.ds(r0, rb), :] = hf
        t_ref[pl.ds(seq + r0, rb), :] = hb
        return acc + jnp.sum(jnp.abs(hf), axis=0, keepdims=True) + jnp.sum(jnp.abs(hb), axis=0, keepdims=True)

    norm = lax.fori_loop(0, seq // rb, build_filter, jnp.zeros((1, ct), F32))
    stage1(gk_ref)
    stage2("filter")

    t_ref[0:seq, :] = z_ref[0].astype(F32)
    t_ref[seq:2 * seq, :] = z_ref[1].astype(F32)
    stage1(gc_ref)
    stage2("data")
    stage_out()

    inv_norm = 1.0 / norm
    for bi in range(2):
        def finish(r, carry, bi=bi):
            r0 = pl.multiple_of(r * rb, rb)
            zb = z_ref[bi, pl.ds(r0, rb), :].astype(F32)
            y = t_ref[pl.ds(bi * seq + r0, rb), :] * inv_norm + zb * bias_ref[...]
            o_ref[bi, pl.ds(r0, rb), :] = (x0_ref[bi, pl.ds(r0, rb), :].astype(F32) * y).astype(o_ref.dtype)
            return carry
        lax.fori_loop(0, seq // rb, finish, 0)


def hyena_long_conv(z, x0, hid, w3, bias, ct=128):
    b, l, c = z.shape
    assert b == 2 and 2 * l == FFT_N
    gc, gk, f2, f2i, hm = _dft_constants()
    lo, hi = math.log(1e-2) / 1.5, math.log(1e-2) / 0.3
    deltas = jnp.asarray(np.abs(np.linspace(lo, hi, c, dtype=np.float32)).reshape(1, c))
    zeros = jnp.zeros((HYENA_HID, c), F32)
    w3f = jnp.concatenate([w3[:, :c], zeros], axis=0)
    w3b = jnp.concatenate([zeros, w3[:, c:]], axis=0)
    nct = c // ct
    full = lambda a: pl.BlockSpec(a.shape, lambda j: (0,) * a.ndim, pipeline_mode=pl.Buffered(1))
    return pl.pallas_call(
        functools.partial(_hyena_fft_kernel, seq=l),
        grid=(nct,),
        in_specs=[
            pl.BlockSpec((b, l, ct), lambda j: (0, 0, j)),
            pl.BlockSpec((b, l, ct), lambda j: (0, 0, j)),
            full(hid),
            pl.BlockSpec((2 * HYENA_HID, ct), lambda j: (0, j)),
            pl.BlockSpec((2 * HYENA_HID, ct), lambda j: (0, j)),
            pl.BlockSpec((1, ct), lambda j: (0, j)),
            pl.BlockSpec((1, ct), lambda j: (0, j)),
            full(gc), full(gk), full(f2), full(f2i), full(hm),
        ],
        out_specs=pl.BlockSpec((b, l, ct), lambda j: (0, 0, j)),
        out_shape=jax.ShapeDtypeStruct((b, l, c), BF16),
        scratch_shapes=[
            pltpu.VMEM((2 * l, ct), F32),
            pltpu.VMEM((FFT_N2 * 2 * FFT_N1, ct), F32),
            pltpu.VMEM((FFT_N1 * 2 * FFT_N2, ct), BF16),
            pltpu.VMEM((FFT_N1 * 2 * FFT_N2, ct), F32),
        ],
        compiler_params=_params("parallel"),
        name="hyena_long_conv",
    )(z, x0, hid, w3f, w3b, deltas, bias.reshape(1, c), gc, gk, f2, f2i, hm)


SEL_ROWS = 8
SEL_BLK = 512


def _select_kernel(aff_ref, tri_ref, idx_ref, gate_ref, pos_ref, *, cap):
    x = aff_ref[...]
    rows, l = x.shape
    bits = pltpu.bitcast(x, I32)

    def search(i, prefix):
        cand = prefix | lax.shift_left(jnp.int32(1), 30 - i)
        cnt = jnp.sum((bits >= cand).astype(I32), axis=1, keepdims=True)
        return jnp.where(cnt >= cap, cand, prefix)

    tau = lax.fori_loop(0, 31, search, jnp.zeros((rows, 1), I32))
    gt = bits > tau
    eq = bits == tau
    need = (cap - jnp.sum(gt.astype(I32), axis=1, keepdims=True)).astype(F32)

    def prefix_count(mask):
        run = jnp.zeros((rows, 1), F32)
        parts = []
        for blk in range(l // SEL_BLK):
            m = mask[:, blk * SEL_BLK:(blk + 1) * SEL_BLK].astype(BF16)
            c = jnp.dot(m, tri_ref[...], preferred_element_type=F32) + run
            parts.append(c)
            run = c[:, SEL_BLK - 1:SEL_BLK]
        return jnp.concatenate(parts, axis=1)

    eq_rank = prefix_count(jnp.where(eq, 1.0, 0.0))
    sel = jnp.logical_or(gt, jnp.logical_and(eq, eq_rank <= need))
    pos_ref[...] = jnp.where(sel, prefix_count(jnp.where(sel, 1.0, 0.0)), 0.0)

    slot = (lax.broadcasted_iota(I32, (cap, SEL_BLK), 0) + 1).astype(F32)
    tok = lax.broadcasted_iota(I32, (1, SEL_BLK), 1).astype(F32)

    def compact(r, carry):
        acc_i = jnp.zeros((cap, SEL_BLK), F32)
        acc_g = jnp.zeros((cap, SEL_BLK), F32)
        for blk in range(l // SEL_BLK):
            sl = pl.ds(blk * SEL_BLK, SEL_BLK)
            hit = pos_ref[pl.ds(r, 1), sl] == slot
            acc_i = acc_i + jnp.where(hit, tok + float(blk * SEL_BLK), 0.0)
            acc_g = acc_g + jnp.where(hit, aff_ref[pl.ds(r, 1), sl], 0.0)
        idx_ref[r] = jnp.sum(acc_i, axis=1, keepdims=True).astype(I32)
        gate_ref[r] = jnp.sum(acc_g, axis=1, keepdims=True)
        return carry
    lax.fori_loop(0, rows, compact, 0)


def expert_select(aff_t, cap):
    b, e, l = aff_t.shape
    r = b * e
    tri = jnp.asarray(np.triu(np.ones((SEL_BLK, SEL_BLK), np.float32))).astype(BF16)
    idx, gate = pl.pallas_call(
        functools.partial(_select_kernel, cap=cap),
        grid=(r // SEL_ROWS,),
        in_specs=[pl.BlockSpec((SEL_ROWS, l), lambda i: (i, 0)), pl.BlockSpec((SEL_BLK, SEL_BLK), lambda i: (0, 0))],
        out_specs=[pl.BlockSpec((SEL_ROWS, cap, 1), lambda i: (i, 0, 0))] * 2,
        out_shape=[jax.ShapeDtypeStruct((r, cap, 1), I32), jax.ShapeDtypeStruct((r, cap, 1), F32)],
        scratch_shapes=[pltpu.VMEM((SEL_ROWS, l), F32)],
        compiler_params=_params("parallel"),
        name="expert_select",
    )(aff_t.reshape(r, l), tri)
    return idx.reshape(r, cap), gate.reshape(r, cap)


def _row_copy(src_ref, src_row, dst_ref, dst_row, sem):
    return pltpu.make_async_copy(src_ref.at[pl.ds(src_row, 1)], dst_ref.at[pl.ds(dst_row, 1)], sem)


def _moe_ffn_kernel(idx_ref, x_hbm, wg_ref, wu_ref, wd_ref, o_ref, stage_ref, xb_ref, act_ref, sem,
                    *, nb, cap, seq, n_f):
    e = pl.program_id(0)
    s = pl.program_id(1)
    ne = pl.num_programs(0) - 1
    tf = wg_ref.shape[2]
    slot = e % 2
    f = s // 3
    is_up = s - 3 * f == 0

    def issue(expert, bi):
        base = (bi * ne + expert) * cap

        def body(j, carry):
            _row_copy(x_hbm, bi * seq + idx_ref[base + j], stage_ref, j, sem).start()
            return carry
        lax.fori_loop(0, cap, body, 0, unroll=8)

    def land(dst_slot, bi):
        def body(j, carry):
            _row_copy(x_hbm, 0, stage_ref, j, sem).wait()
            return carry
        lax.fori_loop(0, cap, body, 0, unroll=8)
        xb_ref[dst_slot, bi * cap:(bi + 1) * cap, :] = stage_ref[...].astype(BF16)

    @pl.when(jnp.logical_and(e == 0, s == 0))
    def _():
        for bi in range(nb):
            issue(0, bi)
            land(0, bi)

    tiles_per_batch = n_f // nb
    rows_per_step = cap // tiles_per_batch
    nxt = jnp.minimum(e + 1, ne - 1)
    for bi in range(1, nb):
        @pl.when(jnp.logical_and(e < ne, s == 3 * bi * tiles_per_batch))
        def _(bi=bi):
            land(1 - slot, bi - 1)

    @pl.when(jnp.logical_and(e < ne, s == 3 * n_f - 1))
    def _():
        land(1 - slot, nb - 1)

    @pl.when(jnp.logical_and(e < ne, is_up))
    def _():
        bi = f // tiles_per_batch
        row0 = (f - bi * tiles_per_batch) * rows_per_step
        base = (bi * ne + nxt) * cap + row0
        for j in range(rows_per_step):
            _row_copy(x_hbm, bi * seq + idx_ref[base + j], stage_ref, row0 + j, sem).start()
        x = xb_ref[slot]
        g = jnp.dot(x, wg_ref[0].astype(BF16), preferred_element_type=F32)
        u = jnp.dot(x, wu_ref[0].astype(BF16), preferred_element_type=F32)
        act = (jax.nn.silu(g) * u).astype(BF16)
        for k in range(n_f):
            @pl.when(f == k)
            def _(k=k):
                act_ref[slot, :, k * tf:(k + 1) * tf] = act

    @pl.when(jnp.logical_and(e > 0, jnp.logical_not(is_up)))
    def _():
        a = act_ref[1 - slot]
        o_ref[0] = jnp.dot(a, wd_ref[0].astype(BF16), preferred_element_type=F32).astype(o_ref.dtype)


def moe_ffn(idx, x_rows, w_gate, w_up, w_down, *, layer, nb, seq, tf=256):
    _, ne, d, ff = w_gate.shape
    assert nb == 2
    cap = idx.shape[0] // (nb * ne)
    n_f = ff // tf
    steps = 3 * n_f
    tn = d // (2 * n_f)
    assert tn % LANES == 0 and n_f % nb == 0 and cap % (n_f // nb) == 0

    def up_tile(e, s):
        t = jnp.minimum((e * steps + s + 2) // 3, ne * n_f - 1)
        return t // n_f, t % n_f

    def down_tile(s):
        return 2 * (s // 3) + jnp.maximum(s % 3 - 1, 0)

    def w_up_map(e, s, idx):
        ex, t = up_tile(e, s)
        return (layer, ex, 0, t)

    grid_spec = pltpu.PrefetchScalarGridSpec(
        num_scalar_prefetch=1,
        grid=(ne + 1, steps),
        in_specs=[
            pl.BlockSpec(memory_space=pl.ANY),
            pl.BlockSpec((None, 1, d, tf), w_up_map),
            pl.BlockSpec((None, 1, d, tf), w_up_map),
            pl.BlockSpec((None, 1, ff, tn), lambda e, s, idx: (layer, jnp.maximum(e - 1, 0), 0, down_tile(s))),
        ],
        out_specs=pl.BlockSpec((1, nb * cap, tn),
                               lambda e, s, idx: (jnp.maximum(e - 1, 0), 0, jnp.where(e == 0, 0, down_tile(s)))),
        scratch_shapes=[
            pltpu.VMEM((cap, d), F32),
            pltpu.VMEM((2, nb * cap, d), BF16),
            pltpu.VMEM((2, nb * cap, ff), BF16),
            pltpu.SemaphoreType.DMA(()),
        ],
    )
    return pl.pallas_call(
        functools.partial(_moe_ffn_kernel, nb=nb, cap=cap, seq=seq, n_f=n_f),
        grid_spec=grid_spec,
        out_shape=jax.ShapeDtypeStruct((ne, nb * cap, d), BF16),
        compiler_params=_params("arbitrary", "arbitrary"),
        name="moe_ffn",
    )(idx, x_rows, w_gate, w_up, w_down)


def _moe_scatter_kernel(idx_ref, h_in, ys_ref, gate_ref, mod_ref, h_out, rows_ref, sem_in, sem_out, *, nb, cap, seq):
    del h_in
    e = pl.program_id(0)
    ne = pl.num_programs(0)
    sub = 16

    def for_rows(fn):
        def body(j, carry):
            fn(j)
            return carry
        lax.fori_loop(0, cap, body, 0, unroll=8)

    def token_row(bi, j):
        return bi * seq + idx_ref[(bi * ne + e) * cap + j]

    for bi in range(nb):
        for_rows(lambda j, bi=bi: _row_copy(h_out, token_row(bi, j), rows_ref, bi * cap + j, sem_in.at[bi]).start())
    for bi in range(nb):
        for_rows(lambda j, bi=bi: _row_copy(h_out, 0, rows_ref, bi * cap + j, sem_in.at[bi]).wait())

        def update(i, carry, bi=bi):
            r0 = pl.multiple_of(bi * cap + i * sub, sub)
            w = gate_ref[0, pl.ds(r0, sub), :] * mod_ref[bi:bi + 1, :]
            rows_ref[pl.ds(r0, sub), :] = rows_ref[pl.ds(r0, sub), :] + w * ys_ref[0, pl.ds(r0, sub), :].astype(F32)
            for k in range(sub):
                j = i * sub + k
                _row_copy(rows_ref, bi * cap + j, h_out, token_row(bi, j), sem_out.at[bi]).start()
            return carry
        lax.fori_loop(0, cap // sub, update, 0, unroll=2)
    for bi in range(nb):
        for_rows(lambda j, bi=bi: _row_copy(rows_ref, bi * cap + j, h_out, 0, sem_out.at[bi]).wait())


def moe_scatter(idx, h_rows, ys, gates, mod, *, nb, seq):
    ne, n, d = ys.shape
    cap = n // nb
    grid_spec = pltpu.PrefetchScalarGridSpec(
        num_scalar_prefetch=1,
        grid=(ne,),
        in_specs=[
            pl.BlockSpec(memory_space=pl.ANY),
            pl.BlockSpec((1, n, d), lambda e, idx: (e, 0, 0)),
            pl.BlockSpec((1, n, 1), lambda e, idx: (e, 0, 0)),
            pl.BlockSpec((nb, d), lambda e, idx: (0, 0)),
        ],
        out_specs=pl.BlockSpec(memory_space=pl.ANY),
        scratch_shapes=[pltpu.VMEM((n, d), F32), pltpu.SemaphoreType.DMA((nb,)), pltpu.SemaphoreType.DMA((nb,))],
    )
    return pl.pallas_call(
        functools.partial(_moe_scatter_kernel, nb=nb, cap=cap, seq=seq),
        grid_spec=grid_spec,
        out_shape=jax.ShapeDtypeStruct(h_rows.shape, F32),
        input_output_aliases={1: 0},
        compiler_params=_params("arbitrary"),
        name="moe_scatter",
    )(idx, h_rows, ys, gates, mod)


def expert_choice_moe_residual(h, norm_g, shift, scale, gate_mod, router_w, w_gate, w_up, w_down, layer):
    b, l, d = h.shape
    ne = router_w.shape[1]
    cap = 2 * l // ne
    x, aff_t = norm_modulate_router(h, norm_g, shift, scale, router_w)
    idx, gates = expert_select(aff_t, cap)
    idx_flat = idx.reshape(-1)
    ys = moe_ffn(idx_flat, x.reshape(b * l, d), w_gate, w_up, w_down, layer=layer, nb=b, seq=l)
    gates_e = jnp.transpose(gates.reshape(b, ne, cap), (1, 0, 2)).reshape(ne, b * cap, 1)
    out = moe_scatter(idx_flat, h.reshape(b * l, d), ys, gates_e, gate_mod, nb=b, seq=l)
    return out.reshape(b, l, d)


def kernel(x, c, ctx, c_ctx, ada_w, ada_b, norm_mix_g, norm_ffn_g, attn_sgu_w_in, attn_sink, sgu_norm_g, sgu_w_s, sgu_b_s, attn_sgu_w_out, hyena_w_in, hyena_conv_w, hyena_conv_b, hyena_filt_w1, hyena_filt_b1, hyena_filt_w2, hyena_filt_b2, hyena_filt_freq, hyena_filt_w3, hyena_bias, hyena_w_out, router_w, expert_w_gate, expert_w_up, expert_w_down, final_norm_g):
    b, l, d = x.shape
    lc = ctx.shape[1]
    c_rows = jnp.concatenate([c, c_ctx[None, :], jnp.zeros((8 - b - 1, d), F32)], axis=0)
    mods = ada_modulation(c_rows, ada_w, ada_b)

    def mod(layer, rows, k):
        return mods[layer, rows, k * d:(k + 1) * d]

    lat = slice(0, b)
    ctx_rows = jnp.full((b,), b, I32)

    a_lat = norm_modulate(x, norm_mix_g[0], mod(0, lat, 0), mod(0, lat, 1), BF16)
    a_ctx = norm_modulate(ctx, norm_mix_g[0], mod(0, ctx_rows, 0), mod(0, ctx_rows, 1), BF16)
    w_in = attn_sgu_w_in[0]
    qkvuz = project(a_lat.reshape(b * l, d), w_in).reshape(b, l, -1)
    ctx_kv = project(a_ctx.reshape(b * lc, d), w_in, n_start=ATTN_WIDTH, n_cols=2 * KV_WIDTH).reshape(b, lc, -1)
    qk = rope_qk(qkvuz)
    o = windowed_attention(qk, qkvuz, ctx_kv, attn_sink[0])
    s = spatial_gating(qkvuz, sgu_norm_g[0], sgu_w_s[0], sgu_b_s[0])
    mixed = [o.reshape(b * l, -1), s.reshape(b * l, -1)]
    h = project_residual(mixed, attn_sgu_w_out[0], x.reshape(b * l, d), mod(0, lat, 2), rows_per_batch=l).reshape(b, l, d)
    h = expert_choice_moe_residual(h, norm_ffn_g[0], mod(0, lat, 3), mod(0, lat, 4), mod(0, lat, 5),
                                   router_w[0], expert_w_gate, expert_w_up, expert_w_down, 0)

    a_lat = norm_modulate(h, norm_mix_g[1], mod(1, lat, 0), mod(1, lat, 1), BF16)
    u3 = project(a_lat.reshape(b * l, d), hyena_w_in[0]).reshape(b, l, -1)
    x0, z = hyena_pre(u3, hyena_conv_w[0], hyena_conv_b[0])
    hid = filter_mlp(l, hyena_filt_w1[0], hyena_filt_b1[0], hyena_filt_w2[0], hyena_filt_b2[0], hyena_filt_freq[0])
    xy = hyena_long_conv(z, x0, hid, hyena_filt_w3[0], hyena_bias[0])
    h = project_residual([xy.reshape(b * l, d)], hyena_w_out[0], h.reshape(b * l, d), mod(1, lat, 2), rows_per_batch=l).reshape(b, l, d)
    h = expert_choice_moe_residual(h, norm_ffn_g[1], mod(1, lat, 3), mod(1, lat, 4), mod(1, lat, 5),
                                   router_w[1], expert_w_gate, expert_w_up, expert_w_down, 1)
    return final_norm(h, final_norm_g)
```

```python
import functools
import math

import numpy as np
import jax
import jax.numpy as jnp
from jax import lax
from jax.experimental import pallas as pl
from jax.experimental.pallas import tpu as pltpu

F32 = jnp.float32
BF16 = jnp.bfloat16
I32 = jnp.int32
HIGHEST = lax.Precision.HIGHEST

D_MODEL = 4096
GRID_W = 64
NORM_EPS = 1e-6
N_MOD = 6
NEG_INF = -1e30
HEAD_DIM = 128
ATTN_HEADS = 16
ATTN_KV_HEADS = 4
GQ = 4
ATTN_BLOCK = 128
ROPE_THETA = 10000.0
ATTN_WIDTH = 2048
KV_WIDTH = 512
SGU_WIDTH = 2048
SGU_CHUNK = 128
ROPE_WIDTH = ATTN_WIDTH + KV_WIDTH
HYENA_BANDS = 16
HYENA_EMB = 33
HYENA_HID = 64
N_EXPERTS = 16
EXPERT_FF = 1024

LANES = 128
VMEM_LIMIT_BYTES = 56 * 1024 * 1024

FFT_N1 = 64
FFT_N2 = 128
FFT_N = FFT_N1 * FFT_N2
FFT_HALF = FFT_N1 // 2


def _params(*sem):
    return pltpu.CompilerParams(dimension_semantics=sem, vmem_limit_bytes=VMEM_LIMIT_BYTES)


def _ada_kernel(c_ref, w_ref, b_ref, o_ref):
    s = jax.nn.silu(c_ref[...]).astype(BF16)
    o_ref[0] = jnp.dot(s, w_ref[0].astype(BF16), preferred_element_type=F32) + b_ref[0]


def ada_modulation(c_rows, ada_w, ada_b, tn=512):
    depth, d, n = ada_w.shape
    rows = c_rows.shape[0]
    return pl.pallas_call(
        _ada_kernel,
        grid=(depth, n // tn),
        in_specs=[
            pl.BlockSpec((rows, d), lambda l, j: (0, 0)),
            pl.BlockSpec((1, d, tn), lambda l, j: (l, 0, j)),
            pl.BlockSpec((1, 1, tn), lambda l, j: (l, 0, j)),
        ],
        out_specs=pl.BlockSpec((1, rows, tn), lambda l, j: (l, 0, j)),
        out_shape=jax.ShapeDtypeStruct((depth, rows, n), F32),
        compiler_params=_params("parallel", "parallel"),
        name="ada_modulation",
    )(c_rows, ada_w, ada_b.reshape(depth, 1, n))


def _rms(x, g):
    return x * lax.rsqrt(jnp.mean(x * x, axis=-1, keepdims=True) + NORM_EPS) * g


def _norm_mod_kernel(h_ref, g_ref, sh_ref, sc_ref, o_ref):
    y = _rms(h_ref[0], g_ref[...])
    o_ref[0] = (y * (1.0 + sc_ref[0]) + sh_ref[0]).astype(o_ref.dtype)


def norm_modulate(h, g, shift, scale, out_dtype, tl=256):
    b, l, d = h.shape
    tl = min(tl, l)
    return pl.pallas_call(
        _norm_mod_kernel,
        grid=(b, l // tl),
        in_specs=[
            pl.BlockSpec((1, tl, d), lambda i, j: (i, j, 0)),
            pl.BlockSpec((1, d), lambda i, j: (0, 0)),
            pl.BlockSpec((1, 1, d), lambda i, j: (i, 0, 0)),
            pl.BlockSpec((1, 1, d), lambda i, j: (i, 0, 0)),
        ],
        out_specs=pl.BlockSpec((1, tl, d), lambda i, j: (i, j, 0)),
        out_shape=jax.ShapeDtypeStruct((b, l, d), out_dtype),
        compiler_params=_params("parallel", "parallel"),
        name="norm_modulate",
    )(h, g.reshape(1, d), shift.reshape(b, 1, d), scale.reshape(b, 1, d))


def _norm_mod_router_kernel(h_ref, g_ref, sh_ref, sc_ref, rw_ref, o_ref, aff_ref):
    y = _rms(h_ref[0], g_ref[...])
    a = y * (1.0 + sc_ref[0]) + sh_ref[0]
    o_ref[0] = a
    def split(v):
        hi = v.astype(BF16)
        return hi, (v - hi.astype(F32)).astype(BF16)

    def nt(x, y):
        return lax.dot_general(x, y, (((1,), (1,)), ((), ())), preferred_element_type=F32)

    a_hi, a_lo = split(a)
    r_hi, r_lo = split(rw_ref[...])
    lt = nt(r_hi, a_hi) + (nt(r_hi, a_lo) + nt(r_lo, a_hi))
    m = jnp.max(lt, axis=0, keepdims=True)
    p = jnp.exp(lt - m)
    aff_ref[0] = p / jnp.sum(p, axis=0, keepdims=True)


def norm_modulate_router(h, g, shift, scale, router_w, tl=256):
    b, l, d = h.shape
    e = router_w.shape[1]
    return pl.pallas_call(
        _norm_mod_router_kernel,
        grid=(b, l // tl),
        in_specs=[
            pl.BlockSpec((1, tl, d), lambda i, j: (i, j, 0)),
            pl.BlockSpec((1, d), lambda i, j: (0, 0)),
            pl.BlockSpec((1, 1, d), lambda i, j: (i, 0, 0)),
            pl.BlockSpec((1, 1, d), lambda i, j: (i, 0, 0)),
            pl.BlockSpec((e, d), lambda i, j: (0, 0)),
        ],
        out_specs=[
            pl.BlockSpec((1, tl, d), lambda i, j: (i, j, 0)),
            pl.BlockSpec((1, e, tl), lambda i, j: (i, 0, j)),
        ],
        out_shape=[jax.ShapeDtypeStruct((b, l, d), F32), jax.ShapeDtypeStruct((b, e, l), F32)],
        compiler_params=_params("parallel", "parallel"),
        name="norm_modulate_router",
    )(h, g.reshape(1, d), shift.reshape(b, 1, d), scale.reshape(b, 1, d), router_w.T)


def _final_norm_kernel(h_ref, g_ref, o_ref):
    o_ref[0] = _rms(h_ref[0], g_ref[...])


def final_norm(h, g, tl=256):
    b, l, d = h.shape
    return pl.pallas_call(
        _final_norm_kernel,
        grid=(b, l // tl),
        in_specs=[pl.BlockSpec((1, tl, d), lambda i, j: (i, j, 0)), pl.BlockSpec((1, d), lambda i, j: (0, 0))],
        out_specs=pl.BlockSpec((1, tl, d), lambda i, j: (i, j, 0)),
        out_shape=jax.ShapeDtypeStruct((b, l, d), F32),
        compiler_params=_params("parallel", "parallel"),
        name="final_norm",
    )(h, g.reshape(1, d))


def _mm_kernel(a_ref, w_hbm, o_ref, wf_ref, wb_ref, sem, *, col0, tn):
    j = pl.program_id(0)
    i = pl.program_id(1)

    def w_copy(jj):
        return pltpu.make_async_copy(w_hbm.at[:, pl.ds(pl.multiple_of(col0 + jj * tn, LANES), tn)], wf_ref, sem)

    @pl.when(i == 0)
    def _():
        @pl.when(j == 0)
        def _():
            w_copy(0).start()

        w_copy(j).wait()
        wb_ref[...] = wf_ref[...].astype(BF16)

        @pl.when(j + 1 < pl.num_programs(0))
        def _():
            w_copy(j + 1).start()

    o_ref[...] = jnp.dot(a_ref[...], wb_ref[...], preferred_element_type=F32).astype(o_ref.dtype)


def _mm_res_kernel(*refs, n_parts):
    a_refs = refs[:n_parts]
    w_ref, res_ref, gate_ref, o_ref, wb_ref = refs[n_parts:]

    @pl.when(pl.program_id(1) == 0)
    def _():
        wb_ref[...] = w_ref[...].astype(BF16)

    acc = None
    k0 = 0
    for a_ref in a_refs:
        kp = a_ref.shape[1]
        part = jnp.dot(a_ref[...], wb_ref[k0:k0 + kp, :], preferred_element_type=F32)
        acc = part if acc is None else acc + part
        k0 += kp
    o_ref[...] = res_ref[...] + gate_ref[0] * acc


def project(a, w, *, n_start=0, n_cols=None, out_dtype=BF16, tm=1024, tn=1024):
    m, k = a.shape
    n_cols = w.shape[1] - n_start if n_cols is None else n_cols
    tm = min(tm, m)
    assert n_cols % tn == 0 and n_start % LANES == 0
    return pl.pallas_call(
        functools.partial(_mm_kernel, col0=n_start, tn=tn),
        grid=(n_cols // tn, m // tm),
        in_specs=[
            pl.BlockSpec((tm, k), lambda j, i: (i, 0)),
            pl.BlockSpec(memory_space=pl.ANY),
        ],
        out_specs=pl.BlockSpec((tm, tn), lambda j, i: (i, j)),
        out_shape=jax.ShapeDtypeStruct((m, n_cols), out_dtype),
        scratch_shapes=[pltpu.VMEM((k, tn), F32), pltpu.VMEM((k, tn), BF16), pltpu.SemaphoreType.DMA(())],
        compiler_params=_params("arbitrary", "arbitrary"),
        name="project",
    )(a, w)


def project_residual(parts, w, res, gate, *, rows_per_batch, tm=1024, tn=512):
    m = parts[0].shape[0]
    k, n = w.shape
    assert sum(p.shape[1] for p in parts) == k
    nb = gate.shape[0]
    assert rows_per_batch % tm == 0, "a row tile must not straddle two batch entries (one gate row per tile)"
    tiles_per_batch = rows_per_batch // tm
    return pl.pallas_call(
        functools.partial(_mm_res_kernel, n_parts=len(parts)),
        grid=(n // tn, m // tm),
        in_specs=[pl.BlockSpec((tm, p.shape[1]), lambda j, i: (i, 0)) for p in parts] + [
            pl.BlockSpec((k, tn), lambda j, i: (0, j)),
            pl.BlockSpec((tm, tn), lambda j, i: (i, j)),
            pl.BlockSpec((1, 1, tn), lambda j, i: (i // tiles_per_batch, 0, j)),
        ],
        out_specs=pl.BlockSpec((tm, tn), lambda j, i: (i, j)),
        out_shape=jax.ShapeDtypeStruct((m, n), F32),
        scratch_shapes=[pltpu.VMEM((k, tn), BF16)],
        compiler_params=_params("arbitrary", "arbitrary"),
        name="project_residual",
    )(*parts, w, res, gate.reshape(nb, 1, n))


def _rope_tables(l):
    rows = l // GRID_W
    row = np.repeat(np.arange(rows), GRID_W).astype(np.float32)
    col = np.tile(np.arange(GRID_W), rows).astype(np.float32)
    half = HEAD_DIM // 2
    inv = (ROPE_THETA ** (-np.arange(0, half, 2, dtype=np.float32) / half)).astype(np.float32)
    ang_r = row[:, None] * inv[None, :]
    ang_c = col[:, None] * inv[None, :]
    ang = np.concatenate([ang_r, ang_r, ang_c, ang_c], axis=1)
    sign = np.tile(np.concatenate([-np.ones(half // 2), np.ones(half // 2)]), 2).astype(np.float32)
    return jnp.asarray(np.cos(ang), F32), jnp.asarray(np.sin(ang) * sign[None, :], F32)


def _rope_kernel(x_ref, cos_ref, sin_ref, o_ref):
    cos = cos_ref[...]
    sin = sin_ref[...]
    quarter = HEAD_DIM // 4
    lane = lax.broadcasted_iota(I32, cos.shape, 1)
    first = (lane % (2 * quarter)) < quarter
    for j in range(x_ref.shape[2] // HEAD_DIM):
        sl = slice(j * HEAD_DIM, (j + 1) * HEAD_DIM)
        x = x_ref[0, :, sl].astype(F32)
        partner = jnp.where(first, pltpu.roll(x, HEAD_DIM - quarter, 1), pltpu.roll(x, quarter, 1))
        o_ref[0, :, sl] = (x * cos + partner * sin).astype(o_ref.dtype)


def rope_qk(qkvuz, tl=256):
    b, l, _ = qkvuz.shape
    cos, sin = _rope_tables(l)
    return pl.pallas_call(
        _rope_kernel,
        grid=(b, l // tl),
        in_specs=[
            pl.BlockSpec((1, tl, ROPE_WIDTH), lambda i, j: (i, j, 0)),
            pl.BlockSpec((tl, HEAD_DIM), lambda i, j: (j, 0)),
            pl.BlockSpec((tl, HEAD_DIM), lambda i, j: (j, 0)),
        ],
        out_specs=pl.BlockSpec((1, tl, ROPE_WIDTH), lambda i, j: (i, j, 0)),
        out_shape=jax.ShapeDtypeStruct((b, l, ROPE_WIDTH), BF16),
        compiler_params=_params("parallel", "parallel"),
        name="rope_qk",
    )(qkvuz, cos, sin)


ATTN_QB = 4


def _attn_kernel(sink_ref, q_ref, kp_ref, k0_ref, kn_ref, vp_ref, v0_ref, vn_ref, kx_ref, vx_ref, o_ref):
    h = pl.program_id(1)
    step = pl.program_id(2)
    n_steps = pl.num_programs(2)
    blk = ATTN_BLOCK
    log2e = 1.0 / math.log(2.0)
    scale = HEAD_DIM ** -0.5 * log2e
    rows = lambda ref, j: ref[0, j * blk:(j + 1) * blk, :]
    keys = [kp_ref[0]] + [rows(k0_ref, j) for j in range(ATTN_QB)] + [kn_ref[0]]
    vals = [vp_ref[0]] + [rows(v0_ref, j) for j in range(ATTN_QB)] + [vn_ref[0]]
    qi = lax.broadcasted_iota(I32, (GQ * blk, blk), 0) % blk
    ki = lax.broadcasted_iota(I32, (GQ * blk, blk), 1)
    sink = jnp.concatenate([jnp.full((blk, 1), sink_ref[h * GQ + g] * log2e, F32) for g in range(GQ)], axis=0)

    def fold(parts, op):
        tiles = [p[:, t * blk:(t + 1) * blk] for p in parts for t in range(p.shape[1] // blk)]
        return functools.reduce(op, tiles)

    def pv(p, v):
        return jnp.dot(p.astype(BF16), v, preferred_element_type=F32)

    for j in range(ATTN_QB):
        q = jnp.concatenate([q_ref[0, j * blk:(j + 1) * blk, g * HEAD_DIM:(g + 1) * HEAD_DIM] for g in range(GQ)], axis=0)

        def scores(k):
            return lax.dot_general(q, k, (((1,), (1,)), ((), ())), preferred_element_type=F32) * scale

        sp, s0, sn, sx = scores(keys[j]), scores(keys[j + 1]), scores(keys[j + 2]), scores(kx_ref[0])
        off_p = jnp.where(step > 0, 0, blk) if j == 0 else 0
        off_n = jnp.where(step < n_steps - 1, 0, blk) if j == ATTN_QB - 1 else 0
        sp = jnp.where(ki >= qi + off_p, sp, NEG_INF)
        sn = jnp.where(ki <= qi - off_n, sn, NEG_INF)
        m = jnp.maximum(jnp.max(fold([sp, s0, sn, sx], jnp.maximum), axis=-1, keepdims=True), sink)
        pp, p0, pn, px = jnp.exp2(sp - m), jnp.exp2(s0 - m), jnp.exp2(sn - m), jnp.exp2(sx - m)
        denom = jnp.sum(fold([pp, p0, pn, px], jnp.add), axis=-1, keepdims=True) + jnp.exp2(sink - m)
        o = pv(pp, vals[j]) + pv(p0, vals[j + 1]) + pv(pn, vals[j + 2]) + pv(px, vx_ref[0])
        o = o / denom
        for g in range(GQ):
            o_ref[0, j * blk:(j + 1) * blk, g * HEAD_DIM:(g + 1) * HEAD_DIM] = o[g * blk:(g + 1) * blk].astype(o_ref.dtype)


def windowed_attention(qk, qkvuz, ctx_kv, sink):
    b, l, _ = qk.shape
    lc = ctx_kv.shape[1]
    nb = l // ATTN_BLOCK
    n_steps = nb // ATTN_QB
    kcol = ATTN_WIDTH // HEAD_DIM
    vcol = (ATTN_WIDTH + KV_WIDTH) // HEAD_DIM
    edge = (1, ATTN_BLOCK, HEAD_DIM)
    cur = (1, ATTN_QB * ATTN_BLOCK, HEAD_DIM)
    prev = lambda s: jnp.maximum(s * ATTN_QB - 1, 0)
    nxt = lambda s: jnp.minimum((s + 1) * ATTN_QB, nb - 1)
    return pl.pallas_call(
        _attn_kernel,
        grid=(b, ATTN_KV_HEADS, n_steps),
        in_specs=[
            pl.BlockSpec(memory_space=pltpu.SMEM),
            pl.BlockSpec((1, ATTN_QB * ATTN_BLOCK, GQ * HEAD_DIM), lambda i, h, s: (i, s, h)),
            pl.BlockSpec(edge, lambda i, h, s: (i, prev(s), kcol + h)),
            pl.BlockSpec(cur, lambda i, h, s: (i, s, kcol + h)),
            pl.BlockSpec(edge, lambda i, h, s: (i, nxt(s), kcol + h)),
            pl.BlockSpec(edge, lambda i, h, s: (i, prev(s), vcol + h)),
            pl.BlockSpec(cur, lambda i, h, s: (i, s, vcol + h)),
            pl.BlockSpec(edge, lambda i, h, s: (i, nxt(s), vcol + h)),
            pl.BlockSpec((1, lc, HEAD_DIM), lambda i, h, s: (i, 0, h)),
            pl.BlockSpec((1, lc, HEAD_DIM), lambda i, h, s: (i, 0, ATTN_KV_HEADS + h)),
        ],
        out_specs=pl.BlockSpec((1, ATTN_QB * ATTN_BLOCK, GQ * HEAD_DIM), lambda i, h, s: (i, s, h)),
        out_shape=jax.ShapeDtypeStruct((b, l, ATTN_WIDTH), BF16),
        compiler_params=_params("parallel", "parallel", "parallel"),
        name="windowed_attention",
    )(sink, qk, qk, qk, qk, qkvuz, qkvuz, qkvuz, ctx_kv, ctx_kv)


def _gelu(x):
    return 0.5 * x * (1.0 + lax.erf(x * (1.0 / math.sqrt(2.0))))


SGU_STEP_CHUNKS = 2


def _sgu_kernel(u_ref, z_ref, g_ref, ws_ref, bs_ref, o_ref):
    for ci in range(SGU_STEP_CHUNKS):
        rows = slice(ci * SGU_CHUNK, (ci + 1) * SGU_CHUNK)
        for gi in range(ws_ref.shape[0]):
            sl = slice(gi * LANES, (gi + 1) * LANES)
            z = _gelu(z_ref[0, rows, sl].astype(F32))
            mu = jnp.mean(z, axis=-1, keepdims=True)
            zc = z - mu
            var = jnp.mean(zc * zc, axis=-1, keepdims=True)
            zn = zc * lax.rsqrt(var + NORM_EPS) * g_ref[:, sl]
            mixed = jnp.dot(ws_ref[gi].astype(BF16), zn.astype(BF16), preferred_element_type=F32) + bs_ref[:, sl]
            o_ref[0, rows, sl] = (_gelu(u_ref[0, rows, sl].astype(F32)) * mixed).astype(o_ref.dtype)


def spatial_gating(qkvuz, g, w_s, b_s, tw=1024):
    b, l, _ = qkvuz.shape
    groups = w_s.shape[0]
    ucol = (ATTN_WIDTH + 2 * KV_WIDTH) // tw
    zcol = (ATTN_WIDTH + 2 * KV_WIDTH + SGU_WIDTH) // tw
    gpt = tw // LANES
    tl = SGU_STEP_CHUNKS * SGU_CHUNK
    bias = jnp.repeat(jnp.transpose(b_s), LANES, axis=1)
    return pl.pallas_call(
        _sgu_kernel,
        grid=(SGU_WIDTH // tw, b, l // tl),
        in_specs=[
            pl.BlockSpec((1, tl, tw), lambda j, i, n: (i, n, ucol + j)),
            pl.BlockSpec((1, tl, tw), lambda j, i, n: (i, n, zcol + j)),
            pl.BlockSpec((1, tw), lambda j, i, n: (0, j)),
            pl.BlockSpec((gpt, SGU_CHUNK, SGU_CHUNK), lambda j, i, n: (j, 0, 0)),
            pl.BlockSpec((SGU_CHUNK, tw), lambda j, i, n: (0, j)),
        ],
        out_specs=pl.BlockSpec((1, tl, tw), lambda j, i, n: (i, n, j)),
        out_shape=jax.ShapeDtypeStruct((b, l, SGU_WIDTH), BF16),
        compiler_params=_params("parallel", "parallel", "parallel"),
        name="spatial_gating",
    )(qkvuz, qkvuz, g.reshape(1, groups * LANES), w_s, bias)


HALO = 16


def _hyena_pre_kernel(*refs, tl):
    (p0, m0, n0, p1, m1, n1, p2, m2, n2, w0, w1, w2, b0, b1, b2, x0_ref, z_ref) = refs
    first = pl.program_id(1) == 0
    last = pl.program_id(1) == pl.num_programs(1) - 1
    rows = lax.broadcasted_iota(I32, m0.shape[1:], 0)

    def conv(p_ref, m_ref, n_ref, w_ref, b_ref):
        x = m_ref[0].astype(F32)
        prev_row = jnp.where(first, 0.0, p_ref[0, HALO - 1:HALO, :].astype(F32))
        next_row = jnp.where(last, 0.0, n_ref[0, 0:1, :].astype(F32))
        xm = jnp.where(rows == 0, prev_row, pltpu.roll(x, 1, 0))
        xp = jnp.where(rows == tl - 1, next_row, pltpu.roll(x, tl - 1, 0))
        return xm * w_ref[0:1, :] + x * w_ref[1:2, :] + xp * w_ref[2:3, :] + b_ref[...]

    x0_ref[0] = conv(p0, m0, n0, w0, b0).astype(x0_ref.dtype)
    x1 = conv(p1, m1, n1, w1, b1)
    v = conv(p2, m2, n2, w2, b2)
    z_ref[0] = (v * x1).astype(z_ref.dtype)


def hyena_pre(u3, conv_w, conv_b, tl=1024, tc=512):
    b, l, w3 = u3.shape
    c = w3 // 3
    ct = c // tc
    hb = tl // HALO
    nhb = l // HALO
    in_specs = []
    for part in range(3):
        in_specs += [
            pl.BlockSpec((1, HALO, tc), lambda i, t, j, part=part: (i, jnp.maximum(t * hb - 1, 0), part * ct + j)),
            pl.BlockSpec((1, tl, tc), lambda i, t, j, part=part: (i, t, part * ct + j)),
            pl.BlockSpec((1, HALO, tc), lambda i, t, j, part=part: (i, jnp.minimum((t + 1) * hb, nhb - 1), part * ct + j)),
        ]
    for part in range(3):
        in_specs.append(pl.BlockSpec((3, tc), lambda i, t, j, part=part: (0, part * ct + j)))
    for part in range(3):
        in_specs.append(pl.BlockSpec((1, tc), lambda i, t, j, part=part: (0, part * ct + j)))
    out_spec = pl.BlockSpec((1, tl, tc), lambda i, t, j: (i, t, j))
    return pl.pallas_call(
        functools.partial(_hyena_pre_kernel, tl=tl),
        grid=(b, l // tl, ct),
        in_specs=in_specs,
        out_specs=[out_spec, out_spec],
        out_shape=[jax.ShapeDtypeStruct((b, l, c), BF16), jax.ShapeDtypeStruct((b, l, c), BF16)],
        compiler_params=_params("parallel", "parallel", "parallel"),
        name="hyena_pre",
    )(*([u3] * 9), *([conv_w] * 3), *([conv_b.reshape(1, w3)] * 3))


def _filter_features(l):
    pos = np.arange(l, dtype=np.float32)
    t01 = pos / np.float32(max(l - 1, 1))
    bands = np.linspace(1e-4, HYENA_BANDS - 1, HYENA_BANDS, dtype=np.float32)
    ang = np.float32(2.0 * math.pi / l) * pos[:, None] * bands[None, :]
    feats = np.concatenate([t01[:, None], np.cos(ang), -np.sin(ang)], axis=-1).astype(np.float32)
    feats = np.pad(feats, ((0, 0), (0, HYENA_HID - HYENA_EMB)))
    rev = np.concatenate([np.zeros((1, HYENA_HID), np.float32), feats[:0:-1]], axis=0)
    return np.concatenate([feats, rev], axis=0)


def _filter_mlp_kernel(f_ref, w1_ref, b1_ref, w2_ref, b2_ref, fr_ref, o_ref):
    h = jnp.dot(f_ref[...], w1_ref[...], precision=HIGHEST, preferred_element_type=F32) + b1_ref[...]
    h = jnp.sin(fr_ref[0:1, :] * h)
    h = jnp.dot(h, w2_ref[...], precision=HIGHEST, preferred_element_type=F32) + b2_ref[...]
    o_ref[...] = jnp.sin(fr_ref[1:2, :] * h)


def filter_mlp(l, w1, b1, w2, b2, freq):
    feats = jnp.asarray(_filter_features(l))
    w1p = jnp.pad(w1, ((0, HYENA_HID - HYENA_EMB), (0, 0)))
    hid = pl.pallas_call(
        _filter_mlp_kernel,
        out_shape=jax.ShapeDtypeStruct((2 * l, HYENA_HID), F32),
        compiler_params=pltpu.CompilerParams(vmem_limit_bytes=VMEM_LIMIT_BYTES),
        name="filter_mlp",
    )(feats, w1p, b1.reshape(1, -1), w2, b2.reshape(1, -1), freq)
    return jnp.concatenate([hid[:l], hid[l:]], axis=1)


def _dft_constants():
    n1 = np.arange(FFT_N1, dtype=np.float64)
    n2 = np.arange(FFT_N2, dtype=np.float64)
    k1 = n1
    two_pi = 2.0 * np.pi
    theta = two_pi * (k1[None, :, None] * n1[None, None, :] / FFT_N1 + n2[:, None, None] * k1[None, :, None] / FFT_N)
    gr, gi = np.cos(theta), -np.sin(theta)
    g_cplx = np.concatenate([np.concatenate([gr[:, :, :FFT_HALF], -gi[:, :, :FFT_HALF]], axis=2),
                             np.concatenate([gi[:, :, :FFT_HALF], gr[:, :, :FFT_HALF]], axis=2)], axis=1)
    g_real = np.concatenate([gr, gi], axis=1)
    ang2 = two_pi * np.outer(n2, n2) / FFT_N2
    cm, sm = np.cos(ang2), np.sin(ang2)
    f2 = np.block([[cm, sm], [-sm, cm]])
    f2i = np.block([[cm, -sm], [sm, cm]])
    phi = two_pi * (n1[None, :FFT_HALF, None] * k1[None, None, :] / FFT_N1 + n2[:, None, None] * k1[None, None, :] / FFT_N)
    cp, sp = np.cos(phi) / FFT_N, np.sin(phi) / FFT_N
    hm = np.concatenate([np.concatenate([cp, -sp], axis=2), np.concatenate([sp, cp], axis=2)], axis=1)

    def pair(m):
        p, r, c = m.shape[0] // 2, m.shape[1], m.shape[2]
        out = np.zeros((p, 2 * r, 2 * c), np.float64)
        out[:, :r, :c] = m[0::2]
        out[:, r:, c:] = m[1::2]
        return out

    as_bf16 = lambda a: jnp.asarray(a.astype(np.float32)).astype(BF16)
    return as_bf16(pair(g_cplx)), as_bf16(pair(g_real)), as_bf16(f2), as_bf16(f2i), as_bf16(pair(hm))


def _hyena_fft_kernel(z_ref, x0_ref, hid_ref, w3f_ref, w3b_ref, dl_ref, bias_ref, gc_ref, gk_ref, f2_ref, f2i_ref, hm_ref,
                      o_ref, t_ref, a_ref, kf_ref, b_ref, *, seq):
    two_n1 = 2 * FFT_N1
    two_n2 = 2 * FFT_N2
    ct = dl_ref.shape[1]

    def stage1(g_ref):
        def body(p, carry):
            n2 = 2 * p
            x = jnp.concatenate([t_ref[pl.ds(n2, FFT_N1, stride=FFT_N2), :],
                                 t_ref[pl.ds(n2 + 1, FFT_N1, stride=FFT_N2), :]], axis=0).astype(BF16)
            rows = pl.ds(pl.multiple_of(p * 2 * two_n1, 2 * two_n1), 2 * two_n1)
            a_ref[rows, :] = jnp.dot(g_ref[p], x, preferred_element_type=F32)
            return carry
        lax.fori_loop(0, FFT_N2 // 2, body, 0, unroll=16)

    def stage2(mode):
        def body(k1, carry):
            ar = a_ref[pl.ds(k1, FFT_N2, stride=two_n1), :]
            ai = a_ref[pl.ds(FFT_N1 + k1, FFT_N2, stride=two_n1), :]
            s = jnp.concatenate([ar, ai], axis=0).astype(BF16)
            x = jnp.dot(f2_ref[...], s, preferred_element_type=F32)
            rows = pl.ds(pl.multiple_of(k1 * two_n2, two_n2), two_n2)
            if mode == "filter":
                kf_ref[rows, :] = x.astype(kf_ref.dtype)
            else:
                kf = kf_ref[rows, :].astype(F32)
                xr, xi = x[:FFT_N2], x[FFT_N2:]
                kr, ki = kf[:FFT_N2], kf[FFT_N2:]
                y = jnp.concatenate([xr * kr - xi * ki, xr * ki + xi * kr], axis=0).astype(BF16)
                b_ref[rows, :] = jnp.dot(f2i_ref[...], y, preferred_element_type=F32)
            return carry
        lax.fori_loop(0, FFT_N1, body, 0, unroll=16)

    def stage_out():
        def body(p, carry):
            n2 = 2 * p
            s = jnp.concatenate([b_ref[pl.ds(n2, FFT_N1, stride=two_n2), :],
                                 b_ref[pl.ds(FFT_N2 + n2, FFT_N1, stride=two_n2), :],
                                 b_ref[pl.ds(n2 + 1, FFT_N1, stride=two_n2), :],
                                 b_ref[pl.ds(FFT_N2 + n2 + 1, FFT_N1, stride=two_n2), :]], axis=0).astype(BF16)
            y = jnp.dot(hm_ref[p], s, preferred_element_type=F32)
            for q in range(2):
                lo = q * 2 * FFT_HALF
                t_ref[pl.ds(n2 + q, FFT_HALF, stride=FFT_N2), :] = y[lo:lo + FFT_HALF]
                t_ref[pl.ds(seq + n2 + q, FFT_HALF, stride=FFT_N2), :] = y[lo + FFT_HALF:lo + 2 * FFT_HALF]
            return carry
        lax.fori_loop(0, FFT_N2 // 2, body, 0, unroll=16)

    rb = 512
    span = float(max(seq - 1, 1))

    def build_filter(r, acc):
        r0 = pl.multiple_of(r * rb, rb)
        t_idx = r0 + lax.broadcasted_iota(I32, (rb, ct), 0)
        tf = t_idx.astype(F32)
        hid = hid_ref[pl.ds(r0, rb), :].astype(BF16)
        hf = jnp.dot(hid, w3f_ref[...].astype(BF16), preferred_element_type=F32)
        hf = hf * jnp.exp(-(tf / span) * dl_ref[...])
        hb = jnp.dot(hid, w3b_ref[...].astype(BF16), preferred_element_type=F32)
        hb = jnp.where(t_idx == 0, 0.0, hb * jnp.exp(-((float(seq) - tf) / span) * dl_ref[...]))
        t_ref[pl.ds(r0, rb), :] = hf
        t_ref[pl.ds(seq + r0, rb), :] = hb
        return acc + jnp.sum(jnp.abs(hf), axis=0, keepdims=True) + jnp.sum(jnp.abs(hb), axis=0, keepdims=True)

    norm = lax.fori_loop(0, seq // rb, build_filter, jnp.zeros((1, ct), F32))
    stage1(gk_ref)
    stage2("filter")

    t_ref[0:seq, :] = z_ref[0].astype(F32)
    t_ref[seq:2 * seq, :] = z_ref[1].astype(F32)
    stage1(gc_ref)
    stage2("data")
    stage_out()

    inv_norm = 1.0 / norm
    for bi in range(2):
        def finish(r, carry, bi=bi):
            r0 = pl.multiple_of(r * rb, rb)
            zb = z_ref[bi, pl.ds(r0, rb), :].astype(F32)
            y = t_ref[pl.ds(bi * seq + r0, rb), :] * inv_norm + zb * bias_ref[...]
            o_ref[bi, pl.ds(r0, rb), :] = (x0_ref[bi, pl.ds(r0, rb), :].astype(F32) * y).astype(o_ref.dtype)
            return carry
        lax.fori_loop(0, seq // rb, finish, 0)


def hyena_long_conv(z, x0, hid, w3, bias, ct=128):
    b, l, c = z.shape
    assert b == 2 and 2 * l == FFT_N
    gc, gk, f2, f2i, hm = _dft_constants()
    lo, hi = math.log(1e-2) / 1.5, math.log(1e-2) / 0.3
    deltas = jnp.asarray(np.abs(np.linspace(lo, hi, c, dtype=np.float32)).reshape(1, c))
    zeros = jnp.zeros((HYENA_HID, c), F32)
    w3f = jnp.concatenate([w3[:, :c], zeros], axis=0)
    w3b = jnp.concatenate([zeros, w3[:, c:]], axis=0)
    nct = c // ct
    full = lambda a: pl.BlockSpec(a.shape, lambda j: (0,) * a.ndim, pipeline_mode=pl.Buffered(1))
    return pl.pallas_call(
        functools.partial(_hyena_fft_kernel, seq=l),
        grid=(nct,),
        in_specs=[
            pl.BlockSpec((b, l, ct), lambda j: (0, 0, j)),
            pl.BlockSpec((b, l, ct), lambda j: (0, 0, j)),
            full(hid),
            pl.BlockSpec((2 * HYENA_HID, ct), lambda j: (0, j)),
            pl.BlockSpec((2 * HYENA_HID, ct), lambda j: (0, j)),
            pl.BlockSpec((1, ct), lambda j: (0, j)),
            pl.BlockSpec((1, ct), lambda j: (0, j)),
            full(gc), full(gk), full(f2), full(f2i), full(hm),
        ],
        out_specs=pl.BlockSpec((b, l, ct), lambda j: (0, 0, j)),
        out_shape=jax.ShapeDtypeStruct((b, l, c), BF16),
        scratch_shapes=[
            pltpu.VMEM((2 * l, ct), F32),
            pltpu.VMEM((FFT_N2 * 2 * FFT_N1, ct), F32),
            pltpu.VMEM((FFT_N1 * 2 * FFT_N2, ct), BF16),
            pltpu.VMEM((FFT_N1 * 2 * FFT_N2, ct), F32),
        ],
        compiler_params=_params("parallel"),
        name="hyena_long_conv",
    )(z, x0, hid, w3f, w3b, deltas, bias.reshape(1, c), gc, gk, f2, f2i, hm)


SEL_ROWS = 8
SEL_BLK = 512


def _select_kernel(aff_ref, tri_ref, idx_ref, gate_ref, pos_ref, *, cap):
    x = aff_ref[...]
    rows, l = x.shape
    bits = pltpu.bitcast(x, I32)

    def search(i, prefix):
        cand = prefix | lax.shift_left(jnp.int32(1), 30 - i)
        cnt = jnp.sum((bits >= cand).astype(I32), axis=1, keepdims=True)
        return jnp.where(cnt >= cap, cand, prefix)

    tau = lax.fori_loop(0, 31, search, jnp.zeros((rows, 1), I32))
    gt = bits > tau
    eq = bits == tau
    need = (cap - jnp.sum(gt.astype(I32), axis=1, keepdims=True)).astype(F32)

    def prefix_count(mask):
        run = jnp.zeros((rows, 1), F32)
        parts = []
        for blk in range(l // SEL_BLK):
            m = mask[:, blk * SEL_BLK:(blk + 1) * SEL_BLK].astype(BF16)
            c = jnp.dot(m, tri_ref[...], preferred_element_type=F32) + run
            parts.append(c)
            run = c[:, SEL_BLK - 1:SEL_BLK]
        return jnp.concatenate(parts, axis=1)

    eq_rank = prefix_count(jnp.where(eq, 1.0, 0.0))
    sel = jnp.logical_or(gt, jnp.logical_and(eq, eq_rank <= need))
    pos_ref[...] = jnp.where(sel, prefix_count(jnp.where(sel, 1.0, 0.0)), 0.0)

    slot = (lax.broadcasted_iota(I32, (cap, SEL_BLK), 0) + 1).astype(F32)
    tok = lax.broadcasted_iota(I32, (1, SEL_BLK), 1).astype(F32)

    def compact(r, carry):
        acc_i = jnp.zeros((cap, SEL_BLK), F32)
        acc_g = jnp.zeros((cap, SEL_BLK), F32)
        for blk in range(l // SEL_BLK):
            sl = pl.ds(blk * SEL_BLK, SEL_BLK)
            hit = pos_ref[pl.ds(r, 1), sl] == slot
            acc_i = acc_i + jnp.where(hit, tok + float(blk * SEL_BLK), 0.0)
            acc_g = acc_g + jnp.where(hit, aff_ref[pl.ds(r, 1), sl], 0.0)
        idx_ref[r] = jnp.sum(acc_i, axis=1, keepdims=True).astype(I32)
        gate_ref[r] = jnp.sum(acc_g, axis=1, keepdims=True)
        return carry
    lax.fori_loop(0, rows, compact, 0)


def expert_select(aff_t, cap):
    b, e, l = aff_t.shape
    r = b * e
    tri = jnp.asarray(np.triu(np.ones((SEL_BLK, SEL_BLK), np.float32))).astype(BF16)
    idx, gate = pl.pallas_call(
        functools.partial(_select_kernel, cap=cap),
        grid=(r // SEL_ROWS,),
        in_specs=[pl.BlockSpec((SEL_ROWS, l), lambda i: (i, 0)), pl.BlockSpec((SEL_BLK, SEL_BLK), lambda i: (0, 0))],
        out_specs=[pl.BlockSpec((SEL_ROWS, cap, 1), lambda i: (i, 0, 0))] * 2,
        out_shape=[jax.ShapeDtypeStruct((r, cap, 1), I32), jax.ShapeDtypeStruct((r, cap, 1), F32)],
        scratch_shapes=[pltpu.VMEM((SEL_ROWS, l), F32)],
        compiler_params=_params("parallel"),
        name="expert_select",
    )(aff_t.reshape(r, l), tri)
    return idx.reshape(r, cap), gate.reshape(r, cap)


def _row_copy(src_ref, src_row, dst_ref, dst_row, sem):
    return pltpu.make_async_copy(src_ref.at[pl.ds(src_row, 1)], dst_ref.at[pl.ds(dst_row, 1)], sem)


def _moe_ffn_kernel(idx_ref, x_hbm, wg_ref, wu_ref, wd_ref, o_ref, stage_ref, xb_ref, act_ref, sem,
                    *, nb, cap, seq, n_f):
    e = pl.program_id(0)
    s = pl.program_id(1)
    ne = pl.num_programs(0) - 1
    tf = wg_ref.shape[2]
    slot = e % 2
    f = s // 3
    is_up = s - 3 * f == 0

    def issue(expert, bi):
        base = (bi * ne + expert) * cap

        def body(j, carry):
            _row_copy(x_hbm, bi * seq + idx_ref[base + j], stage_ref, j, sem).start()
            return carry
        lax.fori_loop(0, cap, body, 0, unroll=8)

    def land(dst_slot, bi):
        def body(j, carry):
            _row_copy(x_hbm, 0, stage_ref, j, sem).wait()
            return carry
        lax.fori_loop(0, cap, body, 0, unroll=8)
        xb_ref[dst_slot, bi * cap:(bi + 1) * cap, :] = stage_ref[...].astype(BF16)

    @pl.when(jnp.logical_and(e == 0, s == 0))
    def _():
        for bi in range(nb):
            issue(0, bi)
            land(0, bi)

    tiles_per_batch = n_f // nb
    rows_per_step = cap // tiles_per_batch
    nxt = jnp.minimum(e + 1, ne - 1)
    for bi in range(1, nb):
        @pl.when(jnp.logical_and(e < ne, s == 3 * bi * tiles_per_batch))
        def _(bi=bi):
            land(1 - slot, bi - 1)

    @pl.when(jnp.logical_and(e < ne, s == 3 * n_f - 1))
    def _():
        land(1 - slot, nb - 1)

    @pl.when(jnp.logical_and(e < ne, is_up))
    def _():
        bi = f // tiles_per_batch
        row0 = (f - bi * tiles_per_batch) * rows_per_step
        base = (bi * ne + nxt) * cap + row0
        for j in range(rows_per_step):
            _row_copy(x_hbm, bi * seq + idx_ref[base + j], stage_ref, row0 + j, sem).start()
        x = xb_ref[slot]
        g = jnp.dot(x, wg_ref[0].astype(BF16), preferred_element_type=F32)
        u = jnp.dot(x, wu_ref[0].astype(BF16), preferred_element_type=F32)
        act = (jax.nn.silu(g) * u).astype(BF16)
        for k in range(n_f):
            @pl.when(f == k)
            def _(k=k):
                act_ref[slot, :, k * tf:(k + 1) * tf] = act

    @pl.when(jnp.logical_and(e > 0, jnp.logical_not(is_up)))
    def _():
        a = act_ref[1 - slot]
        o_ref[0] = jnp.dot(a, wd_ref[0].astype(BF16), preferred_element_type=F32).astype(o_ref.dtype)


def moe_ffn(idx, x_rows, w_gate, w_up, w_down, *, layer, nb, seq, tf=256):
    _, ne, d, ff = w_gate.shape
    assert nb == 2
    cap = idx.shape[0] // (nb * ne)
    n_f = ff // tf
    steps = 3 * n_f
    tn = d // (2 * n_f)
    assert tn % LANES == 0 and n_f % nb == 0 and cap % (n_f // nb) == 0

    def up_tile(e, s):
        t = jnp.minimum((e * steps + s + 2) // 3, ne * n_f - 1)
        return t // n_f, t % n_f

    def down_tile(s):
        return 2 * (s // 3) + jnp.maximum(s % 3 - 1, 0)

    def w_up_map(e, s, idx):
        ex, t = up_tile(e, s)
        return (layer, ex, 0, t)

    grid_spec = pltpu.PrefetchScalarGridSpec(
        num_scalar_prefetch=1,
        grid=(ne + 1, steps),
        in_specs=[
            pl.BlockSpec(memory_space=pl.ANY),
            pl.BlockSpec((None, 1, d, tf), w_up_map),
            pl.BlockSpec((None, 1, d, tf), w_up_map),
            pl.BlockSpec((None, 1, ff, tn), lambda e, s, idx: (layer, jnp.maximum(e - 1, 0), 0, down_tile(s))),
        ],
        out_specs=pl.BlockSpec((1, nb * cap, tn),
                               lambda e, s, idx: (jnp.maximum(e - 1, 0), 0, jnp.where(e == 0, 0, down_tile(s)))),
        scratch_shapes=[
            pltpu.VMEM((cap, d), F32),
            pltpu.VMEM((2, nb * cap, d), BF16),
            pltpu.VMEM((2, nb * cap, ff), BF16),
            pltpu.SemaphoreType.DMA(()),
        ],
    )
    return pl.pallas_call(
        functools.partial(_moe_ffn_kernel, nb=nb, cap=cap, seq=seq, n_f=n_f),
        grid_spec=grid_spec,
        out_shape=jax.ShapeDtypeStruct((ne, nb * cap, d), BF16),
        compiler_params=_params("arbitrary", "arbitrary"),
        name="moe_ffn",
    )(idx, x_rows, w_gate, w_up, w_down)


def _moe_scatter_kernel(idx_ref, h_in, ys_ref, gate_ref, mod_ref, h_out, rows_ref, sem_in, sem_out, *, nb, cap, seq):
    del h_in
    e = pl.program_id(0)
    ne = pl.num_programs(0)
    sub = 16

    def for_rows(fn):
        def body(j, carry):
            fn(j)
            return carry
        lax.fori_loop(0, cap, body, 0, unroll=8)

    def token_row(bi, j):
        return bi * seq + idx_ref[(bi * ne + e) * cap + j]

    for bi in range(nb):
        for_rows(lambda j, bi=bi: _row_copy(h_out, token_row(bi, j), rows_ref, bi * cap + j, sem_in.at[bi]).start())
    for bi in range(nb):
        for_rows(lambda j, bi=bi: _row_copy(h_out, 0, rows_ref, bi * cap + j, sem_in.at[bi]).wait())

        def update(i, carry, bi=bi):
            r0 = pl.multiple_of(bi * cap + i * sub, sub)
            w = gate_ref[0, pl.ds(r0, sub), :] * mod_ref[bi:bi + 1, :]
            rows_ref[pl.ds(r0, sub), :] = rows_ref[pl.ds(r0, sub), :] + w * ys_ref[0, pl.ds(r0, sub), :].astype(F32)
            for k in range(sub):
                j = i * sub + k
                _row_copy(rows_ref, bi * cap + j, h_out, token_row(bi, j), sem_out.at[bi]).start()
            return carry
        lax.fori_loop(0, cap // sub, update, 0, unroll=2)
    for bi in range(nb):
        for_rows(lambda j, bi=bi: _row_copy(rows_ref, bi * cap + j, h_out, 0, sem_out.at[bi]).wait())


def moe_scatter(idx, h_rows, ys, gates, mod, *, nb, seq):
    ne, n, d = ys.shape
    cap = n // nb
    grid_spec = pltpu.PrefetchScalarGridSpec(
        num_scalar_prefetch=1,
        grid=(ne,),
        in_specs=[
            pl.BlockSpec(memory_space=pl.ANY),
            pl.BlockSpec((1, n, d), lambda e, idx: (e, 0, 0)),
            pl.BlockSpec((1, n, 1), lambda e, idx: (e, 0, 0)),
            pl.BlockSpec((nb, d), lambda e, idx: (0, 0)),
        ],
        out_specs=pl.BlockSpec(memory_space=pl.ANY),
        scratch_shapes=[pltpu.VMEM((n, d), F32), pltpu.SemaphoreType.DMA((nb,)), pltpu.SemaphoreType.DMA((nb,))],
    )
    return pl.pallas_call(
        functools.partial(_moe_scatter_kernel, nb=nb, cap=cap, seq=seq),
        grid_spec=grid_spec,
        out_shape=jax.ShapeDtypeStruct(h_rows.shape, F32),
        input_output_aliases={1: 0},
        compiler_params=_params("arbitrary"),
        name="moe_scatter",
    )(idx, h_rows, ys, gates, mod)


def expert_choice_moe_residual(h, norm_g, shift, scale, gate_mod, router_w, w_gate, w_up, w_down, layer):
    b, l, d = h.shape
    ne = router_w.shape[1]
    cap = 2 * l // ne
    x, aff_t = norm_modulate_router(h, norm_g, shift, scale, router_w)
    idx, gates = expert_select(aff_t, cap)
    idx_flat = idx.reshape(-1)
    ys = moe_ffn(idx_flat, x.reshape(b * l, d), w_gate, w_up, w_down, layer=layer, nb=b, seq=l)
    gates_e = jnp.transpose(gates.reshape(b, ne, cap), (1, 0, 2)).reshape(ne, b * cap, 1)
    out = moe_scatter(idx_flat, h.reshape(b * l, d), ys, gates_e, gate_mod, nb=b, seq=l)
    return out.reshape(b, l, d)


def kernel(x, c, ctx, c_ctx, ada_w, ada_b, norm_mix_g, norm_ffn_g, attn_sgu_w_in, attn_sink, sgu_norm_g, sgu_w_s, sgu_b_s, attn_sgu_w_out, hyena_w_in, hyena_conv_w, hyena_conv_b, hyena_filt_w1, hyena_filt_b1, hyena_filt_w2, hyena_filt_b2, hyena_filt_freq, hyena_filt_w3, hyena_bias, hyena_w_out, router_w, expert_w_gate, expert_w_up, expert_w_down, final_norm_g):
    b, l, d = x.shape
    lc = ctx.shape[1]
    c_rows = jnp.concatenate([c, c_ctx[None, :], jnp.zeros((8 - b - 1, d), F32)], axis=0)
    mods = ada_modulation(c_rows, ada_w, ada_b)

    def mod(layer, rows, k):
        return mods[layer, rows, k * d:(k + 1) * d]

    lat = slice(0, b)
    ctx_rows = jnp.full((b,), b, I32)

    a_lat = norm_modulate(x, norm_mix_g[0], mod(0, lat, 0), mod(0, lat, 1), BF16)
    a_ctx = norm_modulate(ctx, norm_mix_g[0], mod(0, ctx_rows, 0), mod(0, ctx_rows, 1), BF16)
    w_in = attn_sgu_w_in[0]
    qkvuz = project(a_lat.reshape(b * l, d), w_in).reshape(b, l, -1)
    ctx_kv = project(a_ctx.reshape(b * lc, d), w_in, n_start=ATTN_WIDTH, n_cols=2 * KV_WIDTH).reshape(b, lc, -1)
    qk = rope_qk(qkvuz)
    o = windowed_attention(qk, qkvuz, ctx_kv, attn_sink[0])
    s = spatial_gating(qkvuz, sgu_norm_g[0], sgu_w_s[0], sgu_b_s[0])
    mixed = [o.reshape(b * l, -1), s.reshape(b * l, -1)]
    h = project_residual(mixed, attn_sgu_w_out[0], x.reshape(b * l, d), mod(0, lat, 2), rows_per_batch=l).reshape(b, l, d)
    h = expert_choice_moe_residual(h, norm_ffn_g[0], mod(0, lat, 3), mod(0, lat, 4), mod(0, lat, 5),
                                   router_w[0], expert_w_gate, expert_w_up, expert_w_down, 0)

    a_lat = norm_modulate(h, norm_mix_g[1], mod(1, lat, 0), mod(1, lat, 1), BF16)
    u3 = project(a_lat.reshape(b * l, d), hyena_w_in[0]).reshape(b, l, -1)
    x0, z = hyena_pre(u3, hyena_conv_w[0], hyena_conv_b[0])
    hid = filter_mlp(l, hyena_filt_w1[0], hyena_filt_b1[0], hyena_filt_w2[0], hyena_filt_b2[0], hyena_filt_freq[0])
    xy = hyena_long_conv(z, x0, hid, hyena_filt_w3[0], hyena_bias[0])
    h = project_residual([xy.reshape(b * l, d)], hyena_w_out[0], h.reshape(b * l, d), mod(1, lat, 2), rows_per_batch=l).reshape(b, l, d)
    h = expert_choice_moe_residual(h, norm_ffn_g[1], mod(1, lat, 3), mod(1, lat, 4), mod(1, lat, 5),
                                   router_w[1], expert_w_gate, expert_w_up, expert_w_down, 1)
    return final_norm(h, final_norm_g)
```

```python
import functools
import math

import numpy as np
import jax
import jax.numpy as jnp
from jax import lax
from jax.experimental import pallas as pl
from jax.experimental.pallas import tpu as pltpu

F32 = jnp.float32
BF16 = jnp.bfloat16
I32 = jnp.int32
HIGHEST = lax.Precision.HIGHEST

D_MODEL = 4096
GRID_W = 64
NORM_EPS = 1e-6
N_MOD = 6
NEG_INF = -1e30
HEAD_DIM = 128
ATTN_HEADS = 16
ATTN_KV_HEADS = 4
GQ = 4
ATTN_BLOCK = 128
ROPE_THETA = 10000.0
ATTN_WIDTH = 2048
KV_WIDTH = 512
SGU_WIDTH = 2048
SGU_CHUNK = 128
ROPE_WIDTH = ATTN_WIDTH + KV_WIDTH
HYENA_BANDS = 16
HYENA_EMB = 33
HYENA_HID = 64
N_EXPERTS = 16
EXPERT_FF = 1024

LANES = 128
VMEM_LIMIT_BYTES = 56 * 1024 * 1024

FFT_N1 = 64
FFT_N2 = 128
FFT_N = FFT_N1 * FFT_N2
FFT_HALF = FFT_N1 // 2


def _params(*sem):
    return pltpu.CompilerParams(dimension_semantics=sem, vmem_limit_bytes=VMEM_LIMIT_BYTES)


def _ada_kernel(c_ref, w_ref, b_ref, o_ref):
    s = jax.nn.silu(c_ref[...]).astype(BF16)
    o_ref[0] = jnp.dot(s, w_ref[0].astype(BF16), preferred_element_type=F32) + b_ref[0]


def ada_modulation(c_rows, ada_w, ada_b, tn=512):
    depth, d, n = ada_w.shape
    rows = c_rows.shape[0]
    return pl.pallas_call(
        _ada_kernel,
        grid=(depth, n // tn),
        in_specs=[
            pl.BlockSpec((rows, d), lambda l, j: (0, 0)),
            pl.BlockSpec((1, d, tn), lambda l, j: (l, 0, j)),
            pl.BlockSpec((1, 1, tn), lambda l, j: (l, 0, j)),
        ],
        out_specs=pl.BlockSpec((1, rows, tn), lambda l, j: (l, 0, j)),
        out_shape=jax.ShapeDtypeStruct((depth, rows, n), F32),
        compiler_params=_params("parallel", "parallel"),
        name="ada_modulation",
    )(c_rows, ada_w, ada_b.reshape(depth, 1, n))


def _rms(x, g):
    return x * lax.rsqrt(jnp.mean(x * x, axis=-1, keepdims=True) + NORM_EPS) * g


def _norm_mod_kernel(h_ref, g_ref, sh_ref, sc_ref, o_ref):
    y = _rms(h_ref[0], g_ref[...])
    o_ref[0] = (y * (1.0 + sc_ref[0]) + sh_ref[0]).astype(o_ref.dtype)


def norm_modulate(h, g, shift, scale, out_dtype, tl=256):
    b, l, d = h.shape
    tl = min(tl, l)
    return pl.pallas_call(
        _norm_mod_kernel,
        grid=(b, l // tl),
        in_specs=[
            pl.BlockSpec((1, tl, d), lambda i, j: (i, j, 0)),
            pl.BlockSpec((1, d), lambda i, j: (0, 0)),
            pl.BlockSpec((1, 1, d), lambda i, j: (i, 0, 0)),
            pl.BlockSpec((1, 1, d), lambda i, j: (i, 0, 0)),
        ],
        out_specs=pl.BlockSpec((1, tl, d), lambda i, j: (i, j, 0)),
        out_shape=jax.ShapeDtypeStruct((b, l, d), out_dtype),
        compiler_params=_params("parallel", "parallel"),
        name="norm_modulate",
    )(h, g.reshape(1, d), shift.reshape(b, 1, d), scale.reshape(b, 1, d))


def _norm_mod_router_kernel(h_ref, g_ref, sh_ref, sc_ref, rw_ref, o_ref, aff_ref):
    y = _rms(h_ref[0], g_ref[...])
    a = y * (1.0 + sc_ref[0]) + sh_ref[0]
    o_ref[0] = a
    def split(v):
        hi = v.astype(BF16)
        return hi, (v - hi.astype(F32)).astype(BF16)

    def nt(x, y):
        return lax.dot_general(x, y, (((1,), (1,)), ((), ())), preferred_element_type=F32)

    a_hi, a_lo = split(a)
    r_hi, r_lo = split(rw_ref[...])
    lt = nt(r_hi, a_hi) + (nt(r_hi, a_lo) + nt(r_lo, a_hi))
    m = jnp.max(lt, axis=0, keepdims=True)
    p = jnp.exp(lt - m)
    aff_ref[0] = p / jnp.sum(p, axis=0, keepdims=True)


def norm_modulate_router(h, g, shift, scale, router_w, tl=256):
    b, l, d = h.shape
    e = router_w.shape[1]
    return pl.pallas_call(
        _norm_mod_router_kernel,
        grid=(b, l // tl),
        in_specs=[
            pl.BlockSpec((1, tl, d), lambda i, j: (i, j, 0)),
            pl.BlockSpec((1, d), lambda i, j: (0, 0)),
            pl.BlockSpec((1, 1, d), lambda i, j: (i, 0, 0)),
            pl.BlockSpec((1, 1, d), lambda i, j: (i, 0, 0)),
            pl.BlockSpec((e, d), lambda i, j: (0, 0)),
        ],
        out_specs=[
            pl.BlockSpec((1, tl, d), lambda i, j: (i, j, 0)),
            pl.BlockSpec((1, e, tl), lambda i, j: (i, 0, j)),
        ],
        out_shape=[jax.ShapeDtypeStruct((b, l, d), F32), jax.ShapeDtypeStruct((b, e, l), F32)],
        compiler_params=_params("parallel", "parallel"),
        name="norm_modulate_router",
    )(h, g.reshape(1, d), shift.reshape(b, 1, d), scale.reshape(b, 1, d), router_w.T)


def _final_norm_kernel(h_ref, g_ref, o_ref):
    o_ref[0] = _rms(h_ref[0], g_ref[...])


def final_norm(h, g, tl=256):
    b, l, d = h.shape
    return pl.pallas_call(
        _final_norm_kernel,
        grid=(b, l // tl),
        in_specs=[pl.BlockSpec((1, tl, d), lambda i, j: (i, j, 0)), pl.BlockSpec((1, d), lambda i, j: (0, 0))],
        out_specs=pl.BlockSpec((1, tl, d), lambda i, j: (i, j, 0)),
        out_shape=jax.ShapeDtypeStruct((b, l, d), F32),
        compiler_params=_params("parallel", "parallel"),
        name="final_norm",
    )(h, g.reshape(1, d))


def _refresh_weight(w_hbm, wf_ref, wb_ref, sem, col0, tn):
    j = pl.program_id(0)
    i = pl.program_id(1)

    def w_copy(jj):
        return pltpu.make_async_copy(w_hbm.at[:, pl.ds(pl.multiple_of(col0 + jj * tn, LANES), tn)], wf_ref, sem)

    @pl.when(i == 0)
    def _():
        @pl.when(j == 0)
        def _():
            w_copy(0).start()

        w_copy(j).wait()
        wb_ref[...] = wf_ref[...].astype(BF16)

        @pl.when(j + 1 < pl.num_programs(0))
        def _():
            w_copy(j + 1).start()


def _mm_kernel(a_ref, w_hbm, o_ref, wf_ref, wb_ref, sem, *, col0, tn):
    _refresh_weight(w_hbm, wf_ref, wb_ref, sem, col0, tn)
    o_ref[...] = jnp.dot(a_ref[...], wb_ref[...], preferred_element_type=F32).astype(o_ref.dtype)


def _mm_res_kernel(*refs, n_parts, tn):
    a_refs = refs[:n_parts]
    w_hbm, res_ref, gate_ref, o_ref, wf_ref, wb_ref, sem = refs[n_parts:]
    _refresh_weight(w_hbm, wf_ref, wb_ref, sem, 0, tn)

    acc = None
    k0 = 0
    for a_ref in a_refs:
        kp = a_ref.shape[1]
        part = jnp.dot(a_ref[...], wb_ref[k0:k0 + kp, :], preferred_element_type=F32)
        acc = part if acc is None else acc + part
        k0 += kp
    o_ref[...] = res_ref[...] + gate_ref[0] * acc


def project(a, w, *, n_start=0, n_cols=None, out_dtype=BF16, tm=1024, tn=1024):
    m, k = a.shape
    n_cols = w.shape[1] - n_start if n_cols is None else n_cols
    tm = min(tm, m)
    assert n_cols % tn == 0 and n_start % LANES == 0
    return pl.pallas_call(
        functools.partial(_mm_kernel, col0=n_start, tn=tn),
        grid=(n_cols // tn, m // tm),
        in_specs=[
            pl.BlockSpec((tm, k), lambda j, i: (i, 0)),
            pl.BlockSpec(memory_space=pl.ANY),
        ],
        out_specs=pl.BlockSpec((tm, tn), lambda j, i: (i, j)),
        out_shape=jax.ShapeDtypeStruct((m, n_cols), out_dtype),
        scratch_shapes=[pltpu.VMEM((k, tn), F32), pltpu.VMEM((k, tn), BF16), pltpu.SemaphoreType.DMA(())],
        compiler_params=_params("arbitrary", "arbitrary"),
        name="project",
    )(a, w)


def project_residual(parts, w, res, gate, *, rows_per_batch, tm=512, tn=1024):
    m = parts[0].shape[0]
    k, n = w.shape
    assert sum(p.shape[1] for p in parts) == k
    nb = gate.shape[0]
    assert rows_per_batch % tm == 0, "a row tile must not straddle two batch entries (one gate row per tile)"
    tiles_per_batch = rows_per_batch // tm
    return pl.pallas_call(
        functools.partial(_mm_res_kernel, n_parts=len(parts), tn=tn),
        grid=(n // tn, m // tm),
        in_specs=[pl.BlockSpec((tm, p.shape[1]), lambda j, i: (i, 0)) for p in parts] + [
            pl.BlockSpec(memory_space=pl.ANY),
            pl.BlockSpec((tm, tn), lambda j, i: (i, j)),
            pl.BlockSpec((1, 1, tn), lambda j, i: (i // tiles_per_batch, 0, j)),
        ],
        out_specs=pl.BlockSpec((tm, tn), lambda j, i: (i, j)),
        out_shape=jax.ShapeDtypeStruct((m, n), F32),
        scratch_shapes=[pltpu.VMEM((k, tn), F32), pltpu.VMEM((k, tn), BF16), pltpu.SemaphoreType.DMA(())],
        compiler_params=_params("arbitrary", "arbitrary"),
        name="project_residual",
    )(*parts, w, res, gate.reshape(nb, 1, n))


def _rope_tables(l):
    rows = l // GRID_W
    row = np.repeat(np.arange(rows), GRID_W).astype(np.float32)
    col = np.tile(np.arange(GRID_W), rows).astype(np.float32)
    half = HEAD_DIM // 2
    inv = (ROPE_THETA ** (-np.arange(0, half, 2, dtype=np.float32) / half)).astype(np.float32)
    ang_r = row[:, None] * inv[None, :]
    ang_c = col[:, None] * inv[None, :]
    ang = np.concatenate([ang_r, ang_r, ang_c, ang_c], axis=1)
    sign = np.tile(np.concatenate([-np.ones(half // 2), np.ones(half // 2)]), 2).astype(np.float32)
    return jnp.asarray(np.cos(ang), F32), jnp.asarray(np.sin(ang) * sign[None, :], F32)


def _rope_kernel(x_ref, cos_ref, sin_ref, o_ref):
    cos = cos_ref[...]
    sin = sin_ref[...]
    quarter = HEAD_DIM // 4
    lane = lax.broadcasted_iota(I32, cos.shape, 1)
    first = (lane % (2 * quarter)) < quarter
    for j in range(x_ref.shape[2] // HEAD_DIM):
        sl = slice(j * HEAD_DIM, (j + 1) * HEAD_DIM)
        x = x_ref[0, :, sl].astype(F32)
        partner = jnp.where(first, pltpu.roll(x, HEAD_DIM - quarter, 1), pltpu.roll(x, quarter, 1))
        o_ref[0, :, sl] = (x * cos + partner * sin).astype(o_ref.dtype)


def rope_qk(qkvuz, tl=256):
    b, l, _ = qkvuz.shape
    cos, sin = _rope_tables(l)
    return pl.pallas_call(
        _rope_kernel,
        grid=(b, l // tl),
        in_specs=[
            pl.BlockSpec((1, tl, ROPE_WIDTH), lambda i, j: (i, j, 0)),
            pl.BlockSpec((tl, HEAD_DIM), lambda i, j: (j, 0)),
            pl.BlockSpec((tl, HEAD_DIM), lambda i, j: (j, 0)),
        ],
        out_specs=pl.BlockSpec((1, tl, ROPE_WIDTH), lambda i, j: (i, j, 0)),
        out_shape=jax.ShapeDtypeStruct((b, l, ROPE_WIDTH), BF16),
        compiler_params=_params("parallel", "parallel"),
        name="rope_qk",
    )(qkvuz, cos, sin)


ATTN_QB = 8


def _attn_kernel(sink_ref, q_ref, kp_ref, k0_ref, kn_ref, vp_ref, v0_ref, vn_ref, kx_ref, vx_ref, o_ref):
    h = pl.program_id(1)
    step = pl.program_id(2)
    n_steps = pl.num_programs(2)
    blk = ATTN_BLOCK
    log2e = 1.0 / math.log(2.0)
    scale = HEAD_DIM ** -0.5 * log2e
    rows = lambda ref, j: ref[0, j * blk:(j + 1) * blk, :]
    keys = [kp_ref[0]] + [rows(k0_ref, j) for j in range(ATTN_QB)] + [kn_ref[0]]
    vals = [vp_ref[0]] + [rows(v0_ref, j) for j in range(ATTN_QB)] + [vn_ref[0]]
    qi = lax.broadcasted_iota(I32, (GQ * blk, blk), 0) % blk
    ki = lax.broadcasted_iota(I32, (GQ * blk, blk), 1)
    sink = jnp.concatenate([jnp.full((blk, 1), sink_ref[h * GQ + g] * log2e, F32) for g in range(GQ)], axis=0)

    def fold(parts, op):
        tiles = [p[:, t * blk:(t + 1) * blk] for p in parts for t in range(p.shape[1] // blk)]
        return functools.reduce(op, tiles)

    def pv(p, v):
        return jnp.dot(p.astype(BF16), v, preferred_element_type=F32)

    for j in range(ATTN_QB):
        q = jnp.concatenate([q_ref[0, j * blk:(j + 1) * blk, g * HEAD_DIM:(g + 1) * HEAD_DIM] for g in range(GQ)], axis=0)

        def scores(k):
            return lax.dot_general(q, k, (((1,), (1,)), ((), ())), preferred_element_type=F32) * scale

        sp, s0, sn, sx = scores(keys[j]), scores(keys[j + 1]), scores(keys[j + 2]), scores(kx_ref[0])
        off_p = jnp.where(step > 0, 0, blk) if j == 0 else 0
        off_n = jnp.where(step < n_steps - 1, 0, blk) if j == ATTN_QB - 1 else 0
        sp = jnp.where(ki >= qi + off_p, sp, NEG_INF)
        sn = jnp.where(ki <= qi - off_n, sn, NEG_INF)
        m = jnp.maximum(jnp.max(fold([sp, s0, sn, sx], jnp.maximum), axis=-1, keepdims=True), sink)
        pp, p0, pn, px = jnp.exp2(sp - m), jnp.exp2(s0 - m), jnp.exp2(sn - m), jnp.exp2(sx - m)
        denom = jnp.sum(fold([pp, p0, pn, px], jnp.add), axis=-1, keepdims=True) + jnp.exp2(sink - m)
        o = pv(pp, vals[j]) + pv(p0, vals[j + 1]) + pv(pn, vals[j + 2]) + pv(px, vx_ref[0])
        o = o / denom
        for g in range(GQ):
            o_ref[0, j * blk:(j + 1) * blk, g * HEAD_DIM:(g + 1) * HEAD_DIM] = o[g * blk:(g + 1) * blk].astype(o_ref.dtype)


def windowed_attention(qk, qkvuz, ctx_kv, sink):
    b, l, _ = qk.shape
    lc = ctx_kv.shape[1]
    nb = l // ATTN_BLOCK
    n_steps = nb // ATTN_QB
    kcol = ATTN_WIDTH // HEAD_DIM
    vcol = (ATTN_WIDTH + KV_WIDTH) // HEAD_DIM
    edge = (1, ATTN_BLOCK, HEAD_DIM)
    cur = (1, ATTN_QB * ATTN_BLOCK, HEAD_DIM)
    prev = lambda s: jnp.maximum(s * ATTN_QB - 1, 0)
    nxt = lambda s: jnp.minimum((s + 1) * ATTN_QB, nb - 1)
    return pl.pallas_call(
        _attn_kernel,
        grid=(b, ATTN_KV_HEADS, n_steps),
        in_specs=[
            pl.BlockSpec(memory_space=pltpu.SMEM),
            pl.BlockSpec((1, ATTN_QB * ATTN_BLOCK, GQ * HEAD_DIM), lambda i, h, s: (i, s, h)),
            pl.BlockSpec(edge, lambda i, h, s: (i, prev(s), kcol + h)),
            pl.BlockSpec(cur, lambda i, h, s: (i, s, kcol + h)),
            pl.BlockSpec(edge, lambda i, h, s: (i, nxt(s), kcol + h)),
            pl.BlockSpec(edge, lambda i, h, s: (i, prev(s), vcol + h)),
            pl.BlockSpec(cur, lambda i, h, s: (i, s, vcol + h)),
            pl.BlockSpec(edge, lambda i, h, s: (i, nxt(s), vcol + h)),
            pl.BlockSpec((1, lc, HEAD_DIM), lambda i, h, s: (i, 0, h)),
            pl.BlockSpec((1, lc, HEAD_DIM), lambda i, h, s: (i, 0, ATTN_KV_HEADS + h)),
        ],
        out_specs=pl.BlockSpec((1, ATTN_QB * ATTN_BLOCK, GQ * HEAD_DIM), lambda i, h, s: (i, s, h)),
        out_shape=jax.ShapeDtypeStruct((b, l, ATTN_WIDTH), BF16),
        compiler_params=_params("parallel", "parallel", "parallel"),
        name="windowed_attention",
    )(sink, qk, qk, qk, qk, qkvuz, qkvuz, qkvuz, ctx_kv, ctx_kv)


def _gelu(x):
    return 0.5 * x * (1.0 + lax.erf(x * (1.0 / math.sqrt(2.0))))


SGU_STEP_CHUNKS = 2


def _sgu_kernel(u_ref, z_ref, g_ref, ws_ref, bs_ref, o_ref):
    for ci in range(SGU_STEP_CHUNKS):
        rows = slice(ci * SGU_CHUNK, (ci + 1) * SGU_CHUNK)
        for gi in range(ws_ref.shape[0]):
            sl = slice(gi * LANES, (gi + 1) * LANES)
            z = _gelu(z_ref[0, rows, sl].astype(F32))
            mu = jnp.mean(z, axis=-1, keepdims=True)
            zc = z - mu
            var = jnp.mean(zc * zc, axis=-1, keepdims=True)
            zn = zc * lax.rsqrt(var + NORM_EPS) * g_ref[:, sl]
            mixed = jnp.dot(ws_ref[gi].astype(BF16), zn.astype(BF16), preferred_element_type=F32) + bs_ref[:, sl]
            o_ref[0, rows, sl] = (_gelu(u_ref[0, rows, sl].astype(F32)) * mixed).astype(o_ref.dtype)


def spatial_gating(qkvuz, g, w_s, b_s, tw=1024):
    b, l, _ = qkvuz.shape
    groups = w_s.shape[0]
    ucol = (ATTN_WIDTH + 2 * KV_WIDTH) // tw
    zcol = (ATTN_WIDTH + 2 * KV_WIDTH + SGU_WIDTH) // tw
    gpt = tw // LANES
    tl = SGU_STEP_CHUNKS * SGU_CHUNK
    bias = jnp.repeat(jnp.transpose(b_s), LANES, axis=1)
    return pl.pallas_call(
        _sgu_kernel,
        grid=(SGU_WIDTH // tw, b, l // tl),
        in_specs=[
            pl.BlockSpec((1, tl, tw), lambda j, i, n: (i, n, ucol + j)),
            pl.BlockSpec((1, tl, tw), lambda j, i, n: (i, n, zcol + j)),
            pl.BlockSpec((1, tw), lambda j, i, n: (0, j)),
            pl.BlockSpec((gpt, SGU_CHUNK, SGU_CHUNK), lambda j, i, n: (j, 0, 0)),
            pl.BlockSpec((SGU_CHUNK, tw), lambda j, i, n: (0, j)),
        ],
        out_specs=pl.BlockSpec((1, tl, tw), lambda j, i, n: (i, n, j)),
        out_shape=jax.ShapeDtypeStruct((b, l, SGU_WIDTH), BF16),
        compiler_params=_params("parallel", "parallel", "parallel"),
        name="spatial_gating",
    )(qkvuz, qkvuz, g.reshape(1, groups * LANES), w_s, bias)


HALO = 16


def _hyena_pre_kernel(*refs, tl):
    (p0, m0, n0, p1, m1, n1, p2, m2, n2, w0, w1, w2, b0, b1, b2, x0_ref, z_ref) = refs
    first = pl.program_id(1) == 0
    last = pl.program_id(1) == pl.num_programs(1) - 1
    rows = lax.broadcasted_iota(I32, m0.shape[1:], 0)

    def conv(p_ref, m_ref, n_ref, w_ref, b_ref):
        x = m_ref[0].astype(F32)
        prev_row = jnp.where(first, 0.0, p_ref[0, HALO - 1:HALO, :].astype(F32))
        next_row = jnp.where(last, 0.0, n_ref[0, 0:1, :].astype(F32))
        xm = jnp.where(rows == 0, prev_row, pltpu.roll(x, 1, 0))
        xp = jnp.where(rows == tl - 1, next_row, pltpu.roll(x, tl - 1, 0))
        return xm * w_ref[0:1, :] + x * w_ref[1:2, :] + xp * w_ref[2:3, :] + b_ref[...]

    x0_ref[0] = conv(p0, m0, n0, w0, b0).astype(x0_ref.dtype)
    x1 = conv(p1, m1, n1, w1, b1)
    v = conv(p2, m2, n2, w2, b2)
    z_ref[0] = (v * x1).astype(z_ref.dtype)


def hyena_pre(u3, conv_w, conv_b, tl=1024, tc=512):
    b, l, w3 = u3.shape
    c = w3 // 3
    ct = c // tc
    hb = tl // HALO
    nhb = l // HALO
    in_specs = []
    for part in range(3):
        in_specs += [
            pl.BlockSpec((1, HALO, tc), lambda i, t, j, part=part: (i, jnp.maximum(t * hb - 1, 0), part * ct + j)),
            pl.BlockSpec((1, tl, tc), lambda i, t, j, part=part: (i, t, part * ct + j)),
            pl.BlockSpec((1, HALO, tc), lambda i, t, j, part=part: (i, jnp.minimum((t + 1) * hb, nhb - 1), part * ct + j)),
        ]
    for part in range(3):
        in_specs.append(pl.BlockSpec((3, tc), lambda i, t, j, part=part: (0, part * ct + j)))
    for part in range(3):
        in_specs.append(pl.BlockSpec((1, tc), lambda i, t, j, part=part: (0, part * ct + j)))
    out_spec = pl.BlockSpec((1, tl, tc), lambda i, t, j: (i, t, j))
    return pl.pallas_call(
        functools.partial(_hyena_pre_kernel, tl=tl),
        grid=(b, l // tl, ct),
        in_specs=in_specs,
        out_specs=[out_spec, out_spec],
        out_shape=[jax.ShapeDtypeStruct((b, l, c), BF16), jax.ShapeDtypeStruct((b, l, c), BF16)],
        compiler_params=_params("parallel", "parallel", "parallel"),
        name="hyena_pre",
    )(*([u3] * 9), *([conv_w] * 3), *([conv_b.reshape(1, w3)] * 3))


def _filter_features(l):
    pos = np.arange(l, dtype=np.float32)
    t01 = pos / np.float32(max(l - 1, 1))
    bands = np.linspace(1e-4, HYENA_BANDS - 1, HYENA_BANDS, dtype=np.float32)
    ang = np.float32(2.0 * math.pi / l) * pos[:, None] * bands[None, :]
    feats = np.concatenate([t01[:, None], np.cos(ang), -np.sin(ang)], axis=-1).astype(np.float32)
    feats = np.pad(feats, ((0, 0), (0, HYENA_HID - HYENA_EMB)))
    rev = np.concatenate([np.zeros((1, HYENA_HID), np.float32), feats[:0:-1]], axis=0)
    return np.concatenate([feats, rev], axis=0)


def _filter_mlp_kernel(f_ref, w1_ref, b1_ref, w2_ref, b2_ref, fr_ref, o_ref):
    h = jnp.dot(f_ref[...], w1_ref[...], precision=HIGHEST, preferred_element_type=F32) + b1_ref[...]
    h = jnp.sin(fr_ref[0:1, :] * h)
    h = jnp.dot(h, w2_ref[...], precision=HIGHEST, preferred_element_type=F32) + b2_ref[...]
    o_ref[...] = jnp.sin(fr_ref[1:2, :] * h)


def filter_mlp(l, w1, b1, w2, b2, freq):
    feats = jnp.asarray(_filter_features(l))
    w1p = jnp.pad(w1, ((0, HYENA_HID - HYENA_EMB), (0, 0)))
    hid = pl.pallas_call(
        _filter_mlp_kernel,
        out_shape=jax.ShapeDtypeStruct((2 * l, HYENA_HID), F32),
        compiler_params=pltpu.CompilerParams(vmem_limit_bytes=VMEM_LIMIT_BYTES),
        name="filter_mlp",
    )(feats, w1p, b1.reshape(1, -1), w2, b2.reshape(1, -1), freq)
    return jnp.concatenate([hid[:l], hid[l:]], axis=1)


def _dft_constants():
    n1 = np.arange(FFT_N1, dtype=np.float64)
    n2 = np.arange(FFT_N2, dtype=np.float64)
    k1 = n1
    two_pi = 2.0 * np.pi
    theta = two_pi * (k1[None, :, None] * n1[None, None, :] / FFT_N1 + n2[:, None, None] * k1[None, :, None] / FFT_N)
    gr, gi = np.cos(theta), -np.sin(theta)
    g_cplx = np.concatenate([np.concatenate([gr[:, :, :FFT_HALF], -gi[:, :, :FFT_HALF]], axis=2),
                             np.concatenate([gi[:, :, :FFT_HALF], gr[:, :, :FFT_HALF]], axis=2)], axis=1)
    g_real = np.concatenate([gr, gi], axis=1)
    ang2 = two_pi * np.outer(n2, n2) / FFT_N2
    cm, sm = np.cos(ang2), np.sin(ang2)
    f2 = np.block([[cm, sm], [-sm, cm]])
    f2i = np.block([[cm, -sm], [sm, cm]])
    phi = two_pi * (n1[None, :FFT_HALF, None] * k1[None, None, :] / FFT_N1 + n2[:, None, None] * k1[None, None, :] / FFT_N)
    cp, sp = np.cos(phi) / FFT_N, np.sin(phi) / FFT_N
    hm = np.concatenate([np.concatenate([cp, -sp], axis=2), np.concatenate([sp, cp], axis=2)], axis=1)

    def pair(m):
        p, r, c = m.shape[0] // 2, m.shape[1], m.shape[2]
        out = np.zeros((p, 2 * r, 2 * c), np.float64)
        out[:, :r, :c] = m[0::2]
        out[:, r:, c:] = m[1::2]
        return out

    as_bf16 = lambda a: jnp.asarray(a.astype(np.float32)).astype(BF16)
    return as_bf16(pair(g_cplx)), as_bf16(pair(g_real)), as_bf16(f2), as_bf16(f2i), as_bf16(pair(hm))


def _hyena_fft_kernel(z_ref, x0_ref, hid_ref, w3f_ref, w3b_ref, dl_ref, bias_ref, gc_ref, gk_ref, f2_ref, f2i_ref, hm_ref,
                      o_ref, t_ref, a_ref, kf_ref, b_ref, *, seq):
    two_n1 = 2 * FFT_N1
    two_n2 = 2 * FFT_N2
    ct = dl_ref.shape[1]

    def stage1(g_ref):
        def body(p, carry):
            n2 = 2 * p
            x = jnp.concatenate([t_ref[pl.ds(n2, FFT_N1, stride=FFT_N2), :],
                                 t_ref[pl.ds(n2 + 1, FFT_N1, stride=FFT_N2), :]], axis=0).astype(BF16)
            rows = pl.ds(pl.multiple_of(p * 2 * two_n1, 2 * two_n1), 2 * two_n1)
            a_ref[rows, :] = jnp.dot(g_ref[p], x, preferred_element_type=F32)
            return carry
        lax.fori_loop(0, FFT_N2 // 2, body, 0, unroll=16)

    def stage2(mode):
        def body(k1, carry):
            ar = a_ref[pl.ds(k1, FFT_N2, stride=two_n1), :]
            ai = a_ref[pl.ds(FFT_N1 + k1, FFT_N2, stride=two_n1), :]
            s = jnp.concatenate([ar, ai], axis=0).astype(BF16)
            x = jnp.dot(f2_ref[...], s, preferred_element_type=F32)
            rows = pl.ds(pl.multiple_of(k1 * two_n2, two_n2), two_n2)
            if mode == "filter":
                kf_ref[rows, :] = x.astype(kf_ref.dtype)
            else:
                kf = kf_ref[rows, :].astype(F32)
                xr, xi = x[:FFT_N2], x[FFT_N2:]
                kr, ki = kf[:FFT_N2], kf[FFT_N2:]
                y = jnp.concatenate([xr * kr - xi * ki, xr * ki + xi * kr], axis=0).astype(BF16)
                b_ref[rows, :] = jnp.dot(f2i_ref[...], y, preferred_element_type=F32)
            return carry
        lax.fori_loop(0, FFT_N1, body, 0, unroll=16)

    def stage_out():
        def body(p, carry):
            n2 = 2 * p
            s = jnp.concatenate([b_ref[pl.ds(n2, FFT_N1, stride=two_n2), :],
                                 b_ref[pl.ds(FFT_N2 + n2, FFT_N1, stride=two_n2), :],
                                 b_ref[pl.ds(n2 + 1, FFT_N1, stride=two_n2), :],
                                 b_ref[pl.ds(FFT_N2 + n2 + 1, FFT_N1, stride=two_n2), :]], axis=0).astype(BF16)
            y = jnp.dot(hm_ref[p], s, preferred_element_type=F32)
            for q in range(2):
                lo = q * 2 * FFT_HALF
                t_ref[pl.ds(n2 + q, FFT_HALF, stride=FFT_N2), :] = y[lo:lo + FFT_HALF]
                t_ref[pl.ds(seq + n2 + q, FFT_HALF, stride=FFT_N2), :] = y[lo + FFT_HALF:lo + 2 * FFT_HALF]
            return carry
        lax.fori_loop(0, FFT_N2 // 2, body, 0, unroll=16)

    rb = 512
    span = float(max(seq - 1, 1))

    def build_filter(r, acc):
        r0 = pl.multiple_of(r * rb, rb)
        t_idx = r0 + lax.broadcasted_iota(I32, (rb, ct), 0)
        tf = t_idx.astype(F32)
        hid = hid_ref[pl.ds(r0, rb), :].astype(BF16)
        hf = jnp.dot(hid, w3f_ref[...].astype(BF16), preferred_element_type=F32)
        hf = hf * jnp.exp(-(tf / span) * dl_ref[...])
        hb = jnp.dot(hid, w3b_ref[...].astype(BF16), preferred_element_type=F32)
        hb = jnp.where(t_idx == 0, 0.0, hb * jnp.exp(-((float(seq) - tf) / span) * dl_ref[...]))
        t_ref[pl.ds(r0, rb), :] = hf
        t_ref[pl.ds(seq + r0, rb), :] = hb
        return acc + jnp.sum(jnp.abs(hf), axis=0, keepdims=True) + jnp.sum(jnp.abs(hb), axis=0, keepdims=True)

    norm = lax.fori_loop(0, seq // rb, build_filter, jnp.zeros((1, ct), F32))
    stage1(gk_ref)
    stage2("filter")

    t_ref[0:seq, :] = z_ref[0].astype(F32)
    t_ref[seq:2 * seq, :] = z_ref[1].astype(F32)
    stage1(gc_ref)
    stage2("data")
    stage_out()

    inv_norm = 1.0 / norm
    for bi in range(2):
        def finish(r, carry, bi=bi):
            r0 = pl.multiple_of(r * rb, rb)
            zb = z_ref[bi, pl.ds(r0, rb), :].astype(F32)
            y = t_ref[pl.ds(bi * seq + r0, rb), :] * inv_norm + zb * bias_ref[...]
            o_ref[bi, pl.ds(r0, rb), :] = (x0_ref[bi, pl.ds(r0, rb), :].astype(F32) * y).astype(o_ref.dtype)
            return carry
        lax.fori_loop(0, seq // rb, finish, 0)


def hyena_long_conv(z, x0, hid, w3, bias, ct=128):
    b, l, c = z.shape
    assert b == 2 and 2 * l == FFT_N
    gc, gk, f2, f2i, hm = _dft_constants()
    lo, hi = math.log(1e-2) / 1.5, math.log(1e-2) / 0.3
    deltas = jnp.asarray(np.abs(np.linspace(lo, hi, c, dtype=np.float32)).reshape(1, c))
    zeros = jnp.zeros((HYENA_HID, c), F32)
    w3f = jnp.concatenate([w3[:, :c], zeros], axis=0)
    w3b = jnp.concatenate([zeros, w3[:, c:]], axis=0)
    nct = c // ct
    full = lambda a: pl.BlockSpec(a.shape, lambda j: (0,) * a.ndim, pipeline_mode=pl.Buffered(1))
    return pl.pallas_call(
        functools.partial(_hyena_fft_kernel, seq=l),
        grid=(nct,),
        in_specs=[
            pl.BlockSpec((b, l, ct), lambda j: (0, 0, j)),
            pl.BlockSpec((b, l, ct), lambda j: (0, 0, j)),
            full(hid),
            pl.BlockSpec((2 * HYENA_HID, ct), lambda j: (0, j)),
            pl.BlockSpec((2 * HYENA_HID, ct), lambda j: (0, j)),
            pl.BlockSpec((1, ct), lambda j: (0, j)),
            pl.BlockSpec((1, ct), lambda j: (0, j)),
            full(gc), full(gk), full(f2), full(f2i), full(hm),
        ],
        out_specs=pl.BlockSpec((b, l, ct), lambda j: (0, 0, j)),
        out_shape=jax.ShapeDtypeStruct((b, l, c), BF16),
        scratch_shapes=[
            pltpu.VMEM((2 * l, ct), F32),
            pltpu.VMEM((FFT_N2 * 2 * FFT_N1, ct), F32),
            pltpu.VMEM((FFT_N1 * 2 * FFT_N2, ct), BF16),
            pltpu.VMEM((FFT_N1 * 2 * FFT_N2, ct), F32),
        ],
        compiler_params=_params("parallel"),
        name="hyena_long_conv",
    )(z, x0, hid, w3f, w3b, deltas, bias.reshape(1, c), gc, gk, f2, f2i, hm)


SEL_ROWS = 8
SEL_BLK = 512


def _select_kernel(aff_ref, tri_ref, idx_ref, gate_ref, pos_ref, *, cap):
    x = aff_ref[...]
    rows, l = x.shape

    def search(i, prefix):
        cand = prefix | lax.shift_left(jnp.int32(1), 30 - i)
        cnt = jnp.sum((x >= pltpu.bitcast(cand, F32)).astype(I32), axis=1, keepdims=True)
        return jnp.where(cnt >= cap, cand, prefix)

    tau = pltpu.bitcast(lax.fori_loop(0, 31, search, jnp.zeros((rows, 1), I32)), F32)
    gt = x > tau
    eq = x == tau
    need = (cap - jnp.sum(gt.astype(I32), axis=1, keepdims=True)).astype(F32)

    def prefix_count(mask):
        run = jnp.zeros((rows, 1), F32)
        parts = []
        for blk in range(l // SEL_BLK):
            m = mask[:, blk * SEL_BLK:(blk + 1) * SEL_BLK].astype(BF16)
            c = jnp.dot(m, tri_ref[...], preferred_element_type=F32) + run
            parts.append(c)
            run = c[:, SEL_BLK - 1:SEL_BLK]
        return jnp.concatenate(parts, axis=1)

    eq_rank = prefix_count(jnp.where(eq, 1.0, 0.0))
    sel = jnp.logical_or(gt, jnp.logical_and(eq, eq_rank <= need))
    pos_ref[...] = jnp.where(sel, prefix_count(jnp.where(sel, 1.0, 0.0)), 0.0)

    slot = (lax.broadcasted_iota(I32, (cap, SEL_BLK), 0) + 1).astype(F32)
    tok = lax.broadcasted_iota(I32, (1, SEL_BLK), 1).astype(F32)

    def compact(r, carry):
        acc_i = jnp.zeros((cap, SEL_BLK), F32)
        acc_g = jnp.zeros((cap, SEL_BLK), F32)
        for blk in range(l // SEL_BLK):
            sl = pl.ds(blk * SEL_BLK, SEL_BLK)
            hit = pos_ref[pl.ds(r, 1), sl] == slot
            acc_i = acc_i + jnp.where(hit, tok + float(blk * SEL_BLK), 0.0)
            acc_g = acc_g + jnp.where(hit, aff_ref[pl.ds(r, 1), sl], 0.0)
        idx_ref[r] = jnp.sum(acc_i, axis=1, keepdims=True).astype(I32)
        gate_ref[r] = jnp.sum(acc_g, axis=1, keepdims=True)
        return carry
    lax.fori_loop(0, rows, compact, 0)


def expert_select(aff_t, cap):
    b, e, l = aff_t.shape
    r = b * e
    tri = jnp.asarray(np.triu(np.ones((SEL_BLK, SEL_BLK), np.float32))).astype(BF16)
    idx, gate = pl.pallas_call(
        functools.partial(_select_kernel, cap=cap),
        grid=(r // SEL_ROWS,),
        in_specs=[pl.BlockSpec((SEL_ROWS, l), lambda i: (i, 0)), pl.BlockSpec((SEL_BLK, SEL_BLK), lambda i: (0, 0))],
        out_specs=[pl.BlockSpec((SEL_ROWS, cap, 1), lambda i: (i, 0, 0))] * 2,
        out_shape=[jax.ShapeDtypeStruct((r, cap, 1), I32), jax.ShapeDtypeStruct((r, cap, 1), F32)],
        scratch_shapes=[pltpu.VMEM((SEL_ROWS, l), F32)],
        compiler_params=_params("parallel"),
        name="expert_select",
    )(aff_t.reshape(r, l), tri)
    return idx.reshape(r, cap), gate.reshape(r, cap)


def _row_copy(src_ref, src_row, dst_ref, dst_row, sem):
    return pltpu.make_async_copy(src_ref.at[pl.ds(src_row, 1)], dst_ref.at[pl.ds(dst_row, 1)], sem)


def _moe_ffn_kernel(idx_ref, x_hbm, wg_ref, wu_ref, wd_ref, o_ref, stage_ref, xb_ref, act_ref, sem,
                    *, nb, cap, seq, n_f):
    e = pl.program_id(0)
    s = pl.program_id(1)
    ne = pl.num_programs(0) - 1
    tf = wg_ref.shape[2]
    slot = e % 2
    f = s // 3
    is_up = s - 3 * f == 0

    def issue(expert, bi):
        base = (bi * ne + expert) * cap

        def body(j, carry):
            _row_copy(x_hbm, bi * seq + idx_ref[base + j], stage_ref, j, sem).start()
            return carry
        lax.fori_loop(0, cap, body, 0, unroll=8)

    def land(dst_slot, bi):
        def body(j, carry):
            _row_copy(x_hbm, 0, stage_ref, j, sem).wait()
            return carry
        lax.fori_loop(0, cap, body, 0, unroll=8)
        xb_ref[dst_slot, bi * cap:(bi + 1) * cap, :] = stage_ref[...].astype(BF16)

    @pl.when(jnp.logical_and(e == 0, s == 0))
    def _():
        for bi in range(nb):
            issue(0, bi)
            land(0, bi)

    tiles_per_batch = n_f // nb
    rows_per_step = cap // tiles_per_batch
    nxt = jnp.minimum(e + 1, ne - 1)
    for bi in range(1, nb):
        @pl.when(jnp.logical_and(e < ne, s == 3 * bi * tiles_per_batch))
        def _(bi=bi):
            land(1 - slot, bi - 1)

    @pl.when(jnp.logical_and(e < ne, s == 3 * n_f - 1))
    def _():
        land(1 - slot, nb - 1)

    @pl.when(jnp.logical_and(e < ne, is_up))
    def _():
        bi = f // tiles_per_batch
        row0 = (f - bi * tiles_per_batch) * rows_per_step
        base = (bi * ne + nxt) * cap + row0
        for j in range(rows_per_step):
            _row_copy(x_hbm, bi * seq + idx_ref[base + j], stage_ref, row0 + j, sem).start()
        x = xb_ref[slot]
        g = jnp.dot(x, wg_ref[0].astype(BF16), preferred_element_type=F32)
        u = jnp.dot(x, wu_ref[0].astype(BF16), preferred_element_type=F32)
        act = (jax.nn.silu(g) * u).astype(BF16)
        for k in range(n_f):
            @pl.when(f == k)
            def _(k=k):
                act_ref[slot, :, k * tf:(k + 1) * tf] = act

    @pl.when(jnp.logical_and(e > 0, jnp.logical_not(is_up)))
    def _():
        a = act_ref[1 - slot]
        o_ref[0] = jnp.dot(a, wd_ref[0].astype(BF16), preferred_element_type=F32).astype(o_ref.dtype)


def moe_ffn(idx, x_rows, w_gate, w_up, w_down, *, layer, nb, seq, tf=256):
    _, ne, d, ff = w_gate.shape
    assert nb == 2
    cap = idx.shape[0] // (nb * ne)
    n_f = ff // tf
    steps = 3 * n_f
    tn = d // (2 * n_f)
    assert tn % LANES == 0 and n_f % nb == 0 and cap % (n_f // nb) == 0

    def up_tile(e, s):
        t = jnp.minimum((e * steps + s + 2) // 3, ne * n_f - 1)
        return t // n_f, t % n_f

    def down_tile(s):
        return 2 * (s // 3) + jnp.maximum(s % 3 - 1, 0)

    def w_up_map(e, s, idx):
        ex, t = up_tile(e, s)
        return (layer, ex, 0, t)

    grid_spec = pltpu.PrefetchScalarGridSpec(
        num_scalar_prefetch=1,
        grid=(ne + 1, steps),
        in_specs=[
            pl.BlockSpec(memory_space=pl.ANY),
            pl.BlockSpec((None, 1, d, tf), w_up_map),
            pl.BlockSpec((None, 1, d, tf), w_up_map),
            pl.BlockSpec((None, 1, ff, tn), lambda e, s, idx: (layer, jnp.maximum(e - 1, 0), 0, down_tile(s))),
        ],
        out_specs=pl.BlockSpec((1, nb * cap, tn),
                               lambda e, s, idx: (jnp.maximum(e - 1, 0), 0, jnp.where(e == 0, 0, down_tile(s)))),
        scratch_shapes=[
            pltpu.VMEM((cap, d), F32),
            pltpu.VMEM((2, nb * cap, d), BF16),
            pltpu.VMEM((2, nb * cap, ff), BF16),
            pltpu.SemaphoreType.DMA(()),
        ],
    )
    return pl.pallas_call(
        functools.partial(_moe_ffn_kernel, nb=nb, cap=cap, seq=seq, n_f=n_f),
        grid_spec=grid_spec,
        out_shape=jax.ShapeDtypeStruct((ne, nb * cap, d), BF16),
        compiler_params=_params("arbitrary", "arbitrary"),
        name="moe_ffn",
    )(idx, x_rows, w_gate, w_up, w_down)


def _moe_scatter_kernel(idx_ref, h_in, ys_ref, gate_ref, mod_ref, h_out, rows_ref, sem_in, sem_out, *, nb, cap, seq):
    del h_in
    e = pl.program_id(0)
    ne = pl.num_programs(0)
    sub = 16

    def for_rows(fn):
        def body(j, carry):
            fn(j)
            return carry
        lax.fori_loop(0, cap, body, 0, unroll=8)

    def token_row(bi, j):
        return bi * seq + idx_ref[(bi * ne + e) * cap + j]

    for bi in range(nb):
        for_rows(lambda j, bi=bi: _row_copy(h_out, token_row(bi, j), rows_ref, bi * cap + j, sem_in.at[bi]).start())
    for bi in range(nb):
        for_rows(lambda j, bi=bi: _row_copy(h_out, 0, rows_ref, bi * cap + j, sem_in.at[bi]).wait())

        def update(i, carry, bi=bi):
            r0 = pl.multiple_of(bi * cap + i * sub, sub)
            w = gate_ref[0, pl.ds(r0, sub), :] * mod_ref[bi:bi + 1, :]
            rows_ref[pl.ds(r0, sub), :] = rows_ref[pl.ds(r0, sub), :] + w * ys_ref[0, pl.ds(r0, sub), :].astype(F32)
            for k in range(sub):
                j = i * sub + k
                _row_copy(rows_ref, bi * cap + j, h_out, token_row(bi, j), sem_out.at[bi]).start()
            return carry
        lax.fori_loop(0, cap // sub, update, 0, unroll=2)
    for bi in range(nb):
        for_rows(lambda j, bi=bi: _row_copy(rows_ref, bi * cap + j, h_out, 0, sem_out.at[bi]).wait())


def moe_scatter(idx, h_rows, ys, gates, mod, *, nb, seq):
    ne, n, d = ys.shape
    cap = n // nb
    grid_spec = pltpu.PrefetchScalarGridSpec(
        num_scalar_prefetch=1,
        grid=(ne,),
        in_specs=[
            pl.BlockSpec(memory_space=pl.ANY),
            pl.BlockSpec((1, n, d), lambda e, idx: (e, 0, 0)),
            pl.BlockSpec((1, n, 1), lambda e, idx: (e, 0, 0)),
            pl.BlockSpec((nb, d), lambda e, idx: (0, 0)),
        ],
        out_specs=pl.BlockSpec(memory_space=pl.ANY),
        scratch_shapes=[pltpu.VMEM((n, d), F32), pltpu.SemaphoreType.DMA((nb,)), pltpu.SemaphoreType.DMA((nb,))],
    )
    return pl.pallas_call(
        functools.partial(_moe_scatter_kernel, nb=nb, cap=cap, seq=seq),
        grid_spec=grid_spec,
        out_shape=jax.ShapeDtypeStruct(h_rows.shape, F32),
        input_output_aliases={1: 0},
        compiler_params=_params("arbitrary"),
        name="moe_scatter",
    )(idx, h_rows, ys, gates, mod)


def expert_choice_moe_residual(h, norm_g, shift, scale, gate_mod, router_w, w_gate, w_up, w_down, layer):
    b, l, d = h.shape
    ne = router_w.shape[1]
    cap = 2 * l // ne
    x, aff_t = norm_modulate_router(h, norm_g, shift, scale, router_w)
    idx, gates = expert_select(aff_t, cap)
    idx_flat = idx.reshape(-1)
    ys = moe_ffn(idx_flat, x.reshape(b * l, d), w_gate, w_up, w_down, layer=layer, nb=b, seq=l)
    gates_e = jnp.transpose(gates.reshape(b, ne, cap), (1, 0, 2)).reshape(ne, b * cap, 1)
    out = moe_scatter(idx_flat, h.reshape(b * l, d), ys, gates_e, gate_mod, nb=b, seq=l)
    return out.reshape(b, l, d)


def kernel(x, c, ctx, c_ctx, ada_w, ada_b, norm_mix_g, norm_ffn_g, attn_sgu_w_in, attn_sink, sgu_norm_g, sgu_w_s, sgu_b_s, attn_sgu_w_out, hyena_w_in, hyena_conv_w, hyena_conv_b, hyena_filt_w1, hyena_filt_b1, hyena_filt_w2, hyena_filt_b2, hyena_filt_freq, hyena_filt_w3, hyena_bias, hyena_w_out, router_w, expert_w_gate, expert_w_up, expert_w_down, final_norm_g):
    b, l, d = x.shape
    lc = ctx.shape[1]
    c_rows = jnp.concatenate([c, c_ctx[None, :], jnp.zeros((8 - b - 1, d), F32)], axis=0)
    mods = ada_modulation(c_rows, ada_w, ada_b)

    def mod(layer, rows, k):
        return mods[layer, rows, k * d:(k + 1) * d]

    lat = slice(0, b)
    ctx_rows = jnp.full((b,), b, I32)

    a_lat = norm_modulate(x, norm_mix_g[0], mod(0, lat, 0), mod(0, lat, 1), BF16)
    a_ctx = norm_modulate(ctx, norm_mix_g[0], mod(0, ctx_rows, 0), mod(0, ctx_rows, 1), BF16)
    w_in = attn_sgu_w_in[0]
    qkvuz = project(a_lat.reshape(b * l, d), w_in).reshape(b, l, -1)
    ctx_kv = project(a_ctx.reshape(b * lc, d), w_in, n_start=ATTN_WIDTH, n_cols=2 * KV_WIDTH).reshape(b, lc, -1)
    qk = rope_qk(qkvuz)
    o = windowed_attention(qk, qkvuz, ctx_kv, attn_sink[0])
    s = spatial_gating(qkvuz, sgu_norm_g[0], sgu_w_s[0], sgu_b_s[0])
    mixed = [o.reshape(b * l, -1), s.reshape(b * l, -1)]
    h = project_residual(mixed, attn_sgu_w_out[0], x.reshape(b * l, d), mod(0, lat, 2), rows_per_batch=l).reshape(b, l, d)
    h = expert_choice_moe_residual(h, norm_ffn_g[0], mod(0, lat, 3), mod(0, lat, 4), mod(0, lat, 5),
                                   router_w[0], expert_w_gate, expert_w_up, expert_w_down, 0)

    a_lat = norm_modulate(h, norm_mix_g[1], mod(1, lat, 0), mod(1, lat, 1), BF16)
    u3 = project(a_lat.reshape(b * l, d), hyena_w_in[0]).reshape(b, l, -1)
    x0, z = hyena_pre(u3, hyena_conv_w[0], hyena_conv_b[0])
    hid = filter_mlp(l, hyena_filt_w1[0], hyena_filt_b1[0], hyena_filt_w2[0], hyena_filt_b2[0], hyena_filt_freq[0])
    xy = hyena_long_conv(z, x0, hid, hyena_filt_w3[0], hyena_bias[0])
    h = project_residual([xy.reshape(b * l, d)], hyena_w_out[0], h.reshape(b * l, d), mod(1, lat, 2), rows_per_batch=l).reshape(b, l, d)
    h = expert_choice_moe_residual(h, norm_ffn_g[1], mod(1, lat, 3), mod(1, lat, 4), mod(1, lat, 5),
                                   router_w[1], expert_w_gate, expert_w_up, expert_w_down, 1)
    return final_norm(h, final_norm_g)
```

```python
import functools
import math

import numpy as np
import jax
import jax.numpy as jnp
from jax import lax
from jax.experimental import pallas as pl
from jax.experimental.pallas import tpu as pltpu

F32 = jnp.float32
BF16 = jnp.bfloat16
I32 = jnp.int32
HIGHEST = lax.Precision.HIGHEST

D_MODEL = 4096
GRID_W = 64
NORM_EPS = 1e-6
N_MOD = 6
NEG_INF = -1e30
HEAD_DIM = 128
ATTN_HEADS = 16
ATTN_KV_HEADS = 4
GQ = 4
ATTN_BLOCK = 128
ROPE_THETA = 10000.0
ATTN_WIDTH = 2048
KV_WIDTH = 512
SGU_WIDTH = 2048
SGU_CHUNK = 128
ROPE_WIDTH = ATTN_WIDTH + KV_WIDTH
HYENA_BANDS = 16
HYENA_EMB = 33
HYENA_HID = 64
N_EXPERTS = 16
EXPERT_FF = 1024

LANES = 128
VMEM_LIMIT_BYTES = 56 * 1024 * 1024

FFT_N1 = 64
FFT_N2 = 128
FFT_N = FFT_N1 * FFT_N2
FFT_HALF = FFT_N1 // 2


def _params(*sem):
    return pltpu.CompilerParams(dimension_semantics=sem, vmem_limit_bytes=VMEM_LIMIT_BYTES)


def _ada_kernel(c_ref, w_ref, b_ref, o_ref):
    s = jax.nn.silu(c_ref[...]).astype(BF16)
    o_ref[0] = jnp.dot(s, w_ref[0].astype(BF16), preferred_element_type=F32) + b_ref[0]


def ada_modulation(c_rows, ada_w, ada_b, tn=512):
    depth, d, n = ada_w.shape
    rows = c_rows.shape[0]
    return pl.pallas_call(
        _ada_kernel,
        grid=(depth, n // tn),
        in_specs=[
            pl.BlockSpec((rows, d), lambda l, j: (0, 0)),
            pl.BlockSpec((1, d, tn), lambda l, j: (l, 0, j)),
            pl.BlockSpec((1, 1, tn), lambda l, j: (l, 0, j)),
        ],
        out_specs=pl.BlockSpec((1, rows, tn), lambda l, j: (l, 0, j)),
        out_shape=jax.ShapeDtypeStruct((depth, rows, n), F32),
        compiler_params=_params("parallel", "parallel"),
        name="ada_modulation",
    )(c_rows, ada_w, ada_b.reshape(depth, 1, n))


def _rms(x, g):
    return x * lax.rsqrt(jnp.mean(x * x, axis=-1, keepdims=True) + NORM_EPS) * g


def _norm_mod_kernel(h_ref, g_ref, sh_ref, sc_ref, o_ref):
    y = _rms(h_ref[0], g_ref[...])
    o_ref[0] = (y * (1.0 + sc_ref[0]) + sh_ref[0]).astype(o_ref.dtype)


def norm_modulate(h, g, shift, scale, out_dtype, tl=256):
    b, l, d = h.shape
    tl = min(tl, l)
    return pl.pallas_call(
        _norm_mod_kernel,
        grid=(b, l // tl),
        in_specs=[
            pl.BlockSpec((1, tl, d), lambda i, j: (i, j, 0)),
            pl.BlockSpec((1, d), lambda i, j: (0, 0)),
            pl.BlockSpec((1, 1, d), lambda i, j: (i, 0, 0)),
            pl.BlockSpec((1, 1, d), lambda i, j: (i, 0, 0)),
        ],
        out_specs=pl.BlockSpec((1, tl, d), lambda i, j: (i, j, 0)),
        out_shape=jax.ShapeDtypeStruct((b, l, d), out_dtype),
        compiler_params=_params("parallel", "parallel"),
        name="norm_modulate",
    )(h, g.reshape(1, d), shift.reshape(b, 1, d), scale.reshape(b, 1, d))


def _norm_mod_router_kernel(h_ref, g_ref, sh_ref, sc_ref, rw_ref, o_ref, aff_ref):
    y = _rms(h_ref[0], g_ref[...])
    a = y * (1.0 + sc_ref[0]) + sh_ref[0]
    o_ref[0] = a
    def split(v):
        hi = v.astype(BF16)
        return hi, (v - hi.astype(F32)).astype(BF16)

    def nt(x, y):
        return lax.dot_general(x, y, (((1,), (1,)), ((), ())), preferred_element_type=F32)

    a_hi, a_lo = split(a)
    r_hi, r_lo = split(rw_ref[...])
    lt = nt(r_hi, a_hi) + (nt(r_hi, a_lo) + nt(r_lo, a_hi))
    m = jnp.max(lt, axis=0, keepdims=True)
    p = jnp.exp(lt - m)
    aff_ref[0] = p / jnp.sum(p, axis=0, keepdims=True)


def norm_modulate_router(h, g, shift, scale, router_w, tl=256):
    b, l, d = h.shape
    e = router_w.shape[1]
    return pl.pallas_call(
        _norm_mod_router_kernel,
        grid=(b, l // tl),
        in_specs=[
            pl.BlockSpec((1, tl, d), lambda i, j: (i, j, 0)),
            pl.BlockSpec((1, d), lambda i, j: (0, 0)),
            pl.BlockSpec((1, 1, d), lambda i, j: (i, 0, 0)),
            pl.BlockSpec((1, 1, d), lambda i, j: (i, 0, 0)),
            pl.BlockSpec((e, d), lambda i, j: (0, 0)),
        ],
        out_specs=[
            pl.BlockSpec((1, tl, d), lambda i, j: (i, j, 0)),
            pl.BlockSpec((1, e, tl), lambda i, j: (i, 0, j)),
        ],
        out_shape=[jax.ShapeDtypeStruct((b, l, d), F32), jax.ShapeDtypeStruct((b, e, l), F32)],
        compiler_params=_params("parallel", "parallel"),
        name="norm_modulate_router",
    )(h, g.reshape(1, d), shift.reshape(b, 1, d), scale.reshape(b, 1, d), router_w.T)


def _final_norm_kernel(h_ref, g_ref, o_ref):
    o_ref[0] = _rms(h_ref[0], g_ref[...])


def final_norm(h, g, tl=256):
    b, l, d = h.shape
    return pl.pallas_call(
        _final_norm_kernel,
        grid=(b, l // tl),
        in_specs=[pl.BlockSpec((1, tl, d), lambda i, j: (i, j, 0)), pl.BlockSpec((1, d), lambda i, j: (0, 0))],
        out_specs=pl.BlockSpec((1, tl, d), lambda i, j: (i, j, 0)),
        out_shape=jax.ShapeDtypeStruct((b, l, d), F32),
        compiler_params=_params("parallel", "parallel"),
        name="final_norm",
    )(h, g.reshape(1, d))


def _refresh_weight(w_hbm, wf_ref, wb_ref, sem, col0, tn):
    j = pl.program_id(0)
    i = pl.program_id(1)

    def w_copy(jj):
        return pltpu.make_async_copy(w_hbm.at[:, pl.ds(pl.multiple_of(col0 + jj * tn, LANES), tn)], wf_ref, sem)

    @pl.when(i == 0)
    def _():
        @pl.when(j == 0)
        def _():
            w_copy(0).start()

        w_copy(j).wait()
        wb_ref[...] = wf_ref[...].astype(BF16)

        @pl.when(j + 1 < pl.num_programs(0))
        def _():
            w_copy(j + 1).start()


def _mm_kernel(a_ref, w_hbm, o_ref, wf_ref, wb_ref, sem, *, col0, tn):
    _refresh_weight(w_hbm, wf_ref, wb_ref, sem, col0, tn)
    o_ref[...] = jnp.dot(a_ref[...], wb_ref[...], preferred_element_type=F32).astype(o_ref.dtype)


def _mm_res_kernel(*refs, n_parts, tn):
    a_refs = refs[:n_parts]
    w_hbm, res_ref, gate_ref, o_ref, wf_ref, wb_ref, sem = refs[n_parts:]
    _refresh_weight(w_hbm, wf_ref, wb_ref, sem, 0, tn)

    acc = None
    k0 = 0
    for a_ref in a_refs:
        kp = a_ref.shape[1]
        part = jnp.dot(a_ref[...], wb_ref[k0:k0 + kp, :], preferred_element_type=F32)
        acc = part if acc is None else acc + part
        k0 += kp
    o_ref[...] = res_ref[...] + gate_ref[0] * acc


def project(a, w, *, n_start=0, n_cols=None, out_dtype=BF16, tm=1024, tn=1024):
    m, k = a.shape
    n_cols = w.shape[1] - n_start if n_cols is None else n_cols
    tm = min(tm, m)
    assert n_cols % tn == 0 and n_start % LANES == 0
    return pl.pallas_call(
        functools.partial(_mm_kernel, col0=n_start, tn=tn),
        grid=(n_cols // tn, m // tm),
        in_specs=[
            pl.BlockSpec((tm, k), lambda j, i: (i, 0)),
            pl.BlockSpec(memory_space=pl.ANY),
        ],
        out_specs=pl.BlockSpec((tm, tn), lambda j, i: (i, j)),
        out_shape=jax.ShapeDtypeStruct((m, n_cols), out_dtype),
        scratch_shapes=[pltpu.VMEM((k, tn), F32), pltpu.VMEM((k, tn), BF16), pltpu.SemaphoreType.DMA(())],
        compiler_params=_params("arbitrary", "arbitrary"),
        name="project",
    )(a, w)


def project_residual(parts, w, res, gate, *, rows_per_batch, tm=512, tn=1024):
    m = parts[0].shape[0]
    k, n = w.shape
    assert sum(p.shape[1] for p in parts) == k
    nb = gate.shape[0]
    assert rows_per_batch % tm == 0, "a row tile must not straddle two batch entries (one gate row per tile)"
    tiles_per_batch = rows_per_batch // tm
    return pl.pallas_call(
        functools.partial(_mm_res_kernel, n_parts=len(parts), tn=tn),
        grid=(n // tn, m // tm),
        in_specs=[pl.BlockSpec((tm, p.shape[1]), lambda j, i: (i, 0)) for p in parts] + [
            pl.BlockSpec(memory_space=pl.ANY),
            pl.BlockSpec((tm, tn), lambda j, i: (i, j)),
            pl.BlockSpec((1, 1, tn), lambda j, i: (i // tiles_per_batch, 0, j)),
        ],
        out_specs=pl.BlockSpec((tm, tn), lambda j, i: (i, j)),
        out_shape=jax.ShapeDtypeStruct((m, n), F32),
        scratch_shapes=[pltpu.VMEM((k, tn), F32), pltpu.VMEM((k, tn), BF16), pltpu.SemaphoreType.DMA(())],
        compiler_params=_params("arbitrary", "arbitrary"),
        name="project_residual",
    )(*parts, w, res, gate.reshape(nb, 1, n))


def _rope_tables(l):
    rows = l // GRID_W
    row = np.repeat(np.arange(rows), GRID_W).astype(np.float32)
    col = np.tile(np.arange(GRID_W), rows).astype(np.float32)
    half = HEAD_DIM // 2
    inv = (ROPE_THETA ** (-np.arange(0, half, 2, dtype=np.float32) / half)).astype(np.float32)
    ang_r = row[:, None] * inv[None, :]
    ang_c = col[:, None] * inv[None, :]
    ang = np.concatenate([ang_r, ang_r, ang_c, ang_c], axis=1)
    sign = np.tile(np.concatenate([-np.ones(half // 2), np.ones(half // 2)]), 2).astype(np.float32)
    return jnp.asarray(np.cos(ang), F32), jnp.asarray(np.sin(ang) * sign[None, :], F32)


def _rope_kernel(x_ref, cos_ref, sin_ref, o_ref):
    cos = cos_ref[...]
    sin = sin_ref[...]
    quarter = HEAD_DIM // 4
    lane = lax.broadcasted_iota(I32, cos.shape, 1)
    first = (lane % (2 * quarter)) < quarter
    for j in range(x_ref.shape[2] // HEAD_DIM):
        sl = slice(j * HEAD_DIM, (j + 1) * HEAD_DIM)
        x = x_ref[0, :, sl].astype(F32)
        partner = jnp.where(first, pltpu.roll(x, HEAD_DIM - quarter, 1), pltpu.roll(x, quarter, 1))
        o_ref[0, :, sl] = (x * cos + partner * sin).astype(o_ref.dtype)


def rope_qk(qkvuz, tl=256):
    b, l, _ = qkvuz.shape
    cos, sin = _rope_tables(l)
    return pl.pallas_call(
        _rope_kernel,
        grid=(b, l // tl),
        in_specs=[
            pl.BlockSpec((1, tl, ROPE_WIDTH), lambda i, j: (i, j, 0)),
            pl.BlockSpec((tl, HEAD_DIM), lambda i, j: (j, 0)),
            pl.BlockSpec((tl, HEAD_DIM), lambda i, j: (j, 0)),
        ],
        out_specs=pl.BlockSpec((1, tl, ROPE_WIDTH), lambda i, j: (i, j, 0)),
        out_shape=jax.ShapeDtypeStruct((b, l, ROPE_WIDTH), BF16),
        compiler_params=_params("parallel", "parallel"),
        name="rope_qk",
    )(qkvuz, cos, sin)


ATTN_QB = 8


def _attn_kernel(sink_ref, q_ref, kp_ref, k0_ref, kn_ref, vp_ref, v0_ref, vn_ref, kx_ref, vx_ref, o_ref):
    h = pl.program_id(1)
    step = pl.program_id(2)
    n_steps = pl.num_programs(2)
    blk = ATTN_BLOCK
    log2e = 1.0 / math.log(2.0)
    scale = HEAD_DIM ** -0.5 * log2e
    rows = lambda ref, j: ref[0, j * blk:(j + 1) * blk, :]
    keys = [kp_ref[0]] + [rows(k0_ref, j) for j in range(ATTN_QB)] + [kn_ref[0]]
    vals = [vp_ref[0]] + [rows(v0_ref, j) for j in range(ATTN_QB)] + [vn_ref[0]]
    qi = lax.broadcasted_iota(I32, (GQ * blk, blk), 0) % blk
    ki = lax.broadcasted_iota(I32, (GQ * blk, blk), 1)
    sink = jnp.concatenate([jnp.full((blk, 1), sink_ref[h * GQ + g] * log2e, F32) for g in range(GQ)], axis=0)

    def fold(parts, op):
        tiles = [p[:, t * blk:(t + 1) * blk] for p in parts for t in range(p.shape[1] // blk)]
        return functools.reduce(op, tiles)

    def pv(p, v):
        return jnp.dot(p.astype(BF16), v, preferred_element_type=F32)

    for j in range(ATTN_QB):
        q = jnp.concatenate([q_ref[0, j * blk:(j + 1) * blk, g * HEAD_DIM:(g + 1) * HEAD_DIM] for g in range(GQ)], axis=0)

        def scores(k):
            return lax.dot_general(q, k, (((1,), (1,)), ((), ())), preferred_element_type=F32) * scale

        sp, s0, sn, sx = scores(keys[j]), scores(keys[j + 1]), scores(keys[j + 2]), scores(kx_ref[0])
        off_p = jnp.where(step > 0, 0, blk) if j == 0 else 0
        off_n = jnp.where(step < n_steps - 1, 0, blk) if j == ATTN_QB - 1 else 0
        sp = jnp.where(ki >= qi + off_p, sp, NEG_INF)
        sn = jnp.where(ki <= qi - off_n, sn, NEG_INF)
        m = jnp.maximum(jnp.max(fold([sp, s0, sn, sx], jnp.maximum), axis=-1, keepdims=True), sink)
        pp, p0, pn, px = jnp.exp2(sp - m), jnp.exp2(s0 - m), jnp.exp2(sn - m), jnp.exp2(sx - m)
        denom = jnp.sum(fold([pp, p0, pn, px], jnp.add), axis=-1, keepdims=True) + jnp.exp2(sink - m)
        o = pv(pp, vals[j]) + pv(p0, vals[j + 1]) + pv(pn, vals[j + 2]) + pv(px, vx_ref[0])
        o = o / denom
        for g in range(GQ):
            o_ref[0, j * blk:(j + 1) * blk, g * HEAD_DIM:(g + 1) * HEAD_DIM] = o[g * blk:(g + 1) * blk].astype(o_ref.dtype)


def windowed_attention(qk, qkvuz, ctx_kv, sink):
    b, l, _ = qk.shape
    lc = ctx_kv.shape[1]
    nb = l // ATTN_BLOCK
    n_steps = nb // ATTN_QB
    kcol = ATTN_WIDTH // HEAD_DIM
    vcol = (ATTN_WIDTH + KV_WIDTH) // HEAD_DIM
    edge = (1, ATTN_BLOCK, HEAD_DIM)
    cur = (1, ATTN_QB * ATTN_BLOCK, HEAD_DIM)
    prev = lambda s: jnp.maximum(s * ATTN_QB - 1, 0)
    nxt = lambda s: jnp.minimum((s + 1) * ATTN_QB, nb - 1)
    return pl.pallas_call(
        _attn_kernel,
        grid=(b, ATTN_KV_HEADS, n_steps),
        in_specs=[
            pl.BlockSpec(memory_space=pltpu.SMEM),
            pl.BlockSpec((1, ATTN_QB * ATTN_BLOCK, GQ * HEAD_DIM), lambda i, h, s: (i, s, h)),
            pl.BlockSpec(edge, lambda i, h, s: (i, prev(s), kcol + h)),
            pl.BlockSpec(cur, lambda i, h, s: (i, s, kcol + h)),
            pl.BlockSpec(edge, lambda i, h, s: (i, nxt(s), kcol + h)),
            pl.BlockSpec(edge, lambda i, h, s: (i, prev(s), vcol + h)),
            pl.BlockSpec(cur, lambda i, h, s: (i, s, vcol + h)),
            pl.BlockSpec(edge, lambda i, h, s: (i, nxt(s), vcol + h)),
            pl.BlockSpec((1, lc, HEAD_DIM), lambda i, h, s: (i, 0, h)),
            pl.BlockSpec((1, lc, HEAD_DIM), lambda i, h, s: (i, 0, ATTN_KV_HEADS + h)),
        ],
        out_specs=pl.BlockSpec((1, ATTN_QB * ATTN_BLOCK, GQ * HEAD_DIM), lambda i, h, s: (i, s, h)),
        out_shape=jax.ShapeDtypeStruct((b, l, ATTN_WIDTH), BF16),
        compiler_params=_params("parallel", "parallel", "parallel"),
        name="windowed_attention",
    )(sink, qk, qk, qk, qk, qkvuz, qkvuz, qkvuz, ctx_kv, ctx_kv)


def _gelu(x):
    return 0.5 * x * (1.0 + lax.erf(x * (1.0 / math.sqrt(2.0))))


SGU_STEP_CHUNKS = 2


def _sgu_kernel(u_ref, z_ref, g_ref, ws_ref, bs_ref, o_ref):
    for ci in range(SGU_STEP_CHUNKS):
        rows = slice(ci * SGU_CHUNK, (ci + 1) * SGU_CHUNK)
        for gi in range(ws_ref.shape[0]):
            sl = slice(gi * LANES, (gi + 1) * LANES)
            z = _gelu(z_ref[0, rows, sl].astype(F32))
            mu = jnp.mean(z, axis=-1, keepdims=True)
            zc = z - mu
            var = jnp.mean(zc * zc, axis=-1, keepdims=True)
            zn = zc * lax.rsqrt(var + NORM_EPS) * g_ref[:, sl]
            mixed = jnp.dot(ws_ref[gi].astype(BF16), zn.astype(BF16), preferred_element_type=F32) + bs_ref[:, sl]
            o_ref[0, rows, sl] = (_gelu(u_ref[0, rows, sl].astype(F32)) * mixed).astype(o_ref.dtype)


def spatial_gating(qkvuz, g, w_s, b_s, tw=1024):
    b, l, _ = qkvuz.shape
    groups = w_s.shape[0]
    ucol = (ATTN_WIDTH + 2 * KV_WIDTH) // tw
    zcol = (ATTN_WIDTH + 2 * KV_WIDTH + SGU_WIDTH) // tw
    gpt = tw // LANES
    tl = SGU_STEP_CHUNKS * SGU_CHUNK
    bias = jnp.repeat(jnp.transpose(b_s), LANES, axis=1)
    return pl.pallas_call(
        _sgu_kernel,
        grid=(SGU_WIDTH // tw, b, l // tl),
        in_specs=[
            pl.BlockSpec((1, tl, tw), lambda j, i, n: (i, n, ucol + j)),
            pl.BlockSpec((1, tl, tw), lambda j, i, n: (i, n, zcol + j)),
            pl.BlockSpec((1, tw), lambda j, i, n: (0, j)),
            pl.BlockSpec((gpt, SGU_CHUNK, SGU_CHUNK), lambda j, i, n: (j, 0, 0)),
            pl.BlockSpec((SGU_CHUNK, tw), lambda j, i, n: (0, j)),
        ],
        out_specs=pl.BlockSpec((1, tl, tw), lambda j, i, n: (i, n, j)),
        out_shape=jax.ShapeDtypeStruct((b, l, SGU_WIDTH), BF16),
        compiler_params=_params("parallel", "parallel", "parallel"),
        name="spatial_gating",
    )(qkvuz, qkvuz, g.reshape(1, groups * LANES), w_s, bias)


HALO = 16


def _hyena_pre_kernel(*refs, tl):
    (p0, m0, n0, p1, m1, n1, p2, m2, n2, w0, w1, w2, b0, b1, b2, x0_ref, z_ref) = refs
    first = pl.program_id(1) == 0
    last = pl.program_id(1) == pl.num_programs(1) - 1
    rows = lax.broadcasted_iota(I32, m0.shape[1:], 0)

    def conv(p_ref, m_ref, n_ref, w_ref, b_ref):
        x = m_ref[0].astype(F32)
        prev_row = jnp.where(first, 0.0, p_ref[0, HALO - 1:HALO, :].astype(F32))
        next_row = jnp.where(last, 0.0, n_ref[0, 0:1, :].astype(F32))
        xm = jnp.where(rows == 0, prev_row, pltpu.roll(x, 1, 0))
        xp = jnp.where(rows == tl - 1, next_row, pltpu.roll(x, tl - 1, 0))
        return xm * w_ref[0:1, :] + x * w_ref[1:2, :] + xp * w_ref[2:3, :] + b_ref[...]

    x0_ref[0] = conv(p0, m0, n0, w0, b0).astype(x0_ref.dtype)
    x1 = conv(p1, m1, n1, w1, b1)
    v = conv(p2, m2, n2, w2, b2)
    z_ref[0] = (v * x1).astype(z_ref.dtype)


def hyena_pre(u3, conv_w, conv_b, tl=1024, tc=512):
    b, l, w3 = u3.shape
    c = w3 // 3
    ct = c // tc
    hb = tl // HALO
    nhb = l // HALO
    in_specs = []
    for part in range(3):
        in_specs += [
            pl.BlockSpec((1, HALO, tc), lambda i, t, j, part=part: (i, jnp.maximum(t * hb - 1, 0), part * ct + j)),
            pl.BlockSpec((1, tl, tc), lambda i, t, j, part=part: (i, t, part * ct + j)),
            pl.BlockSpec((1, HALO, tc), lambda i, t, j, part=part: (i, jnp.minimum((t + 1) * hb, nhb - 1), part * ct + j)),
        ]
    for part in range(3):
        in_specs.append(pl.BlockSpec((3, tc), lambda i, t, j, part=part: (0, part * ct + j)))
    for part in range(3):
        in_specs.append(pl.BlockSpec((1, tc), lambda i, t, j, part=part: (0, part * ct + j)))
    out_spec = pl.BlockSpec((1, tl, tc), lambda i, t, j: (i, t, j))
    return pl.pallas_call(
        functools.partial(_hyena_pre_kernel, tl=tl),
        grid=(b, l // tl, ct),
        in_specs=in_specs,
        out_specs=[out_spec, out_spec],
        out_shape=[jax.ShapeDtypeStruct((b, l, c), BF16), jax.ShapeDtypeStruct((b, l, c), BF16)],
        compiler_params=_params("parallel", "parallel", "parallel"),
        name="hyena_pre",
    )(*([u3] * 9), *([conv_w] * 3), *([conv_b.reshape(1, w3)] * 3))


def _filter_features(l):
    pos = np.arange(l, dtype=np.float32)
    t01 = pos / np.float32(max(l - 1, 1))
    bands = np.linspace(1e-4, HYENA_BANDS - 1, HYENA_BANDS, dtype=np.float32)
    ang = np.float32(2.0 * math.pi / l) * pos[:, None] * bands[None, :]
    feats = np.concatenate([t01[:, None], np.cos(ang), -np.sin(ang)], axis=-1).astype(np.float32)
    feats = np.pad(feats, ((0, 0), (0, HYENA_HID - HYENA_EMB)))
    rev = np.concatenate([np.zeros((1, HYENA_HID), np.float32), feats[:0:-1]], axis=0)
    return np.concatenate([feats, rev], axis=0)


def _filter_mlp_kernel(f_ref, w1_ref, b1_ref, w2_ref, b2_ref, fr_ref, o_ref):
    h = jnp.dot(f_ref[...], w1_ref[...], precision=HIGHEST, preferred_element_type=F32) + b1_ref[...]
    h = jnp.sin(fr_ref[0:1, :] * h)
    h = jnp.dot(h, w2_ref[...], precision=HIGHEST, preferred_element_type=F32) + b2_ref[...]
    o_ref[...] = jnp.sin(fr_ref[1:2, :] * h)


def filter_mlp(l, w1, b1, w2, b2, freq):
    feats = jnp.asarray(_filter_features(l))
    w1p = jnp.pad(w1, ((0, HYENA_HID - HYENA_EMB), (0, 0)))
    hid = pl.pallas_call(
        _filter_mlp_kernel,
        out_shape=jax.ShapeDtypeStruct((2 * l, HYENA_HID), F32),
        compiler_params=pltpu.CompilerParams(vmem_limit_bytes=VMEM_LIMIT_BYTES),
        name="filter_mlp",
    )(feats, w1p, b1.reshape(1, -1), w2, b2.reshape(1, -1), freq)
    return jnp.concatenate([hid[:l], hid[l:]], axis=1)


def _dft_constants():
    n1 = np.arange(FFT_N1, dtype=np.float64)
    n2 = np.arange(FFT_N2, dtype=np.float64)
    k1 = n1
    two_pi = 2.0 * np.pi
    theta = two_pi * (k1[None, :, None] * n1[None, None, :] / FFT_N1 + n2[:, None, None] * k1[None, :, None] / FFT_N)
    gr, gi = np.cos(theta), -np.sin(theta)
    g_cplx = np.concatenate([np.concatenate([gr[:, :, :FFT_HALF], -gi[:, :, :FFT_HALF]], axis=2),
                             np.concatenate([gi[:, :, :FFT_HALF], gr[:, :, :FFT_HALF]], axis=2)], axis=1)
    g_real = np.concatenate([gr, gi], axis=1)
    ang2 = two_pi * np.outer(n2, n2) / FFT_N2
    cm, sm = np.cos(ang2), np.sin(ang2)
    f2 = np.block([[cm, sm], [-sm, cm]])
    f2i = np.block([[cm, -sm], [sm, cm]])
    phi = two_pi * (n1[None, :FFT_HALF, None] * k1[None, None, :] / FFT_N1 + n2[:, None, None] * k1[None, None, :] / FFT_N)
    cp, sp = np.cos(phi) / FFT_N, np.sin(phi) / FFT_N
    hm = np.concatenate([np.concatenate([cp, -sp], axis=2), np.concatenate([sp, cp], axis=2)], axis=1)

    def pair(m):
        p, r, c = m.shape[0] // 2, m.shape[1], m.shape[2]
        out = np.zeros((p, 2 * r, 2 * c), np.float64)
        out[:, :r, :c] = m[0::2]
        out[:, r:, c:] = m[1::2]
        return out

    as_bf16 = lambda a: jnp.asarray(a.astype(np.float32)).astype(BF16)
    return as_bf16(pair(g_cplx)), as_bf16(pair(g_real)), as_bf16(f2), as_bf16(f2i), as_bf16(pair(hm))


def _hyena_fft_kernel(z_ref, x0_ref, hid_ref, w3f_ref, w3b_ref, dl_ref, bias_ref, gc_ref, gk_ref, f2_ref, f2i_ref, hm_ref,
                      o_ref, t_ref, a_ref, kf_ref, b_ref, *, seq):
    two_n1 = 2 * FFT_N1
    two_n2 = 2 * FFT_N2
    ct = dl_ref.shape[1]

    def stage1(g_ref):
        def body(p, carry):
            n2 = 2 * p
            x = jnp.concatenate([t_ref[pl.ds(n2, FFT_N1, stride=FFT_N2), :],
                                 t_ref[pl.ds(n2 + 1, FFT_N1, stride=FFT_N2), :]], axis=0).astype(BF16)
            rows = pl.ds(pl.multiple_of(p * 2 * two_n1, 2 * two_n1), 2 * two_n1)
            a_ref[rows, :] = jnp.dot(g_ref[p], x, preferred_element_type=F32)
            return carry
        lax.fori_loop(0, FFT_N2 // 2, body, 0, unroll=16)

    def stage2(mode):
        def body(k1, carry):
            ar = a_ref[pl.ds(k1, FFT_N2, stride=two_n1), :]
            ai = a_ref[pl.ds(FFT_N1 + k1, FFT_N2, stride=two_n1), :]
            s = jnp.concatenate([ar, ai], axis=0).astype(BF16)
            x = jnp.dot(f2_ref[...], s, preferred_element_type=F32)
            rows = pl.ds(pl.multiple_of(k1 * two_n2, two_n2), two_n2)
            if mode == "filter":
                kf_ref[rows, :] = x.astype(kf_ref.dtype)
            else:
                kf = kf_ref[rows, :].astype(F32)
                xr, xi = x[:FFT_N2], x[FFT_N2:]
                kr, ki = kf[:FFT_N2], kf[FFT_N2:]
                y = jnp.concatenate([xr * kr - xi * ki, xr * ki + xi * kr], axis=0).astype(BF16)
                b_ref[rows, :] = jnp.dot(f2i_ref[...], y, preferred_element_type=F32)
            return carry
        lax.fori_loop(0, FFT_N1, body, 0, unroll=16)

    def stage_out():
        def body(p, carry):
            n2 = 2 * p
            s = jnp.concatenate([b_ref[pl.ds(n2, FFT_N1, stride=two_n2), :],
                                 b_ref[pl.ds(FFT_N2 + n2, FFT_N1, stride=two_n2), :],
                                 b_ref[pl.ds(n2 + 1, FFT_N1, stride=two_n2), :],
                                 b_ref[pl.ds(FFT_N2 + n2 + 1, FFT_N1, stride=two_n2), :]], axis=0).astype(BF16)
            y = jnp.dot(hm_ref[p], s, preferred_element_type=F32)
            for q in range(2):
                lo = q * 2 * FFT_HALF
                t_ref[pl.ds(n2 + q, FFT_HALF, stride=FFT_N2), :] = y[lo:lo + FFT_HALF]
                t_ref[pl.ds(seq + n2 + q, FFT_HALF, stride=FFT_N2), :] = y[lo + FFT_HALF:lo + 2 * FFT_HALF]
            return carry
        lax.fori_loop(0, FFT_N2 // 2, body, 0, unroll=16)

    rb = 512
    span = float(max(seq - 1, 1))

    def build_filter(r, acc):
        r0 = pl.multiple_of(r * rb, rb)
        t_idx = r0 + lax.broadcasted_iota(I32, (rb, ct), 0)
        tf = t_idx.astype(F32)
        hid = hid_ref[pl.ds(r0, rb), :].astype(BF16)
        hf = jnp.dot(hid, w3f_ref[...].astype(BF16), preferred_element_type=F32)
        hf = hf * jnp.exp(-(tf / span) * dl_ref[...])
        hb = jnp.dot(hid, w3b_ref[...].astype(BF16), preferred_element_type=F32)
        hb = jnp.where(t_idx == 0, 0.0, hb * jnp.exp(-((float(seq) - tf) / span) * dl_ref[...]))
        t_ref[pl.ds(r0, rb), :] = hf
        t_ref[pl.ds(seq + r0, rb), :] = hb
        return acc + jnp.sum(jnp.abs(hf), axis=0, keepdims=True) + jnp.sum(jnp.abs(hb), axis=0, keepdims=True)

    norm = lax.fori_loop(0, seq // rb, build_filter, jnp.zeros((1, ct), F32))
    stage1(gk_ref)
    stage2("filter")

    t_ref[0:seq, :] = z_ref[0].astype(F32)
    t_ref[seq:2 * seq, :] = z_ref[1].astype(F32)
    stage1(gc_ref)
    stage2("data")
    stage_out()

    inv_norm = 1.0 / norm
    for bi in range(2):
        def finish(r, carry, bi=bi):
            r0 = pl.multiple_of(r * rb, rb)
            zb = z_ref[bi, pl.ds(r0, rb), :].astype(F32)
            y = t_ref[pl.ds(bi * seq + r0, rb), :] * inv_norm + zb * bias_ref[...]
            o_ref[bi, pl.ds(r0, rb), :] = (x0_ref[bi, pl.ds(r0, rb), :].astype(F32) * y).astype(o_ref.dtype)
            return carry
        lax.fori_loop(0, seq // rb, finish, 0)


def hyena_long_conv(z, x0, hid, w3, bias, ct=128):
    b, l, c = z.shape
    assert b == 2 and 2 * l == FFT_N
    gc, gk, f2, f2i, hm = _dft_constants()
    lo, hi = math.log(1e-2) / 1.5, math.log(1e-2) / 0.3
    deltas = jnp.asarray(np.abs(np.linspace(lo, hi, c, dtype=np.float32)).reshape(1, c))
    zeros = jnp.zeros((HYENA_HID, c), F32)
    w3f = jnp.concatenate([w3[:, :c], zeros], axis=0)
    w3b = jnp.concatenate([zeros, w3[:, c:]], axis=0)
    nct = c // ct
    full = lambda a: pl.BlockSpec(a.shape, lambda j: (0,) * a.ndim, pipeline_mode=pl.Buffered(1))
    return pl.pallas_call(
        functools.partial(_hyena_fft_kernel, seq=l),
        grid=(nct,),
        in_specs=[
            pl.BlockSpec((b, l, ct), lambda j: (0, 0, j)),
            pl.BlockSpec((b, l, ct), lambda j: (0, 0, j)),
            full(hid),
            pl.BlockSpec((2 * HYENA_HID, ct), lambda j: (0, j)),
            pl.BlockSpec((2 * HYENA_HID, ct), lambda j: (0, j)),
            pl.BlockSpec((1, ct), lambda j: (0, j)),
            pl.BlockSpec((1, ct), lambda j: (0, j)),
            full(gc), full(gk), full(f2), full(f2i), full(hm),
        ],
        out_specs=pl.BlockSpec((b, l, ct), lambda j: (0, 0, j)),
        out_shape=jax.ShapeDtypeStruct((b, l, c), BF16),
        scratch_shapes=[
            pltpu.VMEM((2 * l, ct), F32),
            pltpu.VMEM((FFT_N2 * 2 * FFT_N1, ct), F32),
            pltpu.VMEM((FFT_N1 * 2 * FFT_N2, ct), BF16),
            pltpu.VMEM((FFT_N1 * 2 * FFT_N2, ct), F32),
        ],
        compiler_params=_params("parallel"),
        name="hyena_long_conv",
    )(z, x0, hid, w3f, w3b, deltas, bias.reshape(1, c), gc, gk, f2, f2i, hm)


SEL_ROWS = 8
SEL_BLK = 512


def _select_kernel(aff_ref, tri_ref, idx_ref, gate_ref, pos_ref, *, cap):
    x = aff_ref[...]
    rows, l = x.shape

    def search(i, prefix):
        cand = prefix | lax.shift_left(jnp.int32(1), 30 - i)
        cnt = jnp.sum((x >= pltpu.bitcast(cand, F32)).astype(I32), axis=1, keepdims=True)
        return jnp.where(cnt >= cap, cand, prefix)

    tau = pltpu.bitcast(lax.fori_loop(0, 31, search, jnp.zeros((rows, 1), I32)), F32)
    gt = x > tau
    eq = x == tau
    need = (cap - jnp.sum(gt.astype(I32), axis=1, keepdims=True)).astype(F32)

    def prefix_count(mask):
        run = jnp.zeros((rows, 1), F32)
        parts = []
        for blk in range(l // SEL_BLK):
            m = mask[:, blk * SEL_BLK:(blk + 1) * SEL_BLK].astype(BF16)
            c = jnp.dot(m, tri_ref[...], preferred_element_type=F32) + run
            parts.append(c)
            run = c[:, SEL_BLK - 1:SEL_BLK]
        return jnp.concatenate(parts, axis=1)

    eq_rank = prefix_count(jnp.where(eq, 1.0, 0.0))
    sel = jnp.logical_or(gt, jnp.logical_and(eq, eq_rank <= need))
    pos_ref[...] = jnp.where(sel, prefix_count(jnp.where(sel, 1.0, 0.0)), 0.0)

    slot = (lax.broadcasted_iota(I32, (cap, SEL_BLK), 0) + 1).astype(F32)
    tok = lax.broadcasted_iota(I32, (1, SEL_BLK), 1).astype(F32)

    def compact(r, carry):
        acc_i = jnp.zeros((cap, SEL_BLK), F32)
        acc_g = jnp.zeros((cap, SEL_BLK), F32)
        for blk in range(l // SEL_BLK):
            sl = pl.ds(blk * SEL_BLK, SEL_BLK)
            hit = pos_ref[pl.ds(r, 1), sl] == slot
            acc_i = acc_i + jnp.where(hit, tok + float(blk * SEL_BLK), 0.0)
            acc_g = acc_g + jnp.where(hit, aff_ref[pl.ds(r, 1), sl], 0.0)
        idx_ref[r] = jnp.sum(acc_i, axis=1, keepdims=True).astype(I32)
        gate_ref[r] = jnp.sum(acc_g, axis=1, keepdims=True)
        return carry
    lax.fori_loop(0, rows, compact, 0)


def expert_select(aff_t, cap):
    b, e, l = aff_t.shape
    r = b * e
    tri = jnp.asarray(np.triu(np.ones((SEL_BLK, SEL_BLK), np.float32))).astype(BF16)
    idx, gate = pl.pallas_call(
        functools.partial(_select_kernel, cap=cap),
        grid=(r // SEL_ROWS,),
        in_specs=[pl.BlockSpec((SEL_ROWS, l), lambda i: (i, 0)), pl.BlockSpec((SEL_BLK, SEL_BLK), lambda i: (0, 0))],
        out_specs=[pl.BlockSpec((SEL_ROWS, cap, 1), lambda i: (i, 0, 0))] * 2,
        out_shape=[jax.ShapeDtypeStruct((r, cap, 1), I32), jax.ShapeDtypeStruct((r, cap, 1), F32)],
        scratch_shapes=[pltpu.VMEM((SEL_ROWS, l), F32)],
        compiler_params=_params("parallel"),
        name="expert_select",
    )(aff_t.reshape(r, l), tri)
    return idx.reshape(r, cap), gate.reshape(r, cap)


def _row_copy(src_ref, src_row, dst_ref, dst_row, sem):
    return pltpu.make_async_copy(src_ref.at[pl.ds(src_row, 1)], dst_ref.at[pl.ds(dst_row, 1)], sem)


def _moe_ffn_kernel(idx_ref, x_hbm, wg_ref, wu_ref, wd_ref, o_ref, stage_ref, xb_ref, act_ref, sem,
                    *, nb, cap, seq, n_f):
    e = pl.program_id(0)
    s = pl.program_id(1)
    ne = pl.num_programs(0) - 1
    tf = wg_ref.shape[2]
    slot = e % 2
    f = s // 3
    is_up = s - 3 * f == 0

    def issue(expert, bi):
        base = (bi * ne + expert) * cap

        def body(j, carry):
            _row_copy(x_hbm, bi * seq + idx_ref[base + j], stage_ref, j, sem).start()
            return carry
        lax.fori_loop(0, cap, body, 0, unroll=8)

    def land(dst_slot, bi):
        def body(j, carry):
            _row_copy(x_hbm, 0, stage_ref, j, sem).wait()
            return carry
        lax.fori_loop(0, cap, body, 0, unroll=8)
        xb_ref[dst_slot, bi * cap:(bi + 1) * cap, :] = stage_ref[...].astype(BF16)

    @pl.when(jnp.logical_and(e == 0, s == 0))
    def _():
        for bi in range(nb):
            issue(0, bi)
            land(0, bi)

    tiles_per_batch = n_f // nb
    rows_per_step = cap // tiles_per_batch
    nxt = jnp.minimum(e + 1, ne - 1)
    for bi in range(1, nb):
        @pl.when(jnp.logical_and(e < ne, s == 3 * bi * tiles_per_batch))
        def _(bi=bi):
            land(1 - slot, bi - 1)

    @pl.when(jnp.logical_and(e < ne, s == 3 * n_f - 1))
    def _():
        land(1 - slot, nb - 1)

    @pl.when(jnp.logical_and(e < ne, is_up))
    def _():
        bi = f // tiles_per_batch
        row0 = (f - bi * tiles_per_batch) * rows_per_step
        base = (bi * ne + nxt) * cap + row0
        for j in range(rows_per_step):
            _row_copy(x_hbm, bi * seq + idx_ref[base + j], stage_ref, row0 + j, sem).start()
        x = xb_ref[slot]
        g = jnp.dot(x, wg_ref[0].astype(BF16), preferred_element_type=F32)
        u = jnp.dot(x, wu_ref[0].astype(BF16), preferred_element_type=F32)
        act = (jax.nn.silu(g) * u).astype(BF16)
        for k in range(n_f):
            @pl.when(f == k)
            def _(k=k):
                act_ref[slot, :, k * tf:(k + 1) * tf] = act

    @pl.when(jnp.logical_and(e > 0, jnp.logical_not(is_up)))
    def _():
        a = act_ref[1 - slot]
        o_ref[0] = jnp.dot(a, wd_ref[0].astype(BF16), preferred_element_type=F32).astype(o_ref.dtype)


def moe_ffn(idx, x_rows, w_gate, w_up, w_down, *, layer, nb, seq, tf=256):
    _, ne, d, ff = w_gate.shape
    assert nb == 2
    cap = idx.shape[0] // (nb * ne)
    n_f = ff // tf
    steps = 3 * n_f
    tn = d // (2 * n_f)
    assert tn % LANES == 0 and n_f % nb == 0 and cap % (n_f // nb) == 0

    def up_tile(e, s):
        t = jnp.minimum((e * steps + s + 2) // 3, ne * n_f - 1)
        return t // n_f, t % n_f

    def down_tile(s):
        return 2 * (s // 3) + jnp.maximum(s % 3 - 1, 0)

    def w_up_map(e, s, idx):
        ex, t = up_tile(e, s)
        return (layer, ex, 0, t)

    grid_spec = pltpu.PrefetchScalarGridSpec(
        num_scalar_prefetch=1,
        grid=(ne + 1, steps),
        in_specs=[
            pl.BlockSpec(memory_space=pl.ANY),
            pl.BlockSpec((None, 1, d, tf), w_up_map),
            pl.BlockSpec((None, 1, d, tf), w_up_map),
            pl.BlockSpec((None, 1, ff, tn), lambda e, s, idx: (layer, jnp.maximum(e - 1, 0), 0, down_tile(s))),
        ],
        out_specs=pl.BlockSpec((1, nb * cap, tn),
                               lambda e, s, idx: (jnp.maximum(e - 1, 0), 0, jnp.where(e == 0, 0, down_tile(s)))),
        scratch_shapes=[
            pltpu.VMEM((cap, d), F32),
            pltpu.VMEM((2, nb * cap, d), BF16),
            pltpu.VMEM((2, nb * cap, ff), BF16),
            pltpu.SemaphoreType.DMA(()),
        ],
    )
    return pl.pallas_call(
        functools.partial(_moe_ffn_kernel, nb=nb, cap=cap, seq=seq, n_f=n_f),
        grid_spec=grid_spec,
        out_shape=jax.ShapeDtypeStruct((ne, nb * cap, d), BF16),
        compiler_params=_params("arbitrary", "arbitrary"),
        name="moe_ffn",
    )(idx, x_rows, w_gate, w_up, w_down)


def _moe_scatter_kernel(idx_ref, h_in, ys_ref, gate_ref, mod_ref, h_out, rows_ref, sem_in, sem_out, *, nb, cap, seq):
    del h_in
    e = pl.program_id(0)
    ne = pl.num_programs(0)
    sub = 16

    def for_rows(fn):
        def body(j, carry):
            fn(j)
            return carry
        lax.fori_loop(0, cap, body, 0, unroll=8)

    def token_row(bi, j):
        return bi * seq + idx_ref[(bi * ne + e) * cap + j]

    def gather_pair(i, carry, bi):
        for prio in range(2):
            j = 2 * i + prio
            _row_copy(h_out, token_row(bi, j), rows_ref, bi * cap + j, sem_in.at[bi]).start(priority=prio)
        return carry

    for bi in range(nb):
        lax.fori_loop(0, cap // 2, functools.partial(gather_pair, bi=bi), 0, unroll=4)
    for bi in range(nb):
        for_rows(lambda j, bi=bi: _row_copy(h_out, 0, rows_ref, bi * cap + j, sem_in.at[bi]).wait())

        def update(i, carry, bi=bi):
            r0 = pl.multiple_of(bi * cap + i * sub, sub)
            w = gate_ref[0, pl.ds(r0, sub), :] * mod_ref[bi:bi + 1, :]
            rows_ref[pl.ds(r0, sub), :] = rows_ref[pl.ds(r0, sub), :] + w * ys_ref[0, pl.ds(r0, sub), :].astype(F32)
            for k in range(sub):
                j = i * sub + k
                _row_copy(rows_ref, bi * cap + j, h_out, token_row(bi, j), sem_out.at[bi]).start(priority=k % 2)
            return carry
        lax.fori_loop(0, cap // sub, update, 0, unroll=2)
    for bi in range(nb):
        for_rows(lambda j, bi=bi: _row_copy(rows_ref, bi * cap + j, h_out, 0, sem_out.at[bi]).wait())


def moe_scatter(idx, h_rows, ys, gates, mod, *, nb, seq):
    ne, n, d = ys.shape
    cap = n // nb
    grid_spec = pltpu.PrefetchScalarGridSpec(
        num_scalar_prefetch=1,
        grid=(ne,),
        in_specs=[
            pl.BlockSpec(memory_space=pl.ANY),
            pl.BlockSpec((1, n, d), lambda e, idx: (e, 0, 0)),
            pl.BlockSpec((1, n, 1), lambda e, idx: (e, 0, 0)),
            pl.BlockSpec((nb, d), lambda e, idx: (0, 0)),
        ],
        out_specs=pl.BlockSpec(memory_space=pl.ANY),
        scratch_shapes=[pltpu.VMEM((n, d), F32), pltpu.SemaphoreType.DMA((nb,)), pltpu.SemaphoreType.DMA((nb,))],
    )
    return pl.pallas_call(
        functools.partial(_moe_scatter_kernel, nb=nb, cap=cap, seq=seq),
        grid_spec=grid_spec,
        out_shape=jax.ShapeDtypeStruct(h_rows.shape, F32),
        input_output_aliases={1: 0},
        compiler_params=_params("arbitrary"),
        name="moe_scatter",
    )(idx, h_rows, ys, gates, mod)


def expert_choice_moe_residual(h, norm_g, shift, scale, gate_mod, router_w, w_gate, w_up, w_down, layer):
    b, l, d = h.shape
    ne = router_w.shape[1]
    cap = 2 * l // ne
    x, aff_t = norm_modulate_router(h, norm_g, shift, scale, router_w)
    idx, gates = expert_select(aff_t, cap)
    idx_flat = idx.reshape(-1)
    ys = moe_ffn(idx_flat, x.reshape(b * l, d), w_gate, w_up, w_down, layer=layer, nb=b, seq=l)
    gates_e = jnp.transpose(gates.reshape(b, ne, cap), (1, 0, 2)).reshape(ne, b * cap, 1)
    out = moe_scatter(idx_flat, h.reshape(b * l, d), ys, gates_e, gate_mod, nb=b, seq=l)
    return out.reshape(b, l, d)


def kernel(x, c, ctx, c_ctx, ada_w, ada_b, norm_mix_g, norm_ffn_g, attn_sgu_w_in, attn_sink, sgu_norm_g, sgu_w_s, sgu_b_s, attn_sgu_w_out, hyena_w_in, hyena_conv_w, hyena_conv_b, hyena_filt_w1, hyena_filt_b1, hyena_filt_w2, hyena_filt_b2, hyena_filt_freq, hyena_filt_w3, hyena_bias, hyena_w_out, router_w, expert_w_gate, expert_w_up, expert_w_down, final_norm_g):
    b, l, d = x.shape
    lc = ctx.shape[1]
    c_rows = jnp.concatenate([c, c_ctx[None, :], jnp.zeros((8 - b - 1, d), F32)], axis=0)
    mods = ada_modulation(c_rows, ada_w, ada_b)

    def mod(layer, rows, k):
        return mods[layer, rows, k * d:(k + 1) * d]

    lat = slice(0, b)
    ctx_rows = jnp.full((b,), b, I32)

    a_lat = norm_modulate(x, norm_mix_g[0], mod(0, lat, 0), mod(0, lat, 1), BF16)
    a_ctx = norm_modulate(ctx, norm_mix_g[0], mod(0, ctx_rows, 0), mod(0, ctx_rows, 1), BF16)
    w_in = attn_sgu_w_in[0]
    qkvuz = project(a_lat.reshape(b * l, d), w_in).reshape(b, l, -1)
    ctx_kv = project(a_ctx.reshape(b * lc, d), w_in, n_start=ATTN_WIDTH, n_cols=2 * KV_WIDTH).reshape(b, lc, -1)
    qk = rope_qk(qkvuz)
    o = windowed_attention(qk, qkvuz, ctx_kv, attn_sink[0])
    s = spatial_gating(qkvuz, sgu_norm_g[0], sgu_w_s[0], sgu_b_s[0])
    mixed = [o.reshape(b * l, -1), s.reshape(b * l, -1)]
    h = project_residual(mixed, attn_sgu_w_out[0], x.reshape(b * l, d), mod(0, lat, 2), rows_per_batch=l).reshape(b, l, d)
    h = expert_choice_moe_residual(h, norm_ffn_g[0], mod(0, lat, 3), mod(0, lat, 4), mod(0, lat, 5),
                                   router_w[0], expert_w_gate, expert_w_up, expert_w_down, 0)

    a_lat = norm_modulate(h, norm_mix_g[1], mod(1, lat, 0), mod(1, lat, 1), BF16)
    u3 = project(a_lat.reshape(b * l, d), hyena_w_in[0]).reshape(b, l, -1)
    x0, z = hyena_pre(u3, hyena_conv_w[0], hyena_conv_b[0])
    hid = filter_mlp(l, hyena_filt_w1[0], hyena_filt_b1[0], hyena_filt_w2[0], hyena_filt_b2[0], hyena_filt_freq[0])
    xy = hyena_long_conv(z, x0, hid, hyena_filt_w3[0], hyena_bias[0])
    h = project_residual([xy.reshape(b * l, d)], hyena_w_out[0], h.reshape(b * l, d), mod(1, lat, 2), rows_per_batch=l).reshape(b, l, d)
    h = expert_choice_moe_residual(h, norm_ffn_g[1], mod(1, lat, 3), mod(1, lat, 4), mod(1, lat, 5),
                                   router_w[1], expert_w_gate, expert_w_up, expert_w_down, 1)
    return final_norm(h, final_norm_g)
```
